```python
import jax
import jax.numpy as jnp
from jax import lax

D_MODEL = 1024
BATCH = 8
SEQ = 2048
DEPTH = 2
DEC_BATCH = 32
DEC_SEQ = 4
PAST_LEN = 8192
PAGE_SIZE = 128

N_MIXERS = 2
N_FOX = (DEPTH + 1) // 2
N_NSA = DEPTH // 2
HEAD_DIM = 64
FOX_HEADS = D_MODEL // HEAD_DIM
NSA_HEADS = D_MODEL // HEAD_DIM
NSA_KV_HEADS = 4
NSA_GROUP = NSA_HEADS // NSA_KV_HEADS
CMP_LEN = 32
CMP_STRIDE = 16
SLC_BLOCK = 64
SLC_TOP = 16
N_LOCAL_BLOCKS = 2
FORCE_SCORE = 1e4
WINDOW = 512
D_FF = 2816
ROPE_THETA = 10000.0
Q_BLOCK = 128
SLC_Q_BLOCK = 32
NORM_EPS = 1e-6
NEG_INF = -1e30
SCALE = HEAD_DIM ** -0.5
FOX_IN = 3 * FOX_HEADS * HEAD_DIM + FOX_HEADS
NSA_Q = NSA_HEADS * HEAD_DIM
NSA_KV = NSA_KV_HEADS * HEAD_DIM
NSA_IN = NSA_Q + 6 * NSA_KV + 3 * NSA_HEADS

kernel_name = 'hybrid_fox_nsa_macaron_step'


def rms_norm(x, g):
    xf = x.astype(jnp.float32)
    y = xf * lax.rsqrt(jnp.mean(xf * xf, axis=-1, keepdims=True) + NORM_EPS)
    return (y * g.astype(jnp.float32)).astype(x.dtype)


def swiglu(x, w_in, w_out):
    gate, up = jnp.split(x @ w_in, 2, axis=-1)
    return (jax.nn.silu(gate) * up) @ w_out


def rope(x, pos):
    half = HEAD_DIM // 2
    inv = ROPE_THETA ** (-jnp.arange(half, dtype=jnp.float32) / half)
    ang = pos.astype(jnp.float32)[:, None] * inv[None, :]
    cos, sin = jnp.cos(ang)[None, :, None, :], jnp.sin(ang)[None, :, None, :]
    xf = x.astype(jnp.float32)
    x1, x2 = xf[..., :half], xf[..., half:]
    return jnp.concatenate([x1 * cos - x2 * sin, x2 * cos + x1 * sin], axis=-1).astype(x.dtype)


def masked_softmax(s, mask):
    s = jnp.where(mask, s, NEG_INF)
    m = jnp.max(s, axis=-1, keepdims=True)
    p = jnp.where(mask, jnp.exp(s - m), 0.0)
    return p / jnp.maximum(jnp.sum(p, axis=-1, keepdims=True), 1e-30)


def gqa_probs(qg, k, mask):
    s = jnp.einsum('...qghd,...kgd->...ghqk', qg.astype(jnp.float32), k.astype(jnp.float32)) * SCALE
    return masked_softmax(s, mask)


def gqa_out(p, v):
    return jnp.einsum('...ghqk,...kgd->...qghd', p, v.astype(jnp.float32))


def sweep_queries(fn, block, xs, pos):
    t = pos.shape[0]
    if t > block and t % block == 0:
        nb = t // block
        split = lambda a: jnp.moveaxis(a.reshape(a.shape[0], nb, block, *a.shape[2:]), 1, 0)
        out = lax.map(fn, (tuple(split(a) for a in xs), pos.reshape(nb, block)))
        return jnp.moveaxis(out, 0, 1).reshape(out.shape[1], t, *out.shape[3:])
    return fn((tuple(xs), pos))


def gather_pages(pool, page_table):
    g = pool[page_table]
    return g.reshape(g.shape[0], -1, *pool.shape[2:])


def fox_attention(q, k, v, c_q, c_k, q_pos, k_pos):
    kf, vf = k.astype(jnp.float32), v.astype(jnp.float32)
    ck_t = jnp.swapaxes(c_k, 1, 2)

    def blk(args):
        (qb, cqb), pb = args
        s = jnp.einsum('bqhd,bkhd->bhqk', qb.astype(jnp.float32), kf) * SCALE
        s = s + jnp.swapaxes(cqb, 1, 2)[..., None] - ck_t[:, :, None, :]
        p = masked_softmax(s, k_pos[None, :] <= pb[:, None])
        return jnp.einsum('bhqk,bkhd->bqhd', p, vf)

    return sweep_queries(blk, Q_BLOCK, (q, c_q), q_pos)


def fox_mixer(h, w_in, b_f, w_out, pos, past):
    n, t, _ = h.shape
    d = FOX_HEADS * HEAD_DIM
    q, k, v, fl = jnp.split(h @ w_in, [d, 2 * d, 3 * d], axis=-1)
    shp = (n, t, FOX_HEADS, HEAD_DIM)
    q, k, v = q.reshape(shp), k.reshape(shp), v.reshape(shp)
    logf = jax.nn.log_sigmoid(fl.astype(jnp.float32) + b_f.astype(jnp.float32))
    if past is None:
        k_all, v_all, lf_all = k, v, logf
    else:
        pk, pv, plf = past
        k_all = jnp.concatenate([pk, k], axis=1)
        v_all = jnp.concatenate([pv, v], axis=1)
        lf_all = jnp.concatenate([plf.astype(jnp.float32), logf], axis=1)
    k_pos = jnp.arange(k_all.shape[1], dtype=jnp.int32)
    c_all = jnp.cumsum(lf_all, axis=1)
    o = fox_attention(q, k_all, v_all, c_all[:, -t:], c_all, pos, k_pos)
    y = o.reshape(n, t, d).astype(h.dtype) @ w_out
    return y, (k, v, logf.astype(h.dtype))


def compress(kseq, pe, w1, w2):
    n, length = kseq.shape[:2]
    r = CMP_LEN // CMP_STRIDE
    n_chunk = length // CMP_STRIDE
    n_cmp = n_chunk - r + 1
    ch = kseq[:, :n_chunk * CMP_STRIDE].reshape(n, n_chunk, CMP_STRIDE, *kseq.shape[2:])
    blocks = jnp.concatenate([ch[:, i:i + n_cmp] for i in range(r)], axis=2)
    blocks = blocks + pe[None, None, :, None, :]
    hid = jax.nn.silu(jnp.einsum('bnjgd,jde->bnge', blocks, w1))
    return hid @ w2


def banded_window(qg, k, v):
    n, t = qg.shape[:2]
    nb, nw = t // Q_BLOCK, WINDOW // Q_BLOCK

    def band(a):
        ap = jnp.pad(a, ((0, 0), (WINDOW, 0), (0, 0), (0, 0))).reshape(n, nb + nw, Q_BLOCK, *a.shape[2:])
        return jnp.concatenate([ap[:, s:s + nb] for s in range(nw + 1)], axis=2)

    kb, vb = band(k), band(v)
    qb = qg.reshape(n, nb, Q_BLOCK, *qg.shape[2:])
    rel_k = jnp.arange((nw + 1) * Q_BLOCK)[None, :] - WINDOW
    dist = jnp.arange(Q_BLOCK)[:, None] - rel_k
    start = jnp.arange(nb) * Q_BLOCK
    mask = (dist >= 0) & (dist < WINDOW) & ((start[:, None, None] + rel_k[None]) >= 0)
    p = gqa_probs(qb, kb, mask[:, None, None])
    return gqa_out(p, vb).reshape(qg.shape)


def nsa_mixer(h, w_in, pe_k, w1_k, w2_k, pe_v, w1_v, w2_v, w_out, pos, past):
    n, t, _ = h.shape
    g_n, hp = NSA_KV_HEADS, NSA_GROUP
    cuts = [NSA_Q + i * NSA_KV for i in range(7)]
    q, kc, vc, ks, vs, kw, vw, gl = jnp.split(h @ w_in, cuts, axis=-1)
    q = rope(q.reshape(n, t, NSA_HEADS, HEAD_DIM), pos).reshape(n, t, g_n, hp, HEAD_DIM)
    kv = lambda a: a.reshape(n, t, g_n, HEAD_DIM)
    kc, ks, kw = rope(kv(kc), pos), rope(kv(ks), pos), rope(kv(kw), pos)
    vc, vs, vw = kv(vc), kv(vs), kv(vw)
    if past is None:
        kc_all, vc_all, ks_all, vs_all = kc, vc, ks, vs
    else:
        pkc, pvc, pks, pvs, bkw, bvw = past
        cat = lambda a, b: jnp.concatenate([a, b], axis=1)
        kc_all, vc_all, ks_all, vs_all = cat(pkc, kc), cat(pvc, vc), cat(pks, ks), cat(pvs, vs)
    length = kc_all.shape[1]

    kcmp = compress(kc_all, pe_k, w1_k, w2_k)
    vcmp = compress(vc_all, pe_v, w1_v, w2_v)
    n_cmp = kcmp.shape[1]
    ci = jnp.arange(n_cmp)
    end = ci * CMP_STRIDE + CMP_LEN - 1
    p_cmp = gqa_probs(q, kcmp, end[None, :] <= pos[:, None])
    o_cmp = gqa_out(p_cmp, vcmp)

    n_sel = -(-length // SLC_BLOCK)
    sj = jnp.arange(n_sel)
    cover = ((ci[:, None] * CMP_STRIDE < (sj[None, :] + 1) * SLC_BLOCK)
             & (ci[:, None] * CMP_STRIDE + CMP_LEN > sj[None, :] * SLC_BLOCK)).astype(jnp.float32)
    imp = jnp.einsum('bghtn,ns->btgs', p_cmp, cover)
    blk_t = (pos // SLC_BLOCK)[:, None]
    valid = sj[None, :] <= blk_t
    forced = valid & ((sj[None, :] == 0) | (blk_t - sj[None, :] < N_LOCAL_BLOCKS))
    score = jnp.where(valid[:, None, :], jnp.where(forced[:, None, :], FORCE_SCORE, imp), -jnp.inf)
    n_top = min(SLC_TOP, n_sel)
    _, idx = lax.top_k(score, n_top)
    pad = n_sel * SLC_BLOCK - length
    blockify = lambda a: jnp.moveaxis(
        jnp.pad(a, ((0, 0), (0, pad), (0, 0), (0, 0))).reshape(n, n_sel, SLC_BLOCK, g_n, HEAD_DIM), 3, 1)
    ks_blk, vs_blk = blockify(ks_all), blockify(vs_all)
    take = jax.vmap(jax.vmap(lambda blk, ix: blk[ix]))

    def slc_fn(args):
        (qb, ib), pb = args
        nq = pb.shape[0]
        ig = jnp.moveaxis(ib, 2, 1)
        kg = take(ks_blk, ig).reshape(n, g_n, nq, n_top * SLC_BLOCK, HEAD_DIM)
        vg = take(vs_blk, ig).reshape(n, g_n, nq, n_top * SLC_BLOCK, HEAD_DIM)
        kpos = (ig[..., None] * SLC_BLOCK + jnp.arange(SLC_BLOCK)).reshape(n, g_n, nq, n_top * SLC_BLOCK)
        s = jnp.einsum('bqghd,bgqkd->bghqk', qb.astype(jnp.float32), kg.astype(jnp.float32)) * SCALE
        p = masked_softmax(s, (kpos <= pb[:, None])[:, :, None])
        return jnp.einsum('bghqk,bgqkd->bqghd', p, vg.astype(jnp.float32))

    o_slc = sweep_queries(slc_fn, SLC_Q_BLOCK, (q, idx), pos)

    if past is None:
        kw_all, vw_all = kw, vw
        o_win = banded_window(q, kw, vw)
    else:
        kw_all = jnp.concatenate([bkw, kw], axis=1)
        vw_all = jnp.concatenate([bvw, vw], axis=1)
        wpos = pos[0] - bkw.shape[1] + jnp.arange(kw_all.shape[1], dtype=jnp.int32)
        dist = pos[:, None] - wpos[None, :]
        o_win = gqa_out(gqa_probs(q, kw_all, (dist >= 0) & (dist < WINDOW)), vw_all)
    keep = min(WINDOW, kw_all.shape[1])

    gates = jax.nn.sigmoid(gl.astype(jnp.float32)).reshape(n, t, 3, g_n, hp)[..., None]
    o = gates[:, :, 0] * o_cmp + gates[:, :, 1] * o_slc + gates[:, :, 2] * o_win
    y = o.reshape(n, t, NSA_Q).astype(h.dtype) @ w_out
    return y, (kc, vc, ks, vs, kw_all[:, -keep:], vw_all[:, -keep:])


def sandwich_macaron(x, ln_pre, ln_post, f1_in, f1_out, f2_in, f2_out, mixer):
    x = x + 0.5 * rms_norm(swiglu(rms_norm(x, ln_pre[0]), f1_in, f1_out), ln_post[0])
    y, new = mixer(rms_norm(x, ln_pre[1]))
    x = x + rms_norm(y, ln_post[1])
    x = x + 0.5 * rms_norm(swiglu(rms_norm(x, ln_pre[2]), f2_in, f2_out), ln_post[2])
    return x, new


def setup_inputs(seed: int = 0) -> dict:
    key = jax.random.key(seed)
    keys = iter(jax.random.split(key, 40))
    f32 = jnp.float32
    normal = lambda shape, scale=1.0: scale * jax.random.normal(next(keys), shape, f32)
    n_pages = PAST_LEN // PAGE_SIZE
    n_used = DEC_BATCH * n_pages
    n_pool = n_used + n_used // 4
    win_buf = min(WINDOW, PAST_LEN)
    d = D_MODEL

    x_prompt = normal((BATCH, SEQ, d))
    x_sample = normal((DEC_BATCH, DEC_SEQ, d))
    fox_b_f = jax.random.uniform(next(keys), (N_FOX, FOX_HEADS), f32, 1.0, 6.0)
    cache_fox_k = normal((N_FOX, n_pool, PAGE_SIZE, FOX_HEADS, HEAD_DIM))
    cache_fox_v = normal((N_FOX, n_pool, PAGE_SIZE, FOX_HEADS, HEAD_DIM))
    cache_fox_logf = jax.nn.log_sigmoid(fox_b_f[:, None, None, :] + normal((N_FOX, n_pool, PAGE_SIZE, FOX_HEADS), 0.5))
    nsa_cache = lambda: normal((N_NSA, n_pool, PAGE_SIZE, NSA_KV_HEADS, HEAD_DIM))
    cache_nsa_cmp_k, cache_nsa_cmp_v = nsa_cache(), nsa_cache()
    cache_nsa_slc_k, cache_nsa_slc_v = nsa_cache(), nsa_cache()
    state_nsa_win_k = normal((N_NSA, DEC_BATCH, win_buf, NSA_KV_HEADS, HEAD_DIM))
    state_nsa_win_v = normal((N_NSA, DEC_BATCH, win_buf, NSA_KV_HEADS, HEAD_DIM))
    page_table = jax.random.permutation(next(keys), n_pool)[:n_used].reshape(DEC_BATCH, n_pages).astype(jnp.int32)

    return {
        'x_prompt': x_prompt, 'x_sample': x_sample,
        'cache_fox_k': cache_fox_k, 'cache_fox_v': cache_fox_v, 'cache_fox_logf': cache_fox_logf,
        'cache_nsa_cmp_k': cache_nsa_cmp_k, 'cache_nsa_cmp_v': cache_nsa_cmp_v,
        'cache_nsa_slc_k': cache_nsa_slc_k, 'cache_nsa_slc_v': cache_nsa_slc_v,
        'state_nsa_win_k': state_nsa_win_k, 'state_nsa_win_v': state_nsa_win_v,
        'page_table': page_table,
        'norm_pre': 1.0 + normal((DEPTH, 3, d), 0.05),
        'norm_post': 1.0 + normal((DEPTH, 3, d), 0.05),
        'ffn1_w_in': normal((DEPTH, d, 2 * D_FF), d ** -0.5),
        'ffn1_w_out': normal((DEPTH, D_FF, d), D_FF ** -0.5),
        'ffn2_w_in': normal((DEPTH, d, 2 * D_FF), d ** -0.5),
        'ffn2_w_out': normal((DEPTH, D_FF, d), D_FF ** -0.5),
        'fox_w_in': normal((N_FOX, d, FOX_IN), d ** -0.5),
        'fox_b_f': fox_b_f,
        'fox_w_out': normal((N_FOX, FOX_HEADS * HEAD_DIM, d), d ** -0.5),
        'nsa_w_in': normal((N_NSA, d, NSA_IN), d ** -0.5),
        'nsa_cmp_pe_k': normal((N_NSA, CMP_LEN, HEAD_DIM), 0.5),
        'nsa_cmp_w1_k': normal((N_NSA, CMP_LEN, HEAD_DIM, HEAD_DIM), (CMP_LEN * HEAD_DIM) ** -0.5),
        'nsa_cmp_w2_k': normal((N_NSA, HEAD_DIM, HEAD_DIM), HEAD_DIM ** -0.5),
        'nsa_cmp_pe_v': normal((N_NSA, CMP_LEN, HEAD_DIM), 0.5),
        'nsa_cmp_w1_v': normal((N_NSA, CMP_LEN, HEAD_DIM, HEAD_DIM), (CMP_LEN * HEAD_DIM) ** -0.5),
        'nsa_cmp_w2_v': normal((N_NSA, HEAD_DIM, HEAD_DIM), HEAD_DIM ** -0.5),
        'nsa_w_out': normal((N_NSA, NSA_Q, d), NSA_Q ** -0.5),
    }


def reference(x_prompt, x_sample, cache_fox_k, cache_fox_v, cache_fox_logf,
              cache_nsa_cmp_k, cache_nsa_cmp_v, cache_nsa_slc_k, cache_nsa_slc_v,
              state_nsa_win_k, state_nsa_win_v, page_table,
              norm_pre, norm_post, ffn1_w_in, ffn1_w_out, ffn2_w_in, ffn2_w_out,
              fox_w_in, fox_b_f, fox_w_out,
              nsa_w_in, nsa_cmp_pe_k, nsa_cmp_w1_k, nsa_cmp_w2_k,
              nsa_cmp_pe_v, nsa_cmp_w1_v, nsa_cmp_w2_v, nsa_w_out):
    pos_p = jnp.arange(x_prompt.shape[1], dtype=jnp.int32)
    pos_s = PAST_LEN + jnp.arange(x_sample.shape[1], dtype=jnp.int32)
    xp, xs = x_prompt, x_sample
    fox_p, fox_s, nsa_p, nsa_s = [], [], [], []
    for layer in range(DEPTH):
        i = layer // N_MIXERS
        lw = (norm_pre[layer], norm_post[layer], ffn1_w_in[layer], ffn1_w_out[layer],
              ffn2_w_in[layer], ffn2_w_out[layer])
        if layer % N_MIXERS == 0:
            wts = (fox_w_in[i], fox_b_f[i], fox_w_out[i])
            past = (gather_pages(cache_fox_k[i], page_table), gather_pages(cache_fox_v[i], page_table),
                    gather_pages(cache_fox_logf[i], page_table))
            xp, new_p = sandwich_macaron(xp, *lw, lambda h: fox_mixer(h, *wts, pos_p, None))
            xs, new_s = sandwich_macaron(xs, *lw, lambda h: fox_mixer(h, *wts, pos_s, past))
            fox_p.append(new_p)
            fox_s.append(new_s)
        else:
            wts = (nsa_w_in[i], nsa_cmp_pe_k[i], nsa_cmp_w1_k[i], nsa_cmp_w2_k[i],
                   nsa_cmp_pe_v[i], nsa_cmp_w1_v[i], nsa_cmp_w2_v[i], nsa_w_out[i])
            past = (gather_pages(cache_nsa_cmp_k[i], page_table), gather_pages(cache_nsa_cmp_v[i], page_table),
                    gather_pages(cache_nsa_slc_k[i], page_table), gather_pages(cache_nsa_slc_v[i], page_table),
                    state_nsa_win_k[i], state_nsa_win_v[i])
            xp, new_p = sandwich_macaron(xp, *lw, lambda h: nsa_mixer(h, *wts, pos_p, None))
            xs, new_s = sandwich_macaron(xs, *lw, lambda h: nsa_mixer(h, *wts, pos_s, past))
            nsa_p.append(new_p)
            nsa_s.append(new_s)
    stk = lambda lst, j: jnp.stack([s[j] for s in lst])
    fox_k_p, fox_v_p, fox_logf_p = stk(fox_p, 0), stk(fox_p, 1), stk(fox_p, 2)
    fox_k_s, fox_v_s, fox_logf_s = stk(fox_s, 0), stk(fox_s, 1), stk(fox_s, 2)
    nsa_cmp_k_p, nsa_cmp_v_p, nsa_slc_k_p = stk(nsa_p, 0), stk(nsa_p, 1), stk(nsa_p, 2)
    nsa_slc_v_p, nsa_win_k_p, nsa_win_v_p = stk(nsa_p, 3), stk(nsa_p, 4), stk(nsa_p, 5)
    nsa_cmp_k_s, nsa_cmp_v_s, nsa_slc_k_s = stk(nsa_s, 0), stk(nsa_s, 1), stk(nsa_s, 2)
    nsa_slc_v_s, nsa_win_k_s, nsa_win_v_s = stk(nsa_s, 3), stk(nsa_s, 4), stk(nsa_s, 5)
    return (xp, xs, fox_k_p, fox_v_p, fox_logf_p, fox_k_s, fox_v_s, fox_logf_s,
            nsa_cmp_k_p, nsa_cmp_v_p, nsa_slc_k_p, nsa_slc_v_p, nsa_win_k_p, nsa_win_v_p,
            nsa_cmp_k_s, nsa_cmp_v_s, nsa_slc_k_s, nsa_slc_v_s, nsa_win_k_s, nsa_win_v_s)
```

```python
import functools

import jax
import jax.numpy as jnp
from jax import lax
from jax.experimental import pallas as pl
from jax.experimental.pallas import tpu as pltpu

F32 = jnp.float32
BF16 = jnp.bfloat16

HEAD_DIM = 64
FOX_HEADS = 16
NSA_HEADS = 16
NSA_KV_HEADS = 4
NSA_GROUP = NSA_HEADS // NSA_KV_HEADS
CMP_LEN = 32
CMP_STRIDE = 16
SLC_BLOCK = 64
SLC_TOP = 16
N_LOCAL_BLOCKS = 2
FORCE_SCORE = 1e4
WINDOW = 512
ROPE_THETA = 10000.0
NORM_EPS = 1e-6
NEG_INF = -1e30
SCALE = HEAD_DIM ** -0.5
PAGE_SIZE = 128

LANES = 128
SUBLANES = 8
VMEM_LIMIT = 56 * 1024 * 1024

PAGES_PER_STEP = 8
CHUNKS_PER_PAGE = PAGE_SIZE // CMP_STRIDE
CHUNK_LANES = CMP_STRIDE * NSA_KV_HEADS * HEAD_DIM


def _params(*sem):
    return pltpu.CompilerParams(dimension_semantics=sem, vmem_limit_bytes=VMEM_LIMIT)


def _dot(a, b):
    return jnp.dot(a, b, preferred_element_type=F32)


def _dot_nt(a, b):
    return lax.dot_general(a, b, (((1,), (1,)), ((), ())), preferred_element_type=F32)


def _split3(x):
    hi = x.astype(BF16)
    r1 = x - hi.astype(F32)
    mid = r1.astype(BF16)
    lo = (r1 - mid.astype(F32)).astype(BF16)
    return hi, mid, lo


def _dot3_l(a, b):
    hi, mid, lo = _split3(a)
    return _dot(hi, b) + _dot(mid, b) + _dot(lo, b)


def _dot3_r(a, b):
    hi, mid, lo = _split3(b)
    return _dot(a, hi) + _dot(a, mid) + _dot(a, lo)


def _dot3_nt_r(a, b):
    hi, mid, lo = _split3(b)
    return _dot_nt(a, hi) + _dot_nt(a, mid) + _dot_nt(a, lo)


def _rms(x, g):
    return x * lax.rsqrt(jnp.mean(x * x, axis=-1, keepdims=True) + NORM_EPS) * g


def _silu(x):
    return x * jax.nn.sigmoid(x)


def _iota(shape, dim):
    return lax.broadcasted_iota(jnp.int32, shape, dim)


def _masked_softmax(s, mask):
    s = jnp.where(mask, s, NEG_INF)
    m = jnp.max(s, axis=-1, keepdims=True)
    p = jnp.where(mask, jnp.exp(s - m), 0.0)
    return p / jnp.maximum(jnp.sum(p, axis=-1, keepdims=True), 1e-30)


def _online_update(s, mask, v, m_ref, l_ref, acc_ref):
    if mask is not None:
        s = jnp.where(mask, s, NEG_INF)
    m_old = m_ref[...]
    m_new = jnp.maximum(m_old, jnp.max(s, axis=-1, keepdims=True))
    p = jnp.exp(s - m_new)
    if mask is not None:
        p = jnp.where(mask, p, 0.0)
    alpha = jnp.exp(m_old - m_new)
    l_ref[...] = alpha * l_ref[...] + jnp.sum(p, axis=-1, keepdims=True)
    acc_ref[...] = alpha * acc_ref[...] + _dot(p.astype(BF16), v)
    m_ref[...] = m_new


def _top_mask(score, n_real, n_top):
    lane = _iota(score.shape, 1)
    rank = jnp.zeros(score.shape, F32)
    for i in range(n_real):
        col = score[:, i:i + 1]
        tie = jnp.where(lane > i, 1.0, 0.0)
        rank = rank + jnp.where(col > score, 1.0, jnp.where(col == score, tie, 0.0))
    return jnp.where((rank < n_top) & (lane < n_real), 1.0, 0.0)


def _cover_matrix(n_cmp_pad, n_sel_pad, n_cmp, n_sel):
    ci = _iota((n_cmp_pad, n_sel_pad), 0)
    sj = _iota((n_cmp_pad, n_sel_pad), 1)
    hit = ((ci * CMP_STRIDE < (sj + 1) * SLC_BLOCK) & (ci * CMP_STRIDE + CMP_LEN > sj * SLC_BLOCK)
           & (ci < n_cmp) & (sj < n_sel))
    return jnp.where(hit, 1.0, 0.0).astype(BF16)


def _selection_scores(imp, pos_col, n_sel):
    sj = _iota(imp.shape, 1)
    blk_t = pos_col // SLC_BLOCK
    valid = (sj <= blk_t) & (sj < n_sel)
    forced = (sj == 0) | (blk_t - sj < N_LOCAL_BLOCKS)
    return jnp.where(valid, jnp.where(forced, FORCE_SCORE, imp), -jnp.inf)


def _ffn_kernel(x_ref, gpre_ref, gpost_ref, wg_ref, wu_ref, wo_ref, o_ref, h_ref, acc_ref):
    f = pl.program_id(1)

    @pl.when(f == 0)
    def _():
        h_ref[...] = _rms(x_ref[...], gpre_ref[...]).astype(BF16)
        acc_ref[...] = jnp.zeros_like(acc_ref)

    h = h_ref[...]
    gate = _dot(h, wg_ref[...])
    up = _dot(h, wu_ref[...])
    acc_ref[...] += _dot((_silu(gate) * up).astype(BF16), wo_ref[...])

    @pl.when(f == pl.num_programs(1) - 1)
    def _():
        o_ref[...] = x_ref[...] + 0.5 * _rms(acc_ref[...], gpost_ref[...])


def _ffn(x, g_pre, g_post, w_in, w_out, *, tm, tf=256):
    rows, d = x.shape
    d_ff = w_out.shape[0]
    nf = d_ff // tf
    assert rows % tm == 0 and d_ff % tf == 0
    return pl.pallas_call(
        _ffn_kernel,
        grid=(rows // tm, nf),
        in_specs=[
            pl.BlockSpec((tm, d), lambda i, f: (i, 0)),
            pl.BlockSpec((1, d), lambda i, f: (0, 0)),
            pl.BlockSpec((1, d), lambda i, f: (0, 0)),
            pl.BlockSpec((d, tf), lambda i, f: (0, f)),
            pl.BlockSpec((d, tf), lambda i, f: (0, nf + f)),
            pl.BlockSpec((tf, d), lambda i, f: (f, 0)),
        ],
        out_specs=pl.BlockSpec((tm, d), lambda i, f: (i, 0)),
        out_shape=jax.ShapeDtypeStruct((rows, d), F32),
        scratch_shapes=[pltpu.VMEM((tm, d), BF16), pltpu.VMEM((tm, d), F32)],
        compiler_params=_params("parallel", "arbitrary"),
        name="ffn",
    )(x, g_pre, g_post, w_in, w_in, w_out)


def _oproj_kernel(o_ref, x_ref, w_ref, g_ref, out_ref):
    y = _dot(o_ref[...].astype(BF16), w_ref[...])
    out_ref[...] = x_ref[...] + _rms(y, g_ref[...])


def _oproj(o, x, w, g, *, tm):
    rows, d = x.shape
    return pl.pallas_call(
        _oproj_kernel,
        grid=(rows // tm,),
        in_specs=[
            pl.BlockSpec((tm, o.shape[1]), lambda i: (i, 0)),
            pl.BlockSpec((tm, d), lambda i: (i, 0)),
            pl.BlockSpec(w.shape, lambda i: (0, 0)),
            pl.BlockSpec((1, d), lambda i: (0, 0)),
        ],
        out_specs=pl.BlockSpec((tm, d), lambda i: (i, 0)),
        out_shape=jax.ShapeDtypeStruct((rows, d), F32),
        compiler_params=_params("parallel"),
        name="oproj",
    )(o, x, w, g)


def _log_sigmoid(z):
    return jnp.minimum(z, 0.0) - jnp.log1p(jnp.exp(-jnp.abs(z)))


def _fox_proj_kernel(x_ref, g_ref, wqkv_ref, wf_ref, bf_ref, q_ref, k_ref, v_ref, lf_ref):
    h = _rms(x_ref[...], g_ref[...]).astype(BF16)
    d = q_ref.shape[1]
    qkv = _dot(h, wqkv_ref[...])
    q_ref[...] = qkv[:, :d]
    k_ref[...] = qkv[:, d:2 * d]
    v_ref[...] = qkv[:, 2 * d:]
    fl = _dot(h, wf_ref[...])
    lf_ref[...] = _log_sigmoid(fl[:, :FOX_HEADS] + bf_ref[...])


def _fox_proj(x, g, wqkv, wf, bf, *, tm):
    rows, d = x.shape
    row_spec = pl.BlockSpec((tm, d), lambda i: (i, 0))
    full = lambda a: pl.BlockSpec(a.shape, lambda i: (0, 0))
    return pl.pallas_call(
        _fox_proj_kernel,
        grid=(rows // tm,),
        in_specs=[row_spec, full(g), full(wqkv), full(wf), full(bf)],
        out_specs=[row_spec, row_spec, row_spec, pl.BlockSpec((tm, FOX_HEADS), lambda i: (i, 0))],
        out_shape=[jax.ShapeDtypeStruct((rows, d), F32)] * 3 + [jax.ShapeDtypeStruct((rows, FOX_HEADS), F32)],
        compiler_params=_params("parallel"),
        name="fox_proj",
    )(x, g, wqkv, wf, bf)


def _rope_block(x, cos, sin_signed):
    lane = _iota(x.shape, 1)
    first_half = (lane % HEAD_DIM) < (HEAD_DIM // 2)
    rot = jnp.where(first_half, pltpu.roll(x, LANES - HEAD_DIM // 2, 1), pltpu.roll(x, HEAD_DIM // 2, 1))
    return x * cos + rot * sin_signed


def _nsa_proj_kernel(x_ref, g_ref, wr_ref, wp_ref, wg_ref, cos_ref, sin_ref,
                     q_ref, kc_ref, ks_ref, kw_ref, vc_ref, vs_ref, vw_ref, gate_ref):
    h = _rms(x_ref[...], g_ref[...]).astype(BF16)
    cos = cos_ref[...]
    sin = sin_ref[...]
    r = _dot(h, wr_ref[...])
    nq = q_ref.shape[1]
    nkv = kc_ref.shape[1]
    roped = [_rope_block(r[:, b * LANES:(b + 1) * LANES], cos, sin) for b in range(r.shape[1] // LANES)]
    nqb, nkb = nq // LANES, nkv // LANES
    q_ref[...] = jnp.concatenate(roped[:nqb], axis=1)
    kc_ref[...] = jnp.concatenate(roped[nqb:nqb + nkb], axis=1)
    ks_ref[...] = jnp.concatenate(roped[nqb + nkb:nqb + 2 * nkb], axis=1)
    kw_ref[...] = jnp.concatenate(roped[nqb + 2 * nkb:], axis=1)
    p = _dot(h, wp_ref[...])
    vc_ref[...] = p[:, :nkv]
    vs_ref[...] = p[:, nkv:2 * nkv]
    vw_ref[...] = p[:, 2 * nkv:]
    gl = _dot(h, wg_ref[...])
    gate_ref[...] = jax.nn.sigmoid(gl[:, :gate_ref.shape[1]])


def _nsa_proj(x, g, wr, wp, wg, cos, sin, *, tm):
    rows, d = x.shape
    nq = NSA_HEADS * HEAD_DIM
    nkv = NSA_KV_HEADS * HEAD_DIM
    ngate = 3 * NSA_HEADS
    n_tab = cos.shape[0] // tm
    row = lambda c: pl.BlockSpec((tm, c), lambda i: (i, 0))
    full = lambda a: pl.BlockSpec(a.shape, lambda i: (0, 0))
    tab = pl.BlockSpec((tm, LANES), lambda i: (i % n_tab, 0))
    sds = lambda c: jax.ShapeDtypeStruct((rows, c), F32)
    return pl.pallas_call(
        _nsa_proj_kernel,
        grid=(rows // tm,),
        in_specs=[row(d), full(g), full(wr), full(wp), full(wg), tab, tab],
        out_specs=[row(nq)] + [row(nkv)] * 6 + [row(ngate)],
        out_shape=[sds(nq)] + [sds(nkv)] * 6 + [sds(ngate)],
        compiler_params=_params("parallel"),
        name="nsa_proj",
    )(x, g, wr, wp, wg, cos, sin)


def _cumsum_kernel(lf_ref, ccol_ref, crow_ref, *, blk):
    t = lf_ref.shape[1]
    lower = jnp.where(_iota((blk, blk), 1) <= _iota((blk, blk), 0), 1.0, 0.0).astype(BF16)
    carry = jnp.zeros((1, lf_ref.shape[2]), F32)
    for b in range(t // blk):
        cs = _dot3_r(lower, lf_ref[0, b * blk:(b + 1) * blk, :]) + carry
        ccol_ref[0, b * blk:(b + 1) * blk, :] = cs
        carry = cs[blk - 1:blk, :]
    nh = lf_ref.shape[2]
    eye = jnp.where(_iota((nh, nh), 0) == _iota((nh, nh), 1), 1.0, 0.0).astype(BF16)
    crow_ref[0] = _dot3_nt_r(eye, ccol_ref[0])


def _cumsum(lf):
    n, t, h = lf.shape
    return pl.pallas_call(
        functools.partial(_cumsum_kernel, blk=256),
        grid=(n,),
        in_specs=[pl.BlockSpec((1, t, h), lambda i: (i, 0, 0))],
        out_specs=[pl.BlockSpec((1, t, h), lambda i: (i, 0, 0)), pl.BlockSpec((1, h, t), lambda i: (i, 0, 0))],
        out_shape=[jax.ShapeDtypeStruct((n, t, h), F32), jax.ShapeDtypeStruct((n, h, t), F32)],
        compiler_params=_params("parallel"),
        name="fox_cumsum",
    )(lf)


def _fox_attn_kernel(q_ref, k_ref, v_ref, ccol_ref, crow_ref, o_ref, m_ref, l_ref, acc_ref, *, tq, tk):
    hp = pl.program_id(1)
    i = pl.program_id(2)
    q = q_ref[0] * SCALE
    lane = _iota((tq, LANES), 1)
    qs = jnp.concatenate([jnp.where(lane < HEAD_DIM, q, 0.0), jnp.where(lane >= HEAD_DIM, q, 0.0)],
                         axis=0).astype(BF16)
    cc = ccol_ref[0]
    hl = _iota(cc.shape, 1)
    ct = jnp.concatenate([jnp.sum(jnp.where(hl == 2 * hp + e, cc, 0.0), axis=1, keepdims=True) for e in range(2)],
                         axis=0)
    qpos = i * tq + _iota((2 * tq, 1), 0) % tq
    m_ref[...] = jnp.full(m_ref.shape, NEG_INF, F32)
    l_ref[...] = jnp.zeros(l_ref.shape, F32)
    acc_ref[...] = jnp.zeros(acc_ref.shape, F32)

    def body(j, carry):
        k0 = pl.multiple_of(j * tk, tk)
        kt = k_ref[0, pl.ds(k0, tk), :].astype(BF16)
        vt = v_ref[0, pl.ds(k0, tk), :].astype(BF16)
        cr = crow_ref[0, 0, :, pl.ds(k0, tk)]
        ck = jnp.concatenate([jnp.broadcast_to(cr[e:e + 1], (tq, tk)) for e in range(2)], axis=0)
        s = _dot_nt(qs, kt) + ct - ck
        kpos = k0 + _iota((1, tk), 1)
        _online_update(s, kpos <= qpos, vt, m_ref, l_ref, acc_ref)
        return carry

    lax.fori_loop(0, (i * tq + tq + tk - 1) // tk, body, 0)
    o = acc_ref[...] / jnp.maximum(l_ref[...], 1e-30)
    o_ref[0] = jnp.where(lane < HEAD_DIM, o[:tq], o[tq:])


def _fox_attn(q, k, v, ccol, crow, *, tq=256, tk=256):
    n, t, d = q.shape
    npair = d // LANES
    rows = 2 * tq
    return pl.pallas_call(
        functools.partial(_fox_attn_kernel, tq=tq, tk=tk),
        grid=(n, npair, t // tq),
        in_specs=[
            pl.BlockSpec((1, tq, LANES), lambda b, h, i: (b, i, h)),
            pl.BlockSpec((1, t, LANES), lambda b, h, i: (b, 0, h)),
            pl.BlockSpec((1, t, LANES), lambda b, h, i: (b, 0, h)),
            pl.BlockSpec((1, tq, ccol.shape[2]), lambda b, h, i: (b, i, 0)),
            pl.BlockSpec((1, 1, 2, t), lambda b, h, i: (b, h, 0, 0)),
        ],
        out_specs=pl.BlockSpec((1, tq, LANES), lambda b, h, i: (b, i, h)),
        out_shape=jax.ShapeDtypeStruct((n, t, d), F32),
        scratch_shapes=[pltpu.VMEM((rows, 1), F32), pltpu.VMEM((rows, 1), F32), pltpu.VMEM((rows, LANES), F32)],
        compiler_params=_params("parallel", "parallel", "arbitrary"),
        name="fox_attn",
    )(q, k, v, ccol, crow)


def _pad_rows(x, rows):
    rid = _iota((rows, x.shape[1]), 0)
    out = jnp.zeros((rows, x.shape[1]), x.dtype)
    for r in range(x.shape[0]):
        out = jnp.where(rid == r, jnp.broadcast_to(x[r:r + 1], out.shape), out)
    return out


def _fox_decode_kernel(pt_ref, q_ref, kn_ref, vn_ref, lfn_ref, *rest, nq, rows):
    npg = PAGES_PER_STEP
    k_refs, v_refs, lf_refs = rest[:npg], rest[npg:2 * npg], rest[2 * npg:3 * npg]
    o_ref = rest[3 * npg]
    qbd_ref, kcat_ref, vcat_ref, m_ref, l_ref, acc_ref, carry_ref, ncum_ref = rest[3 * npg + 1:]
    del pt_ref
    jj = pl.program_id(1)
    d = q_ref.shape[2]
    nh = FOX_HEADS
    pg = PAGE_SIZE
    r2 = _iota((rows, d), 0)
    block_diag = (r2 % nh == _iota((rows, d), 1) // HEAD_DIM) & (r2 < nq * nh)
    re = _iota((rows, nh), 0)
    head_of_row = jnp.where((re % nh == _iota((rows, nh), 1)) & (re < nq * nh), 1.0, 0.0).astype(BF16)
    t_r = _iota((rows, 1), 0) // nh

    @pl.when(jj == 0)
    def _():
        q = q_ref[0] * SCALE
        qb = jnp.concatenate([jnp.broadcast_to(q[t:t + 1], (nh, d)) for t in range(nq)]
                             + [jnp.zeros((rows - nq * nh, d), F32)], axis=0)
        qbd = jnp.where(block_diag, qb, 0.0).astype(BF16)
        qbd_ref[...] = qbd
        kn = _pad_rows(kn_ref[0], pg).astype(BF16)
        vn = _pad_rows(vn_ref[0], pg).astype(BF16)
        sn = _dot_nt(qbd, kn)
        xn = _dot3_nt_r(head_of_row, _pad_rows(lfn_ref[0], pg))
        cum = []
        c = jnp.zeros((rows, 1), F32)
        for s in range(nq):
            c = c + xn[:, s:s + 1]
            cum.append(c)
        lane = _iota((rows, pg), 1)
        ncum = cum[-1]
        cpre = jnp.broadcast_to(cum[-1], (rows, pg))
        for s in reversed(range(nq - 1)):
            ncum = jnp.where(t_r == s, cum[s], ncum)
            cpre = jnp.where(lane == s, cum[s], cpre)
        mask = (lane <= t_r) & (lane < nq)
        logit = jnp.where(mask, sn + ncum - cpre, NEG_INF)
        m = jnp.max(logit, axis=-1, keepdims=True)
        p = jnp.where(mask, jnp.exp(logit - m), 0.0)
        m_ref[...] = m
        l_ref[...] = jnp.sum(p, axis=-1, keepdims=True)
        acc_ref[...] = _dot(p.astype(BF16), vn)
        carry_ref[...] = jnp.zeros(carry_ref.shape, F32)
        ncum_ref[...] = ncum

    for p in range(npg):
        kcat_ref[p * pg:(p + 1) * pg, :] = k_refs[p][...].astype(BF16)
        vcat_ref[p * pg:(p + 1) * pg, :] = v_refs[p][...].astype(BF16)
    s = _dot_nt(qbd_ref[...], kcat_ref[...])
    lfcat = jnp.concatenate([lf_refs[p][...] for p in range(npg)], axis=0)
    x = _dot3_nt_r(head_of_row, lfcat)
    y = jnp.concatenate([x[:, p * pg:(p + 1) * pg] for p in range(npg)], axis=0)
    later = jnp.where(_iota((pg, pg), 0) >= _iota((pg, pg), 1), 1.0, 0.0).astype(BF16)
    z = _dot3_l(y, later)
    c = carry_ref[...]
    bias = [None] * npg
    for p in reversed(range(npg)):
        zp = z[p * rows:(p + 1) * rows]
        bias[p] = zp - y[p * rows:(p + 1) * rows] + c
        c = c + zp[:, 0:1]
    carry_ref[...] = c
    logits = s + jnp.concatenate(bias, axis=1) + ncum_ref[...]
    _online_update(logits, None, vcat_ref[...], m_ref, l_ref, acc_ref)

    @pl.when(jj == pl.num_programs(1) - 1)
    def _():
        o = jnp.where(block_diag, acc_ref[...] / jnp.maximum(l_ref[...], 1e-30), 0.0)
        for t in range(nq):
            o_ref[0, t:t + 1, :] = jnp.sum(o[t * nh:(t + 1) * nh], axis=0, keepdims=True)


def _fox_decode(page_table, q, k_new, v_new, lf_new, cache_k, cache_v, cache_lf, layer):
    b, nq, d = q.shape
    n_pages = page_table.shape[1]
    npg = PAGES_PER_STEP
    assert n_pages % npg == 0
    rows = LANES
    assert nq * FOX_HEADS <= rows

    def page_spec(width, p):
        return pl.BlockSpec((None, None, PAGE_SIZE, width),
                            lambda i, jj, pt: (layer, pt[i, n_pages - npg * (jj + 1) + p], 0, 0))

    tok = lambda w: pl.BlockSpec((1, nq, w), lambda i, jj, pt: (i, 0, 0))
    grid_spec = pltpu.PrefetchScalarGridSpec(
        num_scalar_prefetch=1,
        grid=(b, n_pages // npg),
        in_specs=[tok(d), tok(d), tok(d), tok(FOX_HEADS)]
        + [page_spec(d, p) for p in range(npg)] * 2 + [page_spec(FOX_HEADS, p) for p in range(npg)],
        out_specs=pl.BlockSpec((1, nq, d), lambda i, jj, pt: (i, 0, 0)),
        scratch_shapes=[
            pltpu.VMEM((rows, d), BF16),
            pltpu.VMEM((npg * PAGE_SIZE, d), BF16),
            pltpu.VMEM((npg * PAGE_SIZE, d), BF16),
            pltpu.VMEM((rows, 1), F32),
            pltpu.VMEM((rows, 1), F32),
            pltpu.VMEM((rows, d), F32),
            pltpu.VMEM((rows, 1), F32),
            pltpu.VMEM((rows, 1), F32),
        ],
    )
    return pl.pallas_call(
        functools.partial(_fox_decode_kernel, nq=nq, rows=rows),
        grid_spec=grid_spec,
        out_shape=jax.ShapeDtypeStruct((b, nq, d), F32),
        compiler_params=_params("parallel", "arbitrary"),
        name="fox_decode",
    )(page_table, q, k_new, v_new, lf_new, *([cache_k] * npg), *([cache_v] * npg), *([cache_lf] * npg))


def _compress_ab(x, pea, peb, w1a, w1b):
    return _dot((x + pea).astype(BF16), w1a), _dot((x + peb).astype(BF16), w1b)


def _compress_finish(a, b, w2):
    n = a.shape[0]
    hid = _silu(a + pltpu.roll(b, n - 1, 0))
    out = _dot(hid.astype(BF16), w2)
    return jnp.where(_iota(out.shape, 0) < n - 1, out, 0.0)


def _compress_kernel(x_ref, pea_ref, peb_ref, w1a_ref, w1b_ref, w2_ref, o_ref):
    a, b = _compress_ab(x_ref[0], pea_ref[...], peb_ref[...], w1a_ref[...], w1b_ref[...])
    o_ref[0] = _compress_finish(a, b, w2_ref[...])


def _compress(x, pea, peb, w1a, w1b, w2):
    n, c, w = x.shape
    full = lambda a: pl.BlockSpec(a.shape, lambda i: (0, 0))
    return pl.pallas_call(
        _compress_kernel,
        grid=(n,),
        in_specs=[pl.BlockSpec((1, c, w), lambda i: (i, 0, 0)), full(pea), full(peb), full(w1a), full(w1b), full(w2)],
        out_specs=pl.BlockSpec((1, c, w2.shape[1]), lambda i: (i, 0, 0)),
        out_shape=jax.ShapeDtypeStruct((n, c, w2.shape[1]), F32),
        compiler_params=_params("parallel"),
        name="nsa_compress",
    )(x, pea, peb, w1a, w1b, w2)


def _to_half(x, src, dst):
    lane_half = _iota(x.shape, 1) // HEAD_DIM
    if isinstance(src, int) and isinstance(dst, int):
        moved = x if src == dst else pltpu.roll(x, HEAD_DIM, 1)
    else:
        moved = jnp.where(src == dst, x, pltpu.roll(x, HEAD_DIM, 1))
    return jnp.where(lane_half == dst, moved, 0.0)


def _head_block(x, h):
    return x[:, (h // 2) * LANES:(h // 2 + 1) * LANES]


def _flash(qs, k_ref, v_ref, lo, hi, tk, mask_fn, m_ref, l_ref, acc_ref):
    m_ref[...] = jnp.full(m_ref.shape, NEG_INF, F32)
    l_ref[...] = jnp.zeros(l_ref.shape, F32)
    acc_ref[...] = jnp.zeros(acc_ref.shape, F32)

    def body(j, carry):
        k0 = pl.multiple_of(j * tk, tk)
        kt = k_ref[0, pl.ds(k0, tk), :].astype(BF16)
        vt = v_ref[0, pl.ds(k0, tk), :].astype(BF16)
        _online_update(_dot_nt(qs, kt), mask_fn(k0, tk), vt, m_ref, l_ref, acc_ref)
        return carry

    lax.fori_loop(lo, hi, body, 0)
    return acc_ref[...] / jnp.maximum(l_ref[...], 1e-30)


def _nsa_attn_kernel(q_ref, kc_ref, vc_ref, ks_ref, vs_ref, kw_ref, vw_ref, gate_ref, o_ref,
                     m_ref, l_ref, acc_ref, *, tq, tk, n_cmp, n_sel):
    g = pl.program_id(1)
    i = pl.program_id(2)
    par = g % 2
    hp = NSA_GROUP
    rows = hp * tq
    q = q_ref[0] * SCALE
    qs = jnp.concatenate([_to_half(_head_block(q, j), j % 2, par) for j in range(hp)], axis=0).astype(BF16)
    qpos = i * tq + _iota((rows, 1), 0) % tq
    tpos = i * tq + _iota((tq, 1), 0)

    ncp = kc_ref.shape[1]
    ci = _iota((1, ncp), 1)
    mask = (ci * CMP_STRIDE + CMP_LEN - 1 <= qpos) & (ci < n_cmp)
    p = _masked_softmax(_dot_nt(qs, kc_ref[0].astype(BF16)), mask)
    o_cmp = _dot(p.astype(BF16), vc_ref[0].astype(BF16))
    psum = p[0:tq]
    for j in range(1, hp):
        psum = psum + p[j * tq:(j + 1) * tq]
    imp = _dot3_l(psum, _cover_matrix(ncp, LANES, n_cmp, n_sel))
    sel = _top_mask(_selection_scores(imp, tpos, n_sel), n_sel, min(SLC_TOP, n_sel)).astype(BF16)

    def slc_mask(k0, width):
        block_of_key = (k0 + _iota((LANES, width), 1)) // SLC_BLOCK
        expand = jnp.where(block_of_key == _iota((LANES, width), 0), 1.0, 0.0).astype(BF16)
        selx = _dot(sel, expand)
        selx = jnp.concatenate([selx] * hp, axis=0)
        return (selx > 0.5) & (k0 + _iota((1, width), 1) <= qpos)

    o_slc = _flash(qs, ks_ref, vs_ref, 0, (i * tq + tq + tk - 1) // tk, tk, slc_mask, m_ref, l_ref, acc_ref)

    def win_mask(k0, width):
        dist = qpos - (k0 + _iota((1, width), 1))
        return (dist >= 0) & (dist < WINDOW)

    o_win = _flash(qs, kw_ref, vw_ref, jnp.maximum(i * tq - WINDOW + 1, 0) // tq, i + 1, tq, win_mask,
                   m_ref, l_ref, acc_ref)

    gates = gate_ref[0]
    gl = _iota(gates.shape, 1)
    outs = []
    for j in range(hp):
        head = g * hp + j
        gc = [jnp.sum(jnp.where(gl == c * NSA_HEADS + head, gates, 0.0), axis=1, keepdims=True) for c in range(3)]
        sl = slice(j * tq, (j + 1) * tq)
        outs.append(_to_half(gc[0] * o_cmp[sl] + gc[1] * o_slc[sl] + gc[2] * o_win[sl], par, j % 2))
    o_ref[0] = jnp.concatenate([outs[2 * e] + outs[2 * e + 1] for e in range(hp // 2)], axis=1)


def _nsa_attn(q, kcmp, vcmp, ks, vs, kw, vw, gates, *, tq=128, tk=256):
    n, t, _ = q.shape
    gw = NSA_GROUP * HEAD_DIM
    ncp = kcmp.shape[1]
    n_cmp = t // CMP_STRIDE - CMP_LEN // CMP_STRIDE + 1
    n_sel = -(-t // SLC_BLOCK)
    assert n_sel <= LANES and t % tq == 0 and t % tk == 0 and WINDOW % tq == 0
    rows = NSA_GROUP * tq
    pair = lambda r: pl.BlockSpec((1, r, LANES), lambda b, g, i: (b, 0, g // 2))
    return pl.pallas_call(
        functools.partial(_nsa_attn_kernel, tq=tq, tk=tk, n_cmp=n_cmp, n_sel=n_sel),
        grid=(n, NSA_KV_HEADS, t // tq),
        in_specs=[pl.BlockSpec((1, tq, gw), lambda b, g, i: (b, i, g)), pair(ncp), pair(ncp),
                  pair(t), pair(t), pair(t), pair(t),
                  pl.BlockSpec((1, tq, gates.shape[2]), lambda b, g, i: (b, i, 0))],
        out_specs=pl.BlockSpec((1, tq, gw), lambda b, g, i: (b, i, g)),
        out_shape=jax.ShapeDtypeStruct(q.shape, F32),
        scratch_shapes=[pltpu.VMEM((rows, 1), F32), pltpu.VMEM((rows, 1), F32), pltpu.VMEM((rows, LANES), F32)],
        compiler_params=_params("parallel", "parallel", "arbitrary"),
        name="nsa_attn",
    )(q, kcmp, vcmp, ks, vs, kw, vw, gates)


def _group_queries(q8, g, par):
    return jnp.concatenate([_to_half(_head_block(q8, h), h % 2, par)
                            for h in range(g * NSA_GROUP, (g + 1) * NSA_GROUP)], axis=0).astype(BF16)


def _assemble_heads(pieces):
    return jnp.concatenate([pieces[2 * e] + pieces[2 * e + 1] for e in range(len(pieces) // 2)], axis=1)


def _nsa_decode_cmp_kernel(pt_ref, q_ref, *rest, past_len, n_cmp, n_sel):
    npg = PAGES_PER_STEP
    kpages, vpages = rest[:npg], rest[npg:2 * npg]
    wk = rest[2 * npg:2 * npg + 5]
    wv = rest[2 * npg + 5:2 * npg + 10]
    o_ref, sel_ref = rest[2 * npg + 10:2 * npg + 12]
    ak_ref, bk_ref, av_ref, bv_ref = rest[2 * npg + 12:]
    del pt_ref
    jj = pl.program_id(1)
    step_chunks = npg * CHUNKS_PER_PAGE
    r0 = pl.multiple_of(jj * step_chunks, step_chunks)
    for pages, w, a_ref, b_ref in ((kpages, wk, ak_ref, bk_ref), (vpages, wv, av_ref, bv_ref)):
        x = jnp.concatenate([r[...] for r in pages], axis=0)
        a, b = _compress_ab(x, w[0][...], w[1][...], w[2][...], w[3][...])
        a_ref[pl.ds(r0, step_chunks), :] = a
        b_ref[pl.ds(r0, step_chunks), :] = b

    @pl.when(jj == pl.num_programs(1) - 1)
    def _():
        kcmp = _compress_finish(ak_ref[...], bk_ref[...], wk[4][...])
        vcmp = _compress_finish(av_ref[...], bv_ref[...], wv[4][...])
        ncp = kcmp.shape[0]
        q8 = _pad_rows(q_ref[0] * SCALE, SUBLANES)
        ci = _iota((1, ncp), 1)
        cover = _cover_matrix(ncp, sel_ref.shape[3], n_cmp, n_sel)
        pos8 = past_len + _iota((SUBLANES, 1), 0)
        pieces = [None] * NSA_HEADS
        for g in range(NSA_KV_HEADS):
            par = g % 2
            lanes = slice((g // 2) * LANES, (g // 2 + 1) * LANES)
            qs = _group_queries(q8, g, par)
            pos = past_len + _iota((qs.shape[0], 1), 0) % SUBLANES
            mask = (ci * CMP_STRIDE + CMP_LEN - 1 <= pos) & (ci < n_cmp)
            p = _masked_softmax(_dot_nt(qs, kcmp[:, lanes].astype(BF16)), mask)
            o = _dot(p.astype(BF16), vcmp[:, lanes].astype(BF16))
            psum = p[0:SUBLANES]
            for j in range(1, NSA_GROUP):
                psum = psum + p[j * SUBLANES:(j + 1) * SUBLANES]
            imp = _dot3_l(psum, cover)
            sel_ref[0, g] = _top_mask(_selection_scores(imp, pos8, n_sel), n_sel, min(SLC_TOP, n_sel))
            for j in range(NSA_GROUP):
                h = g * NSA_GROUP + j
                pieces[h] = _to_half(o[j * SUBLANES:(j + 1) * SUBLANES], par, h % 2)
        o_ref[0] = _assemble_heads(pieces)


def _nsa_decode_cmp(page_table, q, cache_k, cache_v, wk, wv, layer, *, nsp):
    b, nq, dq = q.shape
    n_pages = page_table.shape[1]
    npg = PAGES_PER_STEP
    past_len = n_pages * PAGE_SIZE
    assert n_pages % npg == 0 and nq <= SUBLANES and nq < CMP_STRIDE
    n_chunk = (past_len + nq) // CMP_STRIDE
    assert n_chunk == n_pages * CHUNKS_PER_PAGE
    n_cmp = n_chunk - CMP_LEN // CMP_STRIDE + 1
    n_sel = -(-(past_len + nq) // SLC_BLOCK)
    assert n_sel <= nsp
    gkv = NSA_KV_HEADS * HEAD_DIM

    def page_spec(p):
        return pl.BlockSpec((None, None, CHUNKS_PER_PAGE, CHUNK_LANES),
                            lambda i, jj, pt: (layer, pt[i, npg * jj + p], 0, 0))

    full = lambda a: pl.BlockSpec(a.shape, lambda i, jj, pt: (0, 0))
    grid_spec = pltpu.PrefetchScalarGridSpec(
        num_scalar_prefetch=1,
        grid=(b, n_pages // npg),
        in_specs=[pl.BlockSpec((1, nq, dq), lambda i, jj, pt: (i, 0, 0))]
        + [page_spec(p) for p in range(npg)] * 2 + [full(a) for a in wk] + [full(a) for a in wv],
        out_specs=[pl.BlockSpec((1, SUBLANES, dq), lambda i, jj, pt: (i, 0, 0)),
                   pl.BlockSpec((1, NSA_KV_HEADS, SUBLANES, nsp), lambda i, jj, pt: (i, 0, 0, 0))],
        scratch_shapes=[pltpu.VMEM((n_chunk, gkv), F32)] * 4,
    )
    return pl.pallas_call(
        functools.partial(_nsa_decode_cmp_kernel, past_len=past_len, n_cmp=n_cmp, n_sel=n_sel),
        grid_spec=grid_spec,
        out_shape=[jax.ShapeDtypeStruct((b, SUBLANES, dq), F32),
                   jax.ShapeDtypeStruct((b, NSA_KV_HEADS, SUBLANES, nsp), F32)],
        compiler_params=_params("parallel", "arbitrary"),
        name="nsa_decode_cmp",
    )(page_table, q, *([cache_k] * npg), *([cache_v] * npg), *wk, *wv)


def _nsa_decode_slc_kernel(pt_ref, q_ref, kn_ref, vn_ref, sel_ref, *rest, nq, past_len):
    npg = PAGES_PER_STEP
    kpages, vpages = rest[:npg], rest[npg:2 * npg]
    o_ref = rest[2 * npg]
    qs_ref, kcat_ref, vcat_ref, m_ref, l_ref, acc_ref = rest[2 * npg + 1:]
    del pt_ref
    jj = pl.program_id(1)
    pg = PAGE_SIZE
    npair = NSA_KV_HEADS // 2
    nsp = sel_ref.shape[3]
    selrows = [jnp.concatenate([sel_ref[0, 2 * gp + gi] for gi in range(2) for _ in range(NSA_GROUP)], axis=0)
               for gp in range(npair)]
    rows = selrows[0].shape[0]

    @pl.when(jj == 0)
    def _():
        q8 = _pad_rows(q_ref[0] * SCALE, SUBLANES)
        kn = _pad_rows(kn_ref[0], pg).astype(BF16)
        vn = _pad_rows(vn_ref[0], pg).astype(BF16)
        t_r = _iota((rows, 1), 0) % SUBLANES
        lane = _iota((rows, pg), 1)
        for gp in range(npair):
            lanes = slice(gp * LANES, (gp + 1) * LANES)
            qs = jnp.concatenate([_group_queries(q8, 2 * gp + gi, gi) for gi in range(2)], axis=0)
            qs_ref[gp] = qs
            m_ref[gp] = jnp.full((rows, 1), NEG_INF, F32)
            l_ref[gp] = jnp.zeros((rows, 1), F32)
            acc_ref[gp] = jnp.zeros((rows, LANES), F32)
            new_block = selrows[gp][:, past_len // SLC_BLOCK:past_len // SLC_BLOCK + 1]
            mask = (lane <= t_r) & (lane < nq) & (new_block > 0.5)
            _online_update(_dot_nt(qs, kn[:, lanes]), mask, vn[:, lanes], m_ref.at[gp], l_ref.at[gp], acc_ref.at[gp])

    for p in range(npg):
        kcat_ref[p * pg:(p + 1) * pg, :] = kpages[p][...].astype(BF16)
        vcat_ref[p * pg:(p + 1) * pg, :] = vpages[p][...].astype(BF16)
    width = npg * pg
    block_of_key = jj * (width // SLC_BLOCK) + _iota((nsp, width), 1) // SLC_BLOCK
    expand = jnp.where(block_of_key == _iota((nsp, width), 0), 1.0, 0.0).astype(BF16)
    for gp in range(npair):
        lanes = slice(gp * LANES, (gp + 1) * LANES)
        selx = _dot(selrows[gp].astype(BF16), expand)
        _online_update(_dot_nt(qs_ref[gp], kcat_ref[:, lanes]), selx > 0.5, vcat_ref[:, lanes],
                       m_ref.at[gp], l_ref.at[gp], acc_ref.at[gp])

    @pl.when(jj == pl.num_programs(1) - 1)
    def _():
        pieces = [None] * NSA_HEADS
        for gp in range(npair):
            o = acc_ref[gp] / jnp.maximum(l_ref[gp], 1e-30)
            for gi in range(2):
                for j in range(NSA_GROUP):
                    h = (2 * gp + gi) * NSA_GROUP + j
                    r0 = (gi * NSA_GROUP + j) * SUBLANES
                    pieces[h] = _to_half(o[r0:r0 + SUBLANES], gi, h % 2)
        o_ref[0] = _assemble_heads(pieces)


def _nsa_decode_slc(page_table, q, k_new, v_new, sel, cache_k, cache_v, layer):
    b, nq, dq = q.shape
    gkv = k_new.shape[2]
    n_pages = page_table.shape[1]
    npg = PAGES_PER_STEP
    past_len = n_pages * PAGE_SIZE
    assert n_pages % npg == 0 and past_len % SLC_BLOCK == 0 and nq <= SLC_BLOCK
    npair = NSA_KV_HEADS // 2
    rows = 2 * NSA_GROUP * SUBLANES

    def page_spec(p):
        return pl.BlockSpec((None, None, PAGE_SIZE, gkv), lambda i, jj, pt: (layer, pt[i, npg * jj + p], 0, 0))

    tok = lambda w: pl.BlockSpec((1, nq, w), lambda i, jj, pt: (i, 0, 0))
    grid_spec = pltpu.PrefetchScalarGridSpec(
        num_scalar_prefetch=1,
        grid=(b, n_pages // npg),
        in_specs=[tok(dq), tok(gkv), tok(gkv),
                  pl.BlockSpec((1,) + sel.shape[1:], lambda i, jj, pt: (i, 0, 0, 0))]
        + [page_spec(p) for p in range(npg)] * 2,
        out_specs=pl.BlockSpec((1, SUBLANES, dq), lambda i, jj, pt: (i, 0, 0)),
        scratch_shapes=[
            pltpu.VMEM((npair, rows, LANES), BF16),
            pltpu.VMEM((npg * PAGE_SIZE, gkv), BF16),
            pltpu.VMEM((npg * PAGE_SIZE, gkv), BF16),
            pltpu.VMEM((npair, rows, 1), F32),
            pltpu.VMEM((npair, rows, 1), F32),
            pltpu.VMEM((npair, rows, LANES), F32),
        ],
    )
    return pl.pallas_call(
        functools.partial(_nsa_decode_slc_kernel, nq=nq, past_len=past_len),
        grid_spec=grid_spec,
        out_shape=jax.ShapeDtypeStruct((b, SUBLANES, dq), F32),
        compiler_params=_params("parallel", "arbitrary"),
        name="nsa_decode_slc",
    )(page_table, q, k_new, v_new, sel, *([cache_k] * npg), *([cache_v] * npg))


def _nsa_decode_win_kernel(q_ref, kn_ref, vn_ref, sk_ref, sv_ref, ocmp_ref, oslc_ref, gate_ref,
                           o_ref, nk_ref, nv_ref, *, nq):
    nbuf = sk_ref.shape[1]
    pg = PAGE_SIZE
    q8 = _pad_rows(q_ref[0] * SCALE, SUBLANES)
    kn = _pad_rows(kn_ref[0], pg).astype(BF16)
    vn = _pad_rows(vn_ref[0], pg).astype(BF16)
    gates = _pad_rows(gate_ref[0], SUBLANES)
    gl = _iota(gates.shape, 1)
    rows = NSA_GROUP * SUBLANES
    t_r = _iota((rows, 1), 0) % SUBLANES
    dist_s = t_r + nbuf - _iota((1, nbuf), 1)
    mask_s = (dist_s >= 0) & (dist_s < WINDOW)
    lane_n = _iota((1, pg), 1)
    dist_n = t_r - lane_n
    mask_n = (dist_n >= 0) & (dist_n < WINDOW) & (lane_n < nq)
    pieces = [None] * NSA_HEADS
    for g in range(NSA_KV_HEADS):
        par = g % 2
        lanes = slice((g // 2) * LANES, (g // 2 + 1) * LANES)
        qs = _group_queries(q8, g, par)
        ss = jnp.where(mask_s, _dot_nt(qs, sk_ref[0, :, lanes].astype(BF16)), NEG_INF)
        sn = jnp.where(mask_n, _dot_nt(qs, kn[:, lanes]), NEG_INF)
        m = jnp.maximum(jnp.max(ss, axis=-1, keepdims=True), jnp.max(sn, axis=-1, keepdims=True))
        ps = jnp.where(mask_s, jnp.exp(ss - m), 0.0)
        pn = jnp.where(mask_n, jnp.exp(sn - m), 0.0)
        den = jnp.sum(ps, axis=-1, keepdims=True) + jnp.sum(pn, axis=-1, keepdims=True)
        ow = (_dot(ps.astype(BF16), sv_ref[0, :, lanes].astype(BF16)) + _dot(pn.astype(BF16), vn[:, lanes]))
        ow = ow / jnp.maximum(den, 1e-30)
        for j in range(NSA_GROUP):
            h = g * NSA_GROUP + j
            pieces[h] = _to_half(ow[j * SUBLANES:(j + 1) * SUBLANES], par, h % 2)
    o_win = _assemble_heads(pieces)
    head_of_lane = _iota((SUBLANES, o_win.shape[1]), 1) // HEAD_DIM
    branch_gate = []
    for c in range(3):
        gfull = jnp.zeros(o_win.shape, F32)
        for h in range(NSA_HEADS):
            col = jnp.sum(jnp.where(gl == c * NSA_HEADS + h, gates, 0.0), axis=1, keepdims=True)
            gfull = jnp.where(head_of_lane == h, col, gfull)
        branch_gate.append(gfull)
    o = branch_gate[0] * ocmp_ref[0] + branch_gate[1] * oslc_ref[0] + branch_gate[2] * o_win
    o_ref[0] = o[:nq]
    nk_ref[0, 0:nbuf - nq, :] = sk_ref[0, nq:nbuf, :]
    nk_ref[0, nbuf - nq:nbuf, :] = kn_ref[0]
    nv_ref[0, 0:nbuf - nq, :] = sv_ref[0, nq:nbuf, :]
    nv_ref[0, nbuf - nq:nbuf, :] = vn_ref[0]


def _nsa_decode_win(q, k_new, v_new, state_k, state_v, o_cmp, o_slc, gates):
    b, nq, dq = q.shape
    nbuf, gkv = state_k.shape[1:]
    assert nbuf == WINDOW and nq <= SUBLANES
    blk = lambda a: pl.BlockSpec((1,) + a.shape[1:], lambda i: (i, 0, 0))
    return pl.pallas_call(
        functools.partial(_nsa_decode_win_kernel, nq=nq),
        grid=(b,),
        in_specs=[blk(a) for a in (q, k_new, v_new, state_k, state_v, o_cmp, o_slc, gates)],
        out_specs=[blk(q), blk(state_k), blk(state_v)],
        out_shape=[jax.ShapeDtypeStruct(q.shape, F32), jax.ShapeDtypeStruct(state_k.shape, F32),
                   jax.ShapeDtypeStruct(state_v.shape, F32)],
        compiler_params=_params("parallel"),
        name="nsa_decode_win",
    )(q, k_new, v_new, state_k, state_v, o_cmp, o_slc, gates)


def _rope_tables(pos):
    half = HEAD_DIM // 2
    inv = ROPE_THETA ** (-jnp.arange(half, dtype=F32) / half)
    ang = pos.astype(F32)[:, None] * inv[None, :]
    cos, sin = jnp.cos(ang), jnp.sin(ang)
    reps = LANES // HEAD_DIM
    return (jnp.tile(jnp.concatenate([cos, cos], axis=1), (1, reps)),
            jnp.tile(jnp.concatenate([-sin, sin], axis=1), (1, reps)))


def _compress_weights(pe, w1, w2):
    half = CMP_STRIDE
    assert CMP_LEN == 2 * CMP_STRIDE
    ng = NSA_KV_HEADS
    eye = jnp.eye(ng, dtype=F32)
    big = lambda w: jnp.einsum('jde,gh->jgdhe', w, eye).reshape(half * ng * HEAD_DIM, ng * HEAD_DIM).astype(BF16)
    pe_row = lambda p: jnp.broadcast_to(p[:, None, :], (half, ng, HEAD_DIM)).reshape(1, half * ng * HEAD_DIM)
    return (pe_row(pe[:half]), pe_row(pe[half:]), big(w1[:half]), big(w1[half:]), jnp.kron(eye, w2).astype(BF16))


def _pad_cols(w, cols):
    return jnp.pad(w, ((0, 0), (0, cols - w.shape[1])))


TM_PROMPT_FFN = 1024
TM_PROMPT = 512


def kernel(x_prompt, x_sample, cache_fox_k, cache_fox_v, cache_fox_logf, cache_nsa_cmp_k, cache_nsa_cmp_v, cache_nsa_slc_k, cache_nsa_slc_v, state_nsa_win_k, state_nsa_win_v, page_table, norm_pre, norm_post, ffn1_w_in, ffn1_w_out, ffn2_w_in, ffn2_w_out, fox_w_in, fox_b_f, fox_w_out, nsa_w_in, nsa_cmp_pe_k, nsa_cmp_w1_k, nsa_cmp_w2_k, nsa_cmp_pe_v, nsa_cmp_w1_v, nsa_cmp_w2_v, nsa_w_out):
    n, t, d = x_prompt.shape
    b, nq, _ = x_sample.shape
    depth = norm_pre.shape[0]
    assert depth == 2 and fox_w_in.shape[0] == 1 and nsa_w_in.shape[0] == 1
    n_pages = page_table.shape[1]
    past_len = n_pages * PAGE_SIZE
    pool = cache_fox_k.shape[1]
    rs = b * nq
    row = lambda a: a.reshape(1, -1)
    xp = x_prompt.reshape(n * t, d)
    xs = x_sample.reshape(rs, d)

    def ffn_pair(xp, xs, layer, slot, w_in, w_out):
        wi, wo = w_in[layer].astype(BF16), w_out[layer].astype(BF16)
        gpre, gpost = row(norm_pre[layer, slot]), row(norm_post[layer, slot])
        return (_ffn(xp, gpre, gpost, wi, wo, tm=TM_PROMPT_FFN), _ffn(xs, gpre, gpost, wi, wo, tm=rs))

    xp, xs = ffn_pair(xp, xs, 0, 0, ffn1_w_in, ffn1_w_out)
    dh = FOX_HEADS * HEAD_DIM
    wqkv = fox_w_in[0][:, :3 * dh].astype(BF16)
    wf = _pad_cols(fox_w_in[0][:, 3 * dh:], LANES).astype(BF16)
    bf = row(fox_b_f[0])
    wout = fox_w_out[0].astype(BF16)
    gpre, gpost = row(norm_pre[0, 1]), row(norm_post[0, 1])

    qp, fox_k_p, fox_v_p, fox_lf_p = _fox_proj(xp, gpre, wqkv, wf, bf, tm=TM_PROMPT)
    ccol, crow = _cumsum(fox_lf_p.reshape(n, t, FOX_HEADS))
    op = _fox_attn(qp.reshape(n, t, dh), fox_k_p.reshape(n, t, dh), fox_v_p.reshape(n, t, dh),
                   ccol, crow.reshape(n, FOX_HEADS // 2, 2, t))
    xp = _oproj(op.reshape(n * t, dh), xp, wout, gpost, tm=TM_PROMPT)

    qs, fox_k_s, fox_v_s, fox_lf_s = _fox_proj(xs, gpre, wqkv, wf, bf, tm=rs)
    os_ = _fox_decode(page_table, qs.reshape(b, nq, dh), fox_k_s.reshape(b, nq, dh), fox_v_s.reshape(b, nq, dh),
                      fox_lf_s.reshape(b, nq, FOX_HEADS),
                      cache_fox_k.reshape(1, pool, PAGE_SIZE, dh), cache_fox_v.reshape(1, pool, PAGE_SIZE, dh),
                      cache_fox_logf, 0)
    xs = _oproj(os_.reshape(rs, dh), xs, wout, gpost, tm=rs)
    xp, xs = ffn_pair(xp, xs, 0, 2, ffn2_w_in, ffn2_w_out)

    xp, xs = ffn_pair(xp, xs, 1, 0, ffn1_w_in, ffn1_w_out)
    dq = NSA_HEADS * HEAD_DIM
    gkv = NSA_KV_HEADS * HEAD_DIM
    w = nsa_w_in[0]
    cut = lambda k: w[:, dq + k * gkv:dq + (k + 1) * gkv]
    wr = jnp.concatenate([w[:, :dq], cut(0), cut(2), cut(4)], axis=1).astype(BF16)
    wp = jnp.concatenate([cut(1), cut(3), cut(5)], axis=1).astype(BF16)
    wg = _pad_cols(w[:, dq + 6 * gkv:], LANES).astype(BF16)
    wout = nsa_w_out[0].astype(BF16)
    wk = _compress_weights(nsa_cmp_pe_k[0], nsa_cmp_w1_k[0], nsa_cmp_w2_k[0])
    wv = _compress_weights(nsa_cmp_pe_v[0], nsa_cmp_w1_v[0], nsa_cmp_w2_v[0])
    gpre, gpost = row(norm_pre[1, 1]), row(norm_post[1, 1])

    cos_p, sin_p = _rope_tables(jnp.arange(t, dtype=jnp.int32))
    q, kc, ks, kw, vc, vs, vw, gates = _nsa_proj(xp, gpre, wr, wp, wg, cos_p, sin_p, tm=TM_PROMPT)
    chunked = lambda a: a.reshape(n, t // CMP_STRIDE, CHUNK_LANES)
    kcmp = _compress(chunked(kc), *wk)
    vcmp = _compress(chunked(vc), *wv)
    seq = lambda a: a.reshape(n, t, a.shape[1])
    op = _nsa_attn(seq(q), kcmp, vcmp, seq(ks), seq(vs), seq(kw), seq(vw), seq(gates))
    xp = _oproj(op.reshape(n * t, dq), xp, wout, gpost, tm=TM_PROMPT)
    nsa_p = (kc, vc, ks, vs, kw, vw)

    cos_s, sin_s = _rope_tables(past_len + jnp.arange(nq, dtype=jnp.int32))
    cos_s, sin_s = jnp.tile(cos_s, (b, 1)), jnp.tile(sin_s, (b, 1))
    q, kc, ks, kw, vc, vs, vw, gates = _nsa_proj(xs, gpre, wr, wp, wg, cos_s, sin_s, tm=rs)
    tok = lambda a: a.reshape(b, nq, a.shape[1])
    n_sel = -(-(past_len + nq) // SLC_BLOCK)
    nsp = -(-n_sel // LANES) * LANES
    cpp = CHUNKS_PER_PAGE
    o_cmp, sel = _nsa_decode_cmp(page_table, tok(q), cache_nsa_cmp_k.reshape(1, pool, cpp, CHUNK_LANES),
                                 cache_nsa_cmp_v.reshape(1, pool, cpp, CHUNK_LANES), wk, wv, 0, nsp=nsp)
    o_slc = _nsa_decode_slc(page_table, tok(q), tok(ks), tok(vs), sel,
                            cache_nsa_slc_k.reshape(1, pool, PAGE_SIZE, gkv),
                            cache_nsa_slc_v.reshape(1, pool, PAGE_SIZE, gkv), 0)
    nbuf = state_nsa_win_k.shape[2]
    os_, win_k_s, win_v_s = _nsa_decode_win(tok(q), tok(kw), tok(vw), state_nsa_win_k.reshape(b, nbuf, gkv),
                                            state_nsa_win_v.reshape(b, nbuf, gkv), o_cmp, o_slc, tok(gates))
    xs = _oproj(os_.reshape(rs, dq), xs, wout, gpost, tm=rs)
    nsa_s = (kc, vc, ks, vs)
    xp, xs = ffn_pair(xp, xs, 1, 2, ffn2_w_in, ffn2_w_out)

    keep_p = min(WINDOW, t)
    fox_shape = lambda a, m, r: a.reshape(1, m, r, FOX_HEADS, HEAD_DIM)
    nsa_shape = lambda a, m, r: a.reshape(1, m, r, NSA_KV_HEADS, HEAD_DIM)
    win_p = lambda a: nsa_shape(a, n, t)[:, :, t - keep_p:]
    return (xp.reshape(n, t, d), xs.reshape(b, nq, d),
            fox_shape(fox_k_p, n, t), fox_shape(fox_v_p, n, t), fox_lf_p.reshape(1, n, t, FOX_HEADS),
            fox_shape(fox_k_s, b, nq), fox_shape(fox_v_s, b, nq), fox_lf_s.reshape(1, b, nq, FOX_HEADS),
            nsa_shape(nsa_p[0], n, t), nsa_shape(nsa_p[1], n, t), nsa_shape(nsa_p[2], n, t), nsa_shape(nsa_p[3], n, t),
            win_p(nsa_p[4]), win_p(nsa_p[5]),
            nsa_shape(nsa_s[0], b, nq), nsa_shape(nsa_s[1], b, nq), nsa_shape(nsa_s[2], b, nq), nsa_shape(nsa_s[3], b, nq),
            nsa_shape(win_k_s, b, nbuf), nsa_shape(win_v_s, b, nbuf))
```

```python
import functools

import jax
import jax.numpy as jnp
from jax import lax
from jax.experimental import pallas as pl
from jax.experimental.pallas import tpu as pltpu

F32 = jnp.float32
BF16 = jnp.bfloat16

HEAD_DIM = 64
FOX_HEADS = 16
NSA_HEADS = 16
NSA_KV_HEADS = 4
NSA_GROUP = NSA_HEADS // NSA_KV_HEADS
CMP_LEN = 32
CMP_STRIDE = 16
SLC_BLOCK = 64
SLC_TOP = 16
N_LOCAL_BLOCKS = 2
FORCE_SCORE = 1e4
WINDOW = 512
ROPE_THETA = 10000.0
NORM_EPS = 1e-6
NEG_INF = -1e30
MASKED = 2 * NEG_INF
SCALE = HEAD_DIM ** -0.5
PAGE_SIZE = 128

LANES = 128
SUBLANES = 8
VMEM_LIMIT = 56 * 1024 * 1024

PAGES_PER_STEP = 8
CHUNKS_PER_PAGE = PAGE_SIZE // CMP_STRIDE
ROW_BLOCK = 128


def _params(*sem):
    return pltpu.CompilerParams(dimension_semantics=sem, vmem_limit_bytes=VMEM_LIMIT)


def _dot(a, b):
    return jnp.dot(a, b, preferred_element_type=F32)


def _dot_nt(a, b):
    return lax.dot_general(a, b, (((1,), (1,)), ((), ())), preferred_element_type=F32)


def _split3(x):
    hi = x.astype(BF16)
    r1 = x - hi.astype(F32)
    mid = r1.astype(BF16)
    lo = (r1 - mid.astype(F32)).astype(BF16)
    return hi, mid, lo


def _dot3_l(a, b):
    hi, mid, lo = _split3(a)
    return _dot(hi, b) + _dot(mid, b) + _dot(lo, b)


def _dot3_r(a, b):
    hi, mid, lo = _split3(b)
    return _dot(a, hi) + _dot(a, mid) + _dot(a, lo)


def _dot3_nt_r(a, b):
    hi, mid, lo = _split3(b)
    return _dot_nt(a, hi) + _dot_nt(a, mid) + _dot_nt(a, lo)


def _rms(x, g):
    return x * lax.rsqrt(jnp.mean(x * x, axis=-1, keepdims=True) + NORM_EPS) * g


def _silu(x):
    return x * jax.nn.sigmoid(x)


def _iota(shape, dim):
    return lax.broadcasted_iota(jnp.int32, shape, dim)


def _masked_softmax(s, mask):
    s = jnp.where(mask, s, NEG_INF)
    m = jnp.max(s, axis=-1, keepdims=True)
    p = jnp.where(mask, jnp.exp(s - m), 0.0)
    return p / jnp.maximum(jnp.sum(p, axis=-1, keepdims=True), 1e-30)


def _init_stats(m_ref, l_ref, acc_ref):
    m_ref[...] = jnp.full(m_ref.shape, NEG_INF, F32)
    l_ref[...] = jnp.zeros(l_ref.shape, F32)
    acc_ref[...] = jnp.zeros(acc_ref.shape, F32)


def _softmax_step(s, m_ref, l_ref):
    r, k = s.shape
    ps, alphas = [], []
    for r0 in range(0, r, ROW_BLOCK):
        r1 = min(r0 + ROW_BLOCK, r)
        cols = [s[r0:r1, c:c + LANES] for c in range(0, k, LANES)]
        cmax = cols[0]
        for col in cols[1:]:
            cmax = jnp.maximum(cmax, col)
        m_old = m_ref[r0:r1, :]
        m_new = jnp.maximum(m_old, jnp.max(cmax, axis=1, keepdims=True))
        alpha = jnp.exp(m_old - m_new)
        pcs = [jnp.exp(col - m_new) for col in cols]
        lsum = pcs[0]
        for pc in pcs[1:]:
            lsum = lsum + pc
        l_ref[r0:r1, :] = alpha * l_ref[r0:r1, :] + lsum
        m_ref[r0:r1, :] = m_new
        ps.append(jnp.concatenate([pc.astype(BF16) for pc in pcs], axis=1))
        alphas.append(alpha)
    return jnp.concatenate(ps, axis=0), jnp.concatenate(alphas, axis=0)


def _normalize(acc, l_ref):
    return acc / jnp.maximum(jnp.sum(l_ref[...], axis=1, keepdims=True), 1e-30)


def _top_mask(score, n_real, n_top, axis):
    blk = _iota(score.shape, axis)
    rank = jnp.zeros(score.shape, F32)
    for i in range(n_real):
        one = score[:, i:i + 1] if axis == 1 else score[i:i + 1, :]
        tie = jnp.where(blk > i, 1.0, 0.0)
        rank = rank + jnp.where(one > score, 1.0, jnp.where(one == score, tie, 0.0))
    return jnp.where((rank < n_top) & (blk < n_real), 1.0, 0.0)


def _cover_matrix(shape, cmp_axis, n_cmp, n_sel):
    ci = _iota(shape, cmp_axis)
    sj = _iota(shape, 1 - cmp_axis)
    hit = ((ci * CMP_STRIDE < (sj + 1) * SLC_BLOCK) & (ci * CMP_STRIDE + CMP_LEN > sj * SLC_BLOCK)
           & (ci < n_cmp) & (sj < n_sel))
    return jnp.where(hit, 1.0, 0.0).astype(BF16)


def _selection_scores(imp, pos, n_sel, axis):
    sj = _iota(imp.shape, axis)
    blk_t = pos // SLC_BLOCK
    valid = (sj <= blk_t) & (sj < n_sel)
    forced = (sj == 0) | (blk_t - sj < N_LOCAL_BLOCKS)
    return jnp.where(valid, jnp.where(forced, FORCE_SCORE, imp), -jnp.inf)


def _pad_rows(x, rows):
    rid = _iota((rows, x.shape[1]), 0)
    out = jnp.zeros((rows, x.shape[1]), x.dtype)
    for r in range(x.shape[0]):
        out = jnp.where(rid == r, jnp.broadcast_to(x[r:r + 1], out.shape), out)
    return out


def _to_half(x, src, dst):
    lane_half = _iota(x.shape, 1) // HEAD_DIM
    if isinstance(src, int) and isinstance(dst, int):
        moved = x if src == dst else pltpu.roll(x, HEAD_DIM, 1)
    else:
        moved = jnp.where(src == dst, x, pltpu.roll(x, HEAD_DIM, 1))
    return jnp.where(lane_half == dst, moved, 0.0)


def _head_block(x, h):
    return x[:, (h // 2) * LANES:(h // 2 + 1) * LANES]


def _assemble_heads(pieces):
    return jnp.concatenate([pieces[2 * e] + pieces[2 * e + 1] for e in range(len(pieces) // 2)], axis=1)


def _ffn_kernel(x_ref, gpre_ref, gpost_ref, wg_ref, wu_ref, wo_ref, o_ref, h_ref, acc_ref):
    f = pl.program_id(1)

    @pl.when(f == 0)
    def _():
        h_ref[...] = _rms(x_ref[...], gpre_ref[...]).astype(BF16)
        acc_ref[...] = jnp.zeros_like(acc_ref)

    h = h_ref[...]
    gate = _dot(h, wg_ref[...])
    up = _dot(h, wu_ref[...])
    acc_ref[...] += _dot((_silu(gate) * up).astype(BF16), wo_ref[...])

    @pl.when(f == pl.num_programs(1) - 1)
    def _():
        o_ref[...] = x_ref[...] + 0.5 * _rms(acc_ref[...], gpost_ref[...])


def _ffn(x, g_pre, g_post, w_in, w_out, *, tm, tf=256):
    rows, d = x.shape
    d_ff = w_out.shape[0]
    nf = d_ff // tf
    assert rows % tm == 0 and d_ff % tf == 0
    return pl.pallas_call(
        _ffn_kernel,
        grid=(rows // tm, nf),
        in_specs=[
            pl.BlockSpec((tm, d), lambda i, f: (i, 0)),
            pl.BlockSpec((1, d), lambda i, f: (0, 0)),
            pl.BlockSpec((1, d), lambda i, f: (0, 0)),
            pl.BlockSpec((d, tf), lambda i, f: (0, f)),
            pl.BlockSpec((d, tf), lambda i, f: (0, nf + f)),
            pl.BlockSpec((tf, d), lambda i, f: (f, 0)),
        ],
        out_specs=pl.BlockSpec((tm, d), lambda i, f: (i, 0)),
        out_shape=jax.ShapeDtypeStruct((rows, d), F32),
        scratch_shapes=[pltpu.VMEM((tm, d), BF16), pltpu.VMEM((tm, d), F32)],
        compiler_params=_params("parallel", "arbitrary"),
        name="ffn",
    )(x, g_pre, g_post, w_in, w_in, w_out)


def _oproj_kernel(o_ref, x_ref, w_ref, g_ref, out_ref):
    y = _dot(o_ref[...].astype(BF16), w_ref[...])
    out_ref[...] = x_ref[...] + _rms(y, g_ref[...])


def _oproj(o, x, w, g, *, tm):
    rows, d = x.shape
    return pl.pallas_call(
        _oproj_kernel,
        grid=(rows // tm,),
        in_specs=[
            pl.BlockSpec((tm, o.shape[1]), lambda i: (i, 0)),
            pl.BlockSpec((tm, d), lambda i: (i, 0)),
            pl.BlockSpec(w.shape, lambda i: (0, 0)),
            pl.BlockSpec((1, d), lambda i: (0, 0)),
        ],
        out_specs=pl.BlockSpec((tm, d), lambda i: (i, 0)),
        out_shape=jax.ShapeDtypeStruct((rows, d), F32),
        compiler_params=_params("parallel"),
        name="oproj",
    )(o, x, w, g)


def _log_sigmoid(z):
    return jnp.minimum(z, 0.0) - jnp.log1p(jnp.exp(-jnp.abs(z)))


def _fox_proj_kernel(x_ref, g_ref, wqkv_ref, wf_ref, bf_ref, q_ref, k_ref, v_ref, lf_ref, kb_ref, vb_ref):
    h = _rms(x_ref[...], g_ref[...]).astype(BF16)
    d = q_ref.shape[1]
    qkv = _dot(h, wqkv_ref[...])
    q_ref[...] = qkv[:, :d]
    k_ref[...] = qkv[:, d:2 * d]
    v_ref[...] = qkv[:, 2 * d:]
    kb_ref[...] = qkv[:, d:2 * d].astype(BF16)
    vb_ref[...] = qkv[:, 2 * d:].astype(BF16)
    fl = _dot(h, wf_ref[...])
    lf_ref[...] = _log_sigmoid(fl[:, :FOX_HEADS] + bf_ref[...])


def _fox_proj(x, g, wqkv, wf, bf, *, tm):
    rows, d = x.shape
    row_spec = pl.BlockSpec((tm, d), lambda i: (i, 0))
    full = lambda a: pl.BlockSpec(a.shape, lambda i: (0, 0))
    return pl.pallas_call(
        _fox_proj_kernel,
        grid=(rows // tm,),
        in_specs=[row_spec, full(g), full(wqkv), full(wf), full(bf)],
        out_specs=[row_spec, row_spec, row_spec, pl.BlockSpec((tm, FOX_HEADS), lambda i: (i, 0)), row_spec, row_spec],
        out_shape=[jax.ShapeDtypeStruct((rows, d), F32)] * 3 + [jax.ShapeDtypeStruct((rows, FOX_HEADS), F32)]
        + [jax.ShapeDtypeStruct((rows, d), BF16)] * 2,
        compiler_params=_params("parallel"),
        name="fox_proj",
    )(x, g, wqkv, wf, bf)


def _rope_block(x, cos, sin_signed):
    lane = _iota(x.shape, 1)
    first_half = (lane % HEAD_DIM) < (HEAD_DIM // 2)
    rot = jnp.where(first_half, pltpu.roll(x, LANES - HEAD_DIM // 2, 1), pltpu.roll(x, HEAD_DIM // 2, 1))
    return x * cos + rot * sin_signed


def _nsa_proj_kernel(x_ref, g_ref, wr_ref, wp_ref, wg_ref, cos_ref, sin_ref,
                     q_ref, kc_ref, ks_ref, kw_ref, vc_ref, vs_ref, vw_ref, gate_ref,
                     ksb_ref, kwb_ref, vsb_ref, vwb_ref):
    h = _rms(x_ref[...], g_ref[...]).astype(BF16)
    cos = cos_ref[...]
    sin = sin_ref[...]
    r = _dot(h, wr_ref[...])
    nq = q_ref.shape[1]
    nkv = kc_ref.shape[1]
    roped = [_rope_block(r[:, b * LANES:(b + 1) * LANES], cos, sin) for b in range(r.shape[1] // LANES)]
    nqb, nkb = nq // LANES, nkv // LANES
    q_ref[...] = jnp.concatenate(roped[:nqb], axis=1)
    kc_ref[...] = jnp.concatenate(roped[nqb:nqb + nkb], axis=1)
    ks = jnp.concatenate(roped[nqb + nkb:nqb + 2 * nkb], axis=1)
    kw = jnp.concatenate(roped[nqb + 2 * nkb:], axis=1)
    ks_ref[...] = ks
    kw_ref[...] = kw
    ksb_ref[...] = ks.astype(BF16)
    kwb_ref[...] = kw.astype(BF16)
    p = _dot(h, wp_ref[...])
    vc_ref[...] = p[:, :nkv]
    vs_ref[...] = p[:, nkv:2 * nkv]
    vw_ref[...] = p[:, 2 * nkv:]
    vsb_ref[...] = p[:, nkv:2 * nkv].astype(BF16)
    vwb_ref[...] = p[:, 2 * nkv:].astype(BF16)
    gl = _dot(h, wg_ref[...])
    gate_ref[...] = jax.nn.sigmoid(gl[:, :gate_ref.shape[1]])


def _nsa_proj(x, g, wr, wp, wg, cos, sin, *, tm):
    rows, d = x.shape
    nq = NSA_HEADS * HEAD_DIM
    nkv = NSA_KV_HEADS * HEAD_DIM
    ngate = 3 * NSA_HEADS
    n_tab = cos.shape[0] // tm
    row = lambda c: pl.BlockSpec((tm, c), lambda i: (i, 0))
    full = lambda a: pl.BlockSpec(a.shape, lambda i: (0, 0))
    tab = pl.BlockSpec((tm, LANES), lambda i: (i % n_tab, 0))
    sds = lambda c, dt=F32: jax.ShapeDtypeStruct((rows, c), dt)
    return pl.pallas_call(
        _nsa_proj_kernel,
        grid=(rows // tm,),
        in_specs=[row(d), full(g), full(wr), full(wp), full(wg), tab, tab],
        out_specs=[row(nq)] + [row(nkv)] * 6 + [row(ngate)] + [row(nkv)] * 4,
        out_shape=[sds(nq)] + [sds(nkv)] * 6 + [sds(ngate)] + [sds(nkv, BF16)] * 4,
        compiler_params=_params("parallel"),
        name="nsa_proj",
    )(x, g, wr, wp, wg, cos, sin)


def _cumsum_kernel(lf_ref, ccol_ref, crow_ref, *, blk):
    t = lf_ref.shape[1]
    lower = jnp.where(_iota((blk, blk), 1) <= _iota((blk, blk), 0), 1.0, 0.0).astype(BF16)
    carry = jnp.zeros((1, lf_ref.shape[2]), F32)
    for b in range(t // blk):
        cs = _dot3_r(lower, lf_ref[0, b * blk:(b + 1) * blk, :]) + carry
        ccol_ref[0, b * blk:(b + 1) * blk, :] = cs
        carry = cs[blk - 1:blk, :]
    nh = lf_ref.shape[2]
    eye = jnp.where(_iota((nh, nh), 0) == _iota((nh, nh), 1), 1.0, 0.0).astype(BF16)
    crow_ref[0] = _dot3_nt_r(eye, ccol_ref[0])


def _cumsum(lf):
    n, t, h = lf.shape
    return pl.pallas_call(
        functools.partial(_cumsum_kernel, blk=256),
        grid=(n,),
        in_specs=[pl.BlockSpec((1, t, h), lambda i: (i, 0, 0))],
        out_specs=[pl.BlockSpec((1, t, h), lambda i: (i, 0, 0)), pl.BlockSpec((1, h, t), lambda i: (i, 0, 0))],
        out_shape=[jax.ShapeDtypeStruct((n, t, h), F32), jax.ShapeDtypeStruct((n, h, t), F32)],
        compiler_params=_params("parallel"),
        name="fox_cumsum",
    )(lf)


def _fox_attn_kernel(q_ref, k_ref, v_ref, ccol_ref, crow_ref, o_ref, m_ref, l_ref, acc_ref, *, tq, tk):
    hp = pl.program_id(1)
    i = pl.program_id(2)
    rows = 2 * tq
    q = q_ref[0] * SCALE
    lane = _iota((tq, LANES), 1)
    qs = jnp.concatenate([jnp.where(lane < HEAD_DIM, q, 0.0), jnp.where(lane >= HEAD_DIM, q, 0.0)],
                         axis=0).astype(BF16)
    cc = ccol_ref[0]
    hl = _iota(cc.shape, 1)
    ct = jnp.concatenate([jnp.sum(jnp.where(hl == 2 * hp + e, cc, 0.0), axis=1, keepdims=True) for e in range(2)],
                         axis=0)
    ct = jnp.broadcast_to(ct, (rows, LANES))
    lane_minus_row = _iota((rows, LANES), 1) - _iota((rows, LANES), 0) % tq
    _init_stats(m_ref, l_ref, acc_ref)

    def tile(j, causal):
        k0 = pl.multiple_of(j * tk, tk)
        kt = k_ref[0, pl.ds(k0, tk), :]
        vt = v_ref[0, pl.ds(k0, tk), :]
        s = _dot_nt(qs, kt)
        cr = crow_ref[0, 0, :, pl.ds(k0, tk)]
        cols = []
        for c in range(0, tk, LANES):
            ck = jnp.concatenate([jnp.broadcast_to(cr[e:e + 1, c:c + LANES], (tq, LANES)) for e in range(2)], axis=0)
            sc = s[:, c:c + LANES] + (ct - ck)
            if causal:
                sc = jnp.where(lane_minus_row <= i * tq - k0 - c, sc, MASKED)
            cols.append(sc)
        p, alpha = _softmax_step(jnp.concatenate(cols, axis=1), m_ref, l_ref)
        acc_ref[...] = alpha * acc_ref[...] + _dot(p, vt)

    def run(lo, hi, causal):
        def body(j, carry):
            tile(j, causal)
            return carry
        lax.fori_loop(lo, hi, body, 0)

    n_below = (i * tq) // tk
    run(0, n_below, False)
    run(n_below, (i * tq + tq + tk - 1) // tk, True)
    o = _normalize(acc_ref[...], l_ref)
    o_ref[0] = jnp.where(lane < HEAD_DIM, o[:tq], o[tq:])


def _fox_attn(q, k, v, ccol, crow, *, tq=512, tk=512):
    n, t, d = q.shape
    npair = d // LANES
    rows = 2 * tq
    return pl.pallas_call(
        functools.partial(_fox_attn_kernel, tq=tq, tk=tk),
        grid=(n, npair, t // tq),
        in_specs=[
            pl.BlockSpec((1, tq, LANES), lambda b, h, i: (b, i, h)),
            pl.BlockSpec((1, t, LANES), lambda b, h, i: (b, 0, h)),
            pl.BlockSpec((1, t, LANES), lambda b, h, i: (b, 0, h)),
            pl.BlockSpec((1, tq, ccol.shape[2]), lambda b, h, i: (b, i, 0)),
            pl.BlockSpec((1, 1, 2, t), lambda b, h, i: (b, h, 0, 0)),
        ],
        out_specs=pl.BlockSpec((1, tq, LANES), lambda b, h, i: (b, i, h)),
        out_shape=jax.ShapeDtypeStruct((n, t, d), F32),
        scratch_shapes=[pltpu.VMEM((rows, LANES), F32)] * 3,
        compiler_params=_params("parallel", "parallel", "arbitrary"),
        name="fox_attn",
    )(q, k, v, ccol, crow)


def _fox_decode_kernel(pt_ref, q_ref, kn_ref, vn_ref, lfn_ref, *rest, nq, rows):
    npg = PAGES_PER_STEP
    k_refs, v_refs, lf_refs = rest[:npg], rest[npg:2 * npg], rest[2 * npg:3 * npg]
    o_ref = rest[3 * npg]
    qbd_ref, kcat_ref, vcat_ref, m_ref, l_ref, acc_ref, carry_ref, ncum_ref = rest[3 * npg + 1:]
    del pt_ref
    jj = pl.program_id(1)
    d = q_ref.shape[2]
    nh = FOX_HEADS
    pg = PAGE_SIZE
    r2 = _iota((rows, d), 0)
    block_diag = (r2 % nh == _iota((rows, d), 1) // HEAD_DIM) & (r2 < nq * nh)
    re = _iota((rows, nh), 0)
    head_of_row = jnp.where((re % nh == _iota((rows, nh), 1)) & (re < nq * nh), 1.0, 0.0).astype(BF16)

    @pl.when(jj == 0)
    def _():
        q = q_ref[0] * SCALE
        qb = jnp.concatenate([jnp.broadcast_to(q[t:t + 1], (nh, d)) for t in range(nq)]
                             + [jnp.zeros((rows - nq * nh, d), F32)], axis=0)
        qbd = jnp.where(block_diag, qb, 0.0).astype(BF16)
        qbd_ref[...] = qbd
        kn = _pad_rows(kn_ref[0], pg).astype(BF16)
        vn = _pad_rows(vn_ref[0], pg).astype(BF16)
        sn = _dot_nt(qbd, kn)
        xn = _dot3_nt_r(head_of_row, _pad_rows(lfn_ref[0], pg))
        t_r = _iota((rows, 1), 0) // nh
        cum = []
        c = jnp.zeros((rows, 1), F32)
        for s in range(nq):
            c = c + xn[:, s:s + 1]
            cum.append(c)
        lane = _iota((rows, pg), 1)
        ncum = cum[-1]
        cpre = jnp.broadcast_to(cum[-1], (rows, pg))
        for s in reversed(range(nq - 1)):
            ncum = jnp.where(t_r == s, cum[s], ncum)
            cpre = jnp.where(lane == s, cum[s], cpre)
        mask = (lane <= t_r) & (lane < nq)
        logit = jnp.where(mask, sn + ncum - cpre, MASKED)
        m = jnp.maximum(jnp.max(logit, axis=-1, keepdims=True), NEG_INF)
        p = jnp.exp(logit - m)
        m_ref[...] = jnp.broadcast_to(m, m_ref.shape)
        l_ref[...] = p
        acc_ref[...] = _dot(p.astype(BF16), vn)
        carry_ref[...] = jnp.zeros(carry_ref.shape, F32)
        ncum_ref[...] = jnp.broadcast_to(ncum, ncum_ref.shape)

    for p in range(npg):
        kcat_ref[:, p * pg:(p + 1) * pg] = k_refs[p][...].astype(BF16)
        vcat_ref[:, p * pg:(p + 1) * pg] = v_refs[p][...].astype(BF16)
    s = _dot(qbd_ref[...], kcat_ref[...])
    lfcat = jnp.concatenate([lf_refs[p][...] for p in range(npg)], axis=1)
    x = _dot3_r(head_of_row, lfcat)
    y = jnp.concatenate([x[:, p * pg:(p + 1) * pg] for p in range(npg)], axis=0)
    ri, ci = _iota((pg, 2 * pg), 0), _iota((pg, 2 * pg), 1)
    later_and_total = jnp.where((ri >= ci) | (ci >= pg), 1.0, 0.0).astype(BF16)
    z = _dot3_l(y, later_and_total)
    c = carry_ref[...]
    ncum = ncum_ref[...]
    cols = [None] * npg
    for p in reversed(range(npg)):
        zp = z[p * rows:(p + 1) * rows]
        cols[p] = s[:, p * pg:(p + 1) * pg] + (zp[:, :pg] - y[p * rows:(p + 1) * rows] + c + ncum)
        c = c + zp[:, pg:]
    carry_ref[...] = c
    p, alpha = _softmax_step(jnp.concatenate(cols, axis=1), m_ref, l_ref)
    pv = _dot_nt(p, vcat_ref[...])
    for b in range(0, d, LANES):
        acc_ref[:, b:b + LANES] = alpha * acc_ref[:, b:b + LANES] + pv[:, b:b + LANES]

    @pl.when(jj == pl.num_programs(1) - 1)
    def _():
        o = jnp.where(block_diag, _normalize(acc_ref[...], l_ref), 0.0)
        for t in range(nq):
            o_ref[0, t:t + 1, :] = jnp.sum(o[t * nh:(t + 1) * nh], axis=0, keepdims=True)


def _fox_decode(page_table, q, k_new, v_new, lf_new, cache_kt, cache_vt, cache_lft, layer):
    b, nq, d = q.shape
    n_pages = page_table.shape[1]
    npg = PAGES_PER_STEP
    assert n_pages % npg == 0
    rows = LANES
    assert nq * FOX_HEADS <= rows

    def page_spec(height, p):
        return pl.BlockSpec((None, None, height, PAGE_SIZE),
                            lambda i, jj, pt: (layer, pt[i, n_pages - npg * (jj + 1) + p], 0, 0))

    tok = lambda w: pl.BlockSpec((1, nq, w), lambda i, jj, pt: (i, 0, 0))
    grid_spec = pltpu.PrefetchScalarGridSpec(
        num_scalar_prefetch=1,
        grid=(b, n_pages // npg),
        in_specs=[tok(d), tok(d), tok(d), tok(FOX_HEADS)]
        + [page_spec(d, p) for p in range(npg)] * 2 + [page_spec(FOX_HEADS, p) for p in range(npg)],
        out_specs=pl.BlockSpec((1, nq, d), lambda i, jj, pt: (i, 0, 0)),
        scratch_shapes=[
            pltpu.VMEM((rows, d), BF16),
            pltpu.VMEM((d, npg * PAGE_SIZE), BF16),
            pltpu.VMEM((d, npg * PAGE_SIZE), BF16),
            pltpu.VMEM((rows, LANES), F32),
            pltpu.VMEM((rows, LANES), F32),
            pltpu.VMEM((rows, d), F32),
            pltpu.VMEM((rows, LANES), F32),
            pltpu.VMEM((rows, LANES), F32),
        ],
    )
    return pl.pallas_call(
        functools.partial(_fox_decode_kernel, nq=nq, rows=rows),
        grid_spec=grid_spec,
        out_shape=jax.ShapeDtypeStruct((b, nq, d), F32),
        compiler_params=_params("parallel", "arbitrary"),
        name="fox_decode",
    )(page_table, q, k_new, v_new, lf_new, *([cache_kt] * npg), *([cache_vt] * npg), *([cache_lft] * npg))


def _compress_finish(a, b, w2):
    n = a.shape[0]
    hid = _silu(a + pltpu.roll(b, n - 1, 0))
    out = _dot(hid.astype(BF16), w2)
    return jnp.where(_iota(out.shape, 0) < n - 1, out, 0.0)


def _compress_rows(read_rows, n_chunk, pe_ref, w1_ref, w2_ref):
    half = CMP_STRIDE
    a = jnp.zeros((n_chunk, w2_ref.shape[1]), F32)
    b = jnp.zeros((n_chunk, w2_ref.shape[1]), F32)
    for j in range(half):
        xj = read_rows(j)
        a = a + _dot((xj + pe_ref[j:j + 1, :]).astype(BF16), w1_ref[j])
        b = b + _dot((xj + pe_ref[half + j:half + j + 1, :]).astype(BF16), w1_ref[half + j])
    return _compress_finish(a, b, w2_ref[...])


def _compress_kernel(*refs, nblk):
    x_refs = refs[:nblk]
    pe_ref, w1_ref, w2_ref, o_ref = refs[nblk:]
    n_chunk = o_ref.shape[1]
    read = lambda j: jnp.concatenate([r[0, pl.ds(j, n_chunk, stride=CMP_STRIDE), :] for r in x_refs], axis=1)
    o_ref[0] = _compress_rows(read, n_chunk, pe_ref, w1_ref, w2_ref)


def _compress(x, pe, w1, w2):
    n, t, gkv = x.shape
    assert CMP_LEN == 2 * CMP_STRIDE and t % CMP_STRIDE == 0
    c = t // CMP_STRIDE
    nblk = gkv // LANES
    full = lambda a: pl.BlockSpec(a.shape, lambda i: (0,) * a.ndim)
    lane_block = lambda e: pl.BlockSpec((1, t, LANES), lambda i: (i, 0, e))
    return pl.pallas_call(
        functools.partial(_compress_kernel, nblk=nblk),
        grid=(n,),
        in_specs=[lane_block(e) for e in range(nblk)] + [full(pe), full(w1), full(w2)],
        out_specs=pl.BlockSpec((1, c, gkv), lambda i: (i, 0, 0)),
        out_shape=jax.ShapeDtypeStruct((n, c, gkv), F32),
        compiler_params=_params("parallel"),
        name="nsa_compress",
    )(*([x] * nblk), pe, w1, w2)


def _nsa_attn_kernel(q_ref, kc_ref, vc_ref, ks_ref, vs_ref, kw_ref, vw_ref, gate_ref, expand_ref, o_ref,
                     m_ref, l_ref, acc_ref, bias_ref, *, tq, tk, n_cmp, n_sel):
    g = pl.program_id(1)
    i = pl.program_id(2)
    par = g % 2
    hp = NSA_GROUP
    rows = hp * tq
    q = q_ref[0] * SCALE
    qs = jnp.concatenate([_to_half(_head_block(q, j), j % 2, par) for j in range(hp)], axis=0).astype(BF16)
    qpos = i * tq + _iota((rows, 1), 0) % tq
    row_minus_lane = _iota((rows, LANES), 0) % tq - _iota((rows, LANES), 1)

    ncp = kc_ref.shape[1]
    ci = _iota((1, ncp), 1)
    mask = (ci * CMP_STRIDE + CMP_LEN - 1 <= qpos) & (ci < n_cmp)
    p = _masked_softmax(_dot_nt(qs, kc_ref[0].astype(BF16)), mask)
    o_cmp = _dot(p.astype(BF16), vc_ref[0].astype(BF16))
    psum = p[0:tq]
    for j in range(1, hp):
        psum = psum + p[j * tq:(j + 1) * tq]
    nsr = -(-n_sel // SUBLANES) * SUBLANES
    imp = _dot3_nt_r(_cover_matrix((nsr, ncp), 1, n_cmp, n_sel), psum)
    tpos = i * tq + _iota((1, tq), 1)
    sel_t = _top_mask(_selection_scores(imp, tpos, n_sel, 0), n_sel, min(SLC_TOP, n_sel), 0)
    sel = jnp.concatenate([sel_t, jnp.zeros((LANES - nsr, tq), F32)], axis=0).T.astype(BF16)
    bias_ref[...] = jnp.where(_dot(sel, expand_ref[...]) > 0.5, 0.0, MASKED)

    def attend(k_ref, v_ref, k0, width, transform):
        kt = k_ref[0, pl.ds(k0, width), :]
        vt = v_ref[0, pl.ds(k0, width), :]
        p, alpha = _softmax_step(transform(_dot_nt(qs, kt)), m_ref, l_ref)
        acc_ref[...] = alpha * acc_ref[...] + _dot(p, vt)

    def loop(lo, hi, fn):
        def body(j, carry):
            fn(j)
            return carry
        lax.fori_loop(lo, hi, body, 0)

    def slc_tile(j, causal):
        k0 = pl.multiple_of(j * tk, tk)

        def transform(s):
            b = bias_ref[:, pl.ds(k0, tk)]
            s = s + jnp.concatenate([b] * hp, axis=0)
            if not causal:
                return s
            cols = [jnp.where(row_minus_lane >= k0 + c - i * tq, s[:, c:c + LANES], MASKED)
                    for c in range(0, tk, LANES)]
            return jnp.concatenate(cols, axis=1)

        attend(ks_ref, vs_ref, k0, tk, transform)

    _init_stats(m_ref, l_ref, acc_ref)
    n_below = (i * tq) // tk
    loop(0, n_below, lambda j: slc_tile(j, False))
    loop(n_below, (i * tq + tq + tk - 1) // tk, lambda j: slc_tile(j, True))
    o_slc = _normalize(acc_ref[...], l_ref)

    win_w = WINDOW + tq
    k0w = pl.multiple_of(jnp.maximum(i * tq - WINDOW, 0), tq)

    def win_transform(s):
        base = i * tq - k0w
        cols = []
        for c in range(0, win_w, LANES):
            sc = jnp.where(row_minus_lane >= c - base, s[:, c:c + LANES], MASKED)
            cols.append(jnp.where(row_minus_lane < WINDOW + c - base, sc, MASKED))
        return jnp.concatenate(cols, axis=1)

    _init_stats(m_ref, l_ref, acc_ref)
    attend(kw_ref, vw_ref, k0w, win_w, win_transform)
    o_win = _normalize(acc_ref[...], l_ref)

    gates = gate_ref[0]
    gl = _iota(gates.shape, 1)
    outs = []
    for j in range(hp):
        head = g * hp + j
        gc = [jnp.sum(jnp.where(gl == c * NSA_HEADS + head, gates, 0.0), axis=1, keepdims=True) for c in range(3)]
        sl = slice(j * tq, (j + 1) * tq)
        outs.append(_to_half(gc[0] * o_cmp[sl] + gc[1] * o_slc[sl] + gc[2] * o_win[sl], par, j % 2))
    o_ref[0] = jnp.concatenate([outs[2 * e] + outs[2 * e + 1] for e in range(hp // 2)], axis=1)


def _nsa_attn(q, kcmp, vcmp, ks, vs, kw, vw, gates, *, tq=256, tk=512):
    n, t, _ = q.shape
    gw = NSA_GROUP * HEAD_DIM
    ncp = kcmp.shape[1]
    n_cmp = t // CMP_STRIDE - CMP_LEN // CMP_STRIDE + 1
    n_sel = -(-t // SLC_BLOCK)
    assert n_sel <= LANES and t % tq == 0 and t % tk == 0 and WINDOW % tq == 0 and t >= WINDOW + tq
    rows = NSA_GROUP * tq
    expand = (jnp.arange(t, dtype=jnp.int32)[None, :] // SLC_BLOCK
              == jnp.arange(LANES, dtype=jnp.int32)[:, None]).astype(BF16)
    pair = lambda r: pl.BlockSpec((1, r, LANES), lambda b, g, i: (b, 0, g // 2))
    return pl.pallas_call(
        functools.partial(_nsa_attn_kernel, tq=tq, tk=tk, n_cmp=n_cmp, n_sel=n_sel),
        grid=(n, NSA_KV_HEADS, t // tq),
        in_specs=[pl.BlockSpec((1, tq, gw), lambda b, g, i: (b, i, g)), pair(ncp), pair(ncp),
                  pair(t), pair(t), pair(t), pair(t),
                  pl.BlockSpec((1, tq, gates.shape[2]), lambda b, g, i: (b, i, 0)),
                  pl.BlockSpec(expand.shape, lambda b, g, i: (0, 0))],
        out_specs=pl.BlockSpec((1, tq, gw), lambda b, g, i: (b, i, g)),
        out_shape=jax.ShapeDtypeStruct(q.shape, F32),
        scratch_shapes=[pltpu.VMEM((rows, LANES), F32)] * 3 + [pltpu.VMEM((tq, t), F32)],
        compiler_params=_params("parallel", "parallel", "arbitrary"),
        name="nsa_attn",
    )(q, kcmp, vcmp, ks, vs, kw, vw, gates, expand)


def _group_queries(q8, g, par):
    return jnp.concatenate([_to_half(_head_block(q8, h), h % 2, par)
                            for h in range(g * NSA_GROUP, (g + 1) * NSA_GROUP)], axis=0).astype(BF16)


def _nsa_decode_cmp_kernel(pt_ref, q_ref, *rest, past_len, n_cmp, n_sel):
    npg = PAGES_PER_STEP
    kpages, vpages = rest[:npg], rest[npg:2 * npg]
    pek_ref, w1k_ref, w2k_ref, pev_ref, w1v_ref, w2v_ref = rest[2 * npg:2 * npg + 6]
    o_ref, sel_ref = rest[2 * npg + 6:2 * npg + 8]
    xk_ref, xv_ref = rest[2 * npg + 8:]
    del pt_ref
    jj = pl.program_id(1)
    pg = PAGE_SIZE
    r0 = jj * (npg * pg)
    nblk = xk_ref.shape[0]
    for p in range(npg):
        dst = pl.ds(pl.multiple_of(r0 + p * pg, pg), pg)
        xk = kpages[p][...].T
        xv = vpages[p][...].T
        for e in range(nblk):
            xk_ref[e, dst, :] = xk[:, e * LANES:(e + 1) * LANES]
            xv_ref[e, dst, :] = xv[:, e * LANES:(e + 1) * LANES]

    @pl.when(jj == pl.num_programs(1) - 1)
    def _():
        n_chunk = xk_ref.shape[1] // CMP_STRIDE

        def compress(x_ref, pe_ref, w1_ref, w2_ref):
            read = lambda j: jnp.concatenate(
                [x_ref[e, pl.ds(j, n_chunk, stride=CMP_STRIDE), :] for e in range(nblk)], axis=1)
            return _compress_rows(read, n_chunk, pe_ref, w1_ref, w2_ref)

        kcmp = compress(xk_ref, pek_ref, w1k_ref, w2k_ref)
        vcmp = compress(xv_ref, pev_ref, w1v_ref, w2v_ref)
        ncp = kcmp.shape[0]
        q8 = _pad_rows(q_ref[0] * SCALE, SUBLANES)
        ci = _iota((1, ncp), 1)
        cover = _cover_matrix((ncp, sel_ref.shape[3]), 0, n_cmp, n_sel)
        pos8 = past_len + _iota((SUBLANES, 1), 0)
        pieces = [None] * NSA_HEADS
        for g in range(NSA_KV_HEADS):
            par = g % 2
            lanes = slice((g // 2) * LANES, (g // 2 + 1) * LANES)
            qs = _group_queries(q8, g, par)
            pos = past_len + _iota((qs.shape[0], 1), 0) % SUBLANES
            mask = (ci * CMP_STRIDE + CMP_LEN - 1 <= pos) & (ci < n_cmp)
            p = _masked_softmax(_dot_nt(qs, kcmp[:, lanes].astype(BF16)), mask)
            o = _dot(p.astype(BF16), vcmp[:, lanes].astype(BF16))
            psum = p[0:SUBLANES]
            for j in range(1, NSA_GROUP):
                psum = psum + p[j * SUBLANES:(j + 1) * SUBLANES]
            imp = _dot3_l(psum, cover)
            sel_ref[0, g] = _top_mask(_selection_scores(imp, pos8, n_sel, 1), n_sel, min(SLC_TOP, n_sel), 1)
            for j in range(NSA_GROUP):
                h = g * NSA_GROUP + j
                pieces[h] = _to_half(o[j * SUBLANES:(j + 1) * SUBLANES], par, h % 2)
        o_ref[0] = _assemble_heads(pieces)


def _nsa_decode_cmp(page_table, q, cache_kt, cache_vt, wk, wv, layer, *, nsp):
    b, nq, dq = q.shape
    n_pages = page_table.shape[1]
    npg = PAGES_PER_STEP
    past_len = n_pages * PAGE_SIZE
    assert n_pages % npg == 0 and nq <= SUBLANES and nq < CMP_STRIDE and CMP_LEN == 2 * CMP_STRIDE
    n_chunk = (past_len + nq) // CMP_STRIDE
    assert n_chunk == n_pages * CHUNKS_PER_PAGE
    n_cmp = n_chunk - CMP_LEN // CMP_STRIDE + 1
    n_sel = -(-(past_len + nq) // SLC_BLOCK)
    assert n_sel <= nsp
    gkv = NSA_KV_HEADS * HEAD_DIM

    def page_spec(p):
        return pl.BlockSpec((None, None, gkv, PAGE_SIZE), lambda i, jj, pt: (layer, pt[i, npg * jj + p], 0, 0))

    full = lambda a: pl.BlockSpec(a.shape, lambda i, jj, pt: (0,) * a.ndim)
    grid_spec = pltpu.PrefetchScalarGridSpec(
        num_scalar_prefetch=1,
        grid=(b, n_pages // npg),
        in_specs=[pl.BlockSpec((1, nq, dq), lambda i, jj, pt: (i, 0, 0))]
        + [page_spec(p) for p in range(npg)] * 2 + [full(a) for a in wk] + [full(a) for a in wv],
        out_specs=[pl.BlockSpec((1, SUBLANES, dq), lambda i, jj, pt: (i, 0, 0)),
                   pl.BlockSpec((1, NSA_KV_HEADS, SUBLANES, nsp), lambda i, jj, pt: (i, 0, 0, 0))],
        scratch_shapes=[pltpu.VMEM((gkv // LANES, past_len, LANES), F32)] * 2,
    )
    return pl.pallas_call(
        functools.partial(_nsa_decode_cmp_kernel, past_len=past_len, n_cmp=n_cmp, n_sel=n_sel),
        grid_spec=grid_spec,
        out_shape=[jax.ShapeDtypeStruct((b, SUBLANES, dq), F32),
                   jax.ShapeDtypeStruct((b, NSA_KV_HEADS, SUBLANES, nsp), F32)],
        compiler_params=_params("parallel", "arbitrary"),
        name="nsa_decode_cmp",
    )(page_table, q, *([cache_kt] * npg), *([cache_vt] * npg), *wk, *wv)


def _nsa_decode_slc_kernel(pt_ref, q_ref, kn_ref, vn_ref, sel_ref, *rest, nq, past_len):
    npg = PAGES_PER_STEP
    kpages, vpages = rest[:npg], rest[npg:2 * npg]
    o_ref = rest[2 * npg]
    qs_ref, kcat_ref, vcat_ref, m_ref, l_ref, acc_ref = rest[2 * npg + 1:]
    del pt_ref
    jj = pl.program_id(1)
    pg = PAGE_SIZE
    npair = NSA_KV_HEADS // 2
    nsp = sel_ref.shape[3]
    selrows = [jnp.concatenate([sel_ref[0, 2 * gp + gi] for gi in range(2) for _ in range(NSA_GROUP)], axis=0)
               for gp in range(npair)]
    rows = selrows[0].shape[0]

    @pl.when(jj == 0)
    def _():
        q8 = _pad_rows(q_ref[0] * SCALE, SUBLANES)
        kn = _pad_rows(kn_ref[0], pg).astype(BF16)
        vn = _pad_rows(vn_ref[0], pg).astype(BF16)
        t_r = _iota((rows, 1), 0) % SUBLANES
        lane = _iota((rows, pg), 1)
        for gp in range(npair):
            lanes = slice(gp * LANES, (gp + 1) * LANES)
            qs = jnp.concatenate([_group_queries(q8, 2 * gp + gi, gi) for gi in range(2)], axis=0)
            qs_ref[gp] = qs
            _init_stats(m_ref.at[gp], l_ref.at[gp], acc_ref.at[gp])
            new_block = selrows[gp][:, past_len // SLC_BLOCK:past_len // SLC_BLOCK + 1]
            mask = (lane <= t_r) & (lane < nq) & (new_block > 0.5)
            s = jnp.where(mask, _dot_nt(qs, kn[:, lanes]), MASKED)
            p, alpha = _softmax_step(s, m_ref.at[gp], l_ref.at[gp])
            acc_ref[gp] = alpha * acc_ref[gp] + _dot(p, vn[:, lanes])

    for p in range(npg):
        kcat_ref[:, p * pg:(p + 1) * pg] = kpages[p][...].astype(BF16)
        vcat_ref[:, p * pg:(p + 1) * pg] = vpages[p][...].astype(BF16)
    width = npg * pg
    block_of_key = jj * (width // SLC_BLOCK) + _iota((nsp, width), 1) // SLC_BLOCK
    expand = jnp.where(block_of_key == _iota((nsp, width), 0), 1.0, 0.0).astype(BF16)
    for gp in range(npair):
        rws = slice(gp * LANES, (gp + 1) * LANES)
        selx = _dot(selrows[gp].astype(BF16), expand)
        s = jnp.where(selx > 0.5, _dot(qs_ref[gp], kcat_ref[rws, :]), MASKED)
        p, alpha = _softmax_step(s, m_ref.at[gp], l_ref.at[gp])
        acc_ref[gp] = alpha * acc_ref[gp] + _dot_nt(p, vcat_ref[rws, :])

    @pl.when(jj == pl.num_programs(1) - 1)
    def _():
        pieces = [None] * NSA_HEADS
        for gp in range(npair):
            o = _normalize(acc_ref[gp], l_ref.at[gp])
            for gi in range(2):
                for j in range(NSA_GROUP):
                    h = (2 * gp + gi) * NSA_GROUP + j
                    r0 = (gi * NSA_GROUP + j) * SUBLANES
                    pieces[h] = _to_half(o[r0:r0 + SUBLANES], gi, h % 2)
        o_ref[0] = _assemble_heads(pieces)


def _nsa_decode_slc(page_table, q, k_new, v_new, sel, cache_kt, cache_vt, layer):
    b, nq, dq = q.shape
    gkv = k_new.shape[2]
    n_pages = page_table.shape[1]
    npg = PAGES_PER_STEP
    past_len = n_pages * PAGE_SIZE
    assert n_pages % npg == 0 and past_len % SLC_BLOCK == 0 and nq <= SLC_BLOCK
    npair = NSA_KV_HEADS // 2
    rows = 2 * NSA_GROUP * SUBLANES

    def page_spec(p):
        return pl.BlockSpec((None, None, gkv, PAGE_SIZE), lambda i, jj, pt: (layer, pt[i, npg * jj + p], 0, 0))

    tok = lambda w: pl.BlockSpec((1, nq, w), lambda i, jj, pt: (i, 0, 0))
    grid_spec = pltpu.PrefetchScalarGridSpec(
        num_scalar_prefetch=1,
        grid=(b, n_pages // npg),
        in_specs=[tok(dq), tok(gkv), tok(gkv),
                  pl.BlockSpec((1,) + sel.shape[1:], lambda i, jj, pt: (i, 0, 0, 0))]
        + [page_spec(p) for p in range(npg)] * 2,
        out_specs=pl.BlockSpec((1, SUBLANES, dq), lambda i, jj, pt: (i, 0, 0)),
        scratch_shapes=[
            pltpu.VMEM((npair, rows, LANES), BF16),
            pltpu.VMEM((gkv, npg * PAGE_SIZE), BF16),
            pltpu.VMEM((gkv, npg * PAGE_SIZE), BF16),
            pltpu.VMEM((npair, rows, LANES), F32),
            pltpu.VMEM((npair, rows, LANES), F32),
            pltpu.VMEM((npair, rows, LANES), F32),
        ],
    )
    return pl.pallas_call(
        functools.partial(_nsa_decode_slc_kernel, nq=nq, past_len=past_len),
        grid_spec=grid_spec,
        out_shape=jax.ShapeDtypeStruct((b, SUBLANES, dq), F32),
        compiler_params=_params("parallel", "arbitrary"),
        name="nsa_decode_slc",
    )(page_table, q, k_new, v_new, sel, *([cache_kt] * npg), *([cache_vt] * npg))


def _nsa_decode_win_kernel(q_ref, kn_ref, vn_ref, sk_ref, sv_ref, ocmp_ref, oslc_ref, gate_ref,
                           o_ref, nk_ref, nv_ref, *, nq):
    nbuf = sk_ref.shape[1]
    pg = PAGE_SIZE
    q8 = _pad_rows(q_ref[0] * SCALE, SUBLANES)
    kn = _pad_rows(kn_ref[0], pg).astype(BF16)
    vn = _pad_rows(vn_ref[0], pg).astype(BF16)
    gates = _pad_rows(gate_ref[0], SUBLANES)
    gl = _iota(gates.shape, 1)
    rows = NSA_GROUP * SUBLANES
    t_r = _iota((rows, 1), 0) % SUBLANES
    dist_s = t_r + nbuf - _iota((1, nbuf), 1)
    mask_s = (dist_s >= 0) & (dist_s < WINDOW)
    lane_n = _iota((1, pg), 1)
    dist_n = t_r - lane_n
    mask_n = (dist_n >= 0) & (dist_n < WINDOW) & (lane_n < nq)
    pieces = [None] * NSA_HEADS
    for g in range(NSA_KV_HEADS):
        par = g % 2
        lanes = slice((g // 2) * LANES, (g // 2 + 1) * LANES)
        qs = _group_queries(q8, g, par)
        ss = jnp.where(mask_s, _dot_nt(qs, sk_ref[0, :, lanes].astype(BF16)), NEG_INF)
        sn = jnp.where(mask_n, _dot_nt(qs, kn[:, lanes]), NEG_INF)
        m = jnp.maximum(jnp.max(ss, axis=-1, keepdims=True), jnp.max(sn, axis=-1, keepdims=True))
        ps = jnp.where(mask_s, jnp.exp(ss - m), 0.0)
        pn = jnp.where(mask_n, jnp.exp(sn - m), 0.0)
        den = jnp.sum(ps, axis=-1, keepdims=True) + jnp.sum(pn, axis=-1, keepdims=True)
        ow = (_dot(ps.astype(BF16), sv_ref[0, :, lanes].astype(BF16)) + _dot(pn.astype(BF16), vn[:, lanes]))
        ow = ow / jnp.maximum(den, 1e-30)
        for j in range(NSA_GROUP):
            h = g * NSA_GROUP + j
            pieces[h] = _to_half(ow[j * SUBLANES:(j + 1) * SUBLANES], par, h % 2)
    o_win = _assemble_heads(pieces)
    head_of_lane = _iota((SUBLANES, o_win.shape[1]), 1) // HEAD_DIM
    branch_gate = []
    for c in range(3):
        gfull = jnp.zeros(o_win.shape, F32)
        for h in range(NSA_HEADS):
            col = jnp.sum(jnp.where(gl == c * NSA_HEADS + h, gates, 0.0), axis=1, keepdims=True)
            gfull = jnp.where(head_of_lane == h, col, gfull)
        branch_gate.append(gfull)
    o = branch_gate[0] * ocmp_ref[0] + branch_gate[1] * oslc_ref[0] + branch_gate[2] * o_win
    o_ref[0] = o[:nq]
    nk_ref[0, 0:nbuf - nq, :] = sk_ref[0, nq:nbuf, :]
    nk_ref[0, nbuf - nq:nbuf, :] = kn_ref[0]
    nv_ref[0, 0:nbuf - nq, :] = sv_ref[0, nq:nbuf, :]
    nv_ref[0, nbuf - nq:nbuf, :] = vn_ref[0]


def _nsa_decode_win(q, k_new, v_new, state_k, state_v, o_cmp, o_slc, gates):
    b, nq, dq = q.shape
    nbuf, gkv = state_k.shape[1:]
    assert nbuf == WINDOW and nq <= SUBLANES
    blk = lambda a: pl.BlockSpec((1,) + a.shape[1:], lambda i: (i, 0, 0))
    return pl.pallas_call(
        functools.partial(_nsa_decode_win_kernel, nq=nq),
        grid=(b,),
        in_specs=[blk(a) for a in (q, k_new, v_new, state_k, state_v, o_cmp, o_slc, gates)],
        out_specs=[blk(q), blk(state_k), blk(state_v)],
        out_shape=[jax.ShapeDtypeStruct(q.shape, F32), jax.ShapeDtypeStruct(state_k.shape, F32),
                   jax.ShapeDtypeStruct(state_v.shape, F32)],
        compiler_params=_params("parallel"),
        name="nsa_decode_win",
    )(q, k_new, v_new, state_k, state_v, o_cmp, o_slc, gates)


def _rope_tables(pos):
    half = HEAD_DIM // 2
    inv = ROPE_THETA ** (-jnp.arange(half, dtype=F32) / half)
    ang = pos.astype(F32)[:, None] * inv[None, :]
    cos, sin = jnp.cos(ang), jnp.sin(ang)
    reps = LANES // HEAD_DIM
    return (jnp.tile(jnp.concatenate([cos, cos], axis=1), (1, reps)),
            jnp.tile(jnp.concatenate([-sin, sin], axis=1), (1, reps)))


def _compress_weights(pe, w1, w2):
    ng = NSA_KV_HEADS
    gkv = ng * HEAD_DIM
    eye = jnp.eye(ng, dtype=F32)
    w1_bd = jnp.einsum('jde,gh->jgdhe', w1, eye).reshape(CMP_LEN, gkv, gkv).astype(BF16)
    return jnp.tile(pe, (1, ng)), w1_bd, jnp.kron(eye, w2).astype(BF16)


def _pad_cols(w, cols):
    return jnp.pad(w, ((0, 0), (0, cols - w.shape[1])))


def _transposed_pages(cache):
    l, pool, pg, nh, hd = cache.shape
    return jnp.transpose(cache, (0, 1, 3, 4, 2)).reshape(l, pool, nh * hd, pg)


TM_PROMPT_FFN = 1024
TM_PROMPT = 512


def kernel(x_prompt, x_sample, cache_fox_k, cache_fox_v, cache_fox_logf, cache_nsa_cmp_k, cache_nsa_cmp_v, cache_nsa_slc_k, cache_nsa_slc_v, state_nsa_win_k, state_nsa_win_v, page_table, norm_pre, norm_post, ffn1_w_in, ffn1_w_out, ffn2_w_in, ffn2_w_out, fox_w_in, fox_b_f, fox_w_out, nsa_w_in, nsa_cmp_pe_k, nsa_cmp_w1_k, nsa_cmp_w2_k, nsa_cmp_pe_v, nsa_cmp_w1_v, nsa_cmp_w2_v, nsa_w_out):
    n, t, d = x_prompt.shape
    b, nq, _ = x_sample.shape
    depth = norm_pre.shape[0]
    assert depth == 2 and fox_w_in.shape[0] == 1 and nsa_w_in.shape[0] == 1
    n_pages = page_table.shape[1]
    past_len = n_pages * PAGE_SIZE
    rs = b * nq
    row = lambda a: a.reshape(1, -1)
    xp = x_prompt.reshape(n * t, d)
    xs = x_sample.reshape(rs, d)

    def ffn_pair(xp, xs, layer, slot, w_in, w_out):
        wi, wo = w_in[layer].astype(BF16), w_out[layer].astype(BF16)
        gpre, gpost = row(norm_pre[layer, slot]), row(norm_post[layer, slot])
        return (_ffn(xp, gpre, gpost, wi, wo, tm=TM_PROMPT_FFN), _ffn(xs, gpre, gpost, wi, wo, tm=rs))

    xp, xs = ffn_pair(xp, xs, 0, 0, ffn1_w_in, ffn1_w_out)
    dh = FOX_HEADS * HEAD_DIM
    wqkv = fox_w_in[0][:, :3 * dh].astype(BF16)
    wf = _pad_cols(fox_w_in[0][:, 3 * dh:], LANES).astype(BF16)
    bf = row(fox_b_f[0])
    wout = fox_w_out[0].astype(BF16)
    gpre, gpost = row(norm_pre[0, 1]), row(norm_post[0, 1])

    qp, fox_k_p, fox_v_p, fox_lf_p, kb, vb = _fox_proj(xp, gpre, wqkv, wf, bf, tm=TM_PROMPT)
    ccol, crow = _cumsum(fox_lf_p.reshape(n, t, FOX_HEADS))
    op = _fox_attn(qp.reshape(n, t, dh), kb.reshape(n, t, dh), vb.reshape(n, t, dh),
                   ccol, crow.reshape(n, FOX_HEADS // 2, 2, t))
    xp = _oproj(op.reshape(n * t, dh), xp, wout, gpost, tm=TM_PROMPT)

    qs, fox_k_s, fox_v_s, fox_lf_s, _, _ = _fox_proj(xs, gpre, wqkv, wf, bf, tm=rs)
    os_ = _fox_decode(page_table, qs.reshape(b, nq, dh), fox_k_s.reshape(b, nq, dh), fox_v_s.reshape(b, nq, dh),
                      fox_lf_s.reshape(b, nq, FOX_HEADS),
                      _transposed_pages(cache_fox_k), _transposed_pages(cache_fox_v),
                      jnp.transpose(cache_fox_logf, (0, 1, 3, 2)), 0)
    xs = _oproj(os_.reshape(rs, dh), xs, wout, gpost, tm=rs)
    xp, xs = ffn_pair(xp, xs, 0, 2, ffn2_w_in, ffn2_w_out)

    xp, xs = ffn_pair(xp, xs, 1, 0, ffn1_w_in, ffn1_w_out)
    dq = NSA_HEADS * HEAD_DIM
    gkv = NSA_KV_HEADS * HEAD_DIM
    w = nsa_w_in[0]
    cut = lambda k: w[:, dq + k * gkv:dq + (k + 1) * gkv]
    wr = jnp.concatenate([w[:, :dq], cut(0), cut(2), cut(4)], axis=1).astype(BF16)
    wp = jnp.concatenate([cut(1), cut(3), cut(5)], axis=1).astype(BF16)
    wg = _pad_cols(w[:, dq + 6 * gkv:], LANES).astype(BF16)
    wout = nsa_w_out[0].astype(BF16)
    wk = _compress_weights(nsa_cmp_pe_k[0], nsa_cmp_w1_k[0], nsa_cmp_w2_k[0])
    wv = _compress_weights(nsa_cmp_pe_v[0], nsa_cmp_w1_v[0], nsa_cmp_w2_v[0])
    gpre, gpost = row(norm_pre[1, 1]), row(norm_post[1, 1])

    cos_p, sin_p = _rope_tables(jnp.arange(t, dtype=jnp.int32))
    q, kc, ks, kw, vc, vs, vw, gates, ksb, kwb, vsb, vwb = _nsa_proj(xp, gpre, wr, wp, wg, cos_p, sin_p, tm=TM_PROMPT)
    seq = lambda a: a.reshape(n, t, a.shape[1])
    kcmp = _compress(seq(kc), *wk)
    vcmp = _compress(seq(vc), *wv)
    op = _nsa_attn(seq(q), kcmp, vcmp, seq(ksb), seq(vsb), seq(kwb), seq(vwb), seq(gates))
    xp = _oproj(op.reshape(n * t, dq), xp, wout, gpost, tm=TM_PROMPT)
    nsa_p = (kc, vc, ks, vs, kw, vw)

    cos_s, sin_s = _rope_tables(past_len + jnp.arange(nq, dtype=jnp.int32))
    cos_s, sin_s = jnp.tile(cos_s, (b, 1)), jnp.tile(sin_s, (b, 1))
    q, kc, ks, kw, vc, vs, vw, gates, _, _, _, _ = _nsa_proj(xs, gpre, wr, wp, wg, cos_s, sin_s, tm=rs)
    tok = lambda a: a.reshape(b, nq, a.shape[1])
    n_sel = -(-(past_len + nq) // SLC_BLOCK)
    nsp = -(-n_sel // LANES) * LANES
    o_cmp, sel = _nsa_decode_cmp(page_table, tok(q), _transposed_pages(cache_nsa_cmp_k),
                                 _transposed_pages(cache_nsa_cmp_v), wk, wv, 0, nsp=nsp)
    o_slc = _nsa_decode_slc(page_table, tok(q), tok(ks), tok(vs), sel,
                            _transposed_pages(cache_nsa_slc_k), _transposed_pages(cache_nsa_slc_v), 0)
    nbuf = state_nsa_win_k.shape[2]
    os_, win_k_s, win_v_s = _nsa_decode_win(tok(q), tok(kw), tok(vw), state_nsa_win_k.reshape(b, nbuf, gkv),
                                            state_nsa_win_v.reshape(b, nbuf, gkv), o_cmp, o_slc, tok(gates))
    xs = _oproj(os_.reshape(rs, dq), xs, wout, gpost, tm=rs)
    nsa_s = (kc, vc, ks, vs)
    xp, xs = ffn_pair(xp, xs, 1, 2, ffn2_w_in, ffn2_w_out)

    keep_p = min(WINDOW, t)
    fox_shape = lambda a, m, r: a.reshape(1, m, r, FOX_HEADS, HEAD_DIM)
    nsa_shape = lambda a, m, r: a.reshape(1, m, r, NSA_KV_HEADS, HEAD_DIM)
    win_p = lambda a: nsa_shape(a, n, t)[:, :, t - keep_p:]
    return (xp.reshape(n, t, d), xs.reshape(b, nq, d),
            fox_shape(fox_k_p, n, t), fox_shape(fox_v_p, n, t), fox_lf_p.reshape(1, n, t, FOX_HEADS),
            fox_shape(fox_k_s, b, nq), fox_shape(fox_v_s, b, nq), fox_lf_s.reshape(1, b, nq, FOX_HEADS),
            nsa_shape(nsa_p[0], n, t), nsa_shape(nsa_p[1], n, t), nsa_shape(nsa_p[2], n, t), nsa_shape(nsa_p[3], n, t),
            win_p(nsa_p[4]), win_p(nsa_p[5]),
            nsa_shape(nsa_s[0], b, nq), nsa_shape(nsa_s[1], b, nq), nsa_shape(nsa_s[2], b, nq), nsa_shape(nsa_s[3], b, nq),
            nsa_shape(win_k_s, b, nbuf), nsa_shape(win_v_s, b, nbuf))
```

```python
import functools

import jax
import jax.numpy as jnp
from jax import lax
from jax.experimental import pallas as pl
from jax.experimental.pallas import tpu as pltpu

F32 = jnp.float32
BF16 = jnp.bfloat16

HEAD_DIM = 64
FOX_HEADS = 16
NSA_HEADS = 16
NSA_KV_HEADS = 4
NSA_GROUP = NSA_HEADS // NSA_KV_HEADS
CMP_LEN = 32
CMP_STRIDE = 16
SLC_BLOCK = 64
SLC_TOP = 16
N_LOCAL_BLOCKS = 2
FORCE_SCORE = 1e4
WINDOW = 512
ROPE_THETA = 10000.0
NORM_EPS = 1e-6
NEG_INF = -1e30
MASKED = 2 * NEG_INF
SCALE = HEAD_DIM ** -0.5
LOG2E = 1.4426950408889634
SCALE2 = SCALE * LOG2E
PAGE_SIZE = 128

LANES = 128
SUBLANES = 8
VMEM_LIMIT = 56 * 1024 * 1024

PAGES_PER_STEP = 8
NSA_PAGES_PER_STEP = 16
CHUNKS_PER_PAGE = PAGE_SIZE // CMP_STRIDE
ROW_BLOCK = 128


def _params(*sem):
    return pltpu.CompilerParams(dimension_semantics=sem, vmem_limit_bytes=VMEM_LIMIT)


def _dot(a, b):
    return jnp.dot(a, b, preferred_element_type=F32)


def _dot_nt(a, b):
    return lax.dot_general(a, b, (((1,), (1,)), ((), ())), preferred_element_type=F32)


def _split3(x):
    hi = x.astype(BF16)
    r1 = x - hi.astype(F32)
    mid = r1.astype(BF16)
    lo = (r1 - mid.astype(F32)).astype(BF16)
    return hi, mid, lo


def _dot3_l(a, b):
    hi, mid, lo = _split3(a)
    return _dot(hi, b) + _dot(mid, b) + _dot(lo, b)


def _dot3_r(a, b):
    hi, mid, lo = _split3(b)
    return _dot(a, hi) + _dot(a, mid) + _dot(a, lo)


def _dot3_nt_r(a, b):
    hi, mid, lo = _split3(b)
    return _dot_nt(a, hi) + _dot_nt(a, mid) + _dot_nt(a, lo)


def _rms(x, g):
    return x * lax.rsqrt(jnp.mean(x * x, axis=-1, keepdims=True) + NORM_EPS) * g


def _silu(x):
    return x * jax.nn.sigmoid(x)


def _iota(shape, dim):
    return lax.broadcasted_iota(jnp.int32, shape, dim)


def _masked_softmax2(s, mask):
    s = jnp.where(mask, s, NEG_INF)
    m = jnp.max(s, axis=-1, keepdims=True)
    p = jnp.where(mask, jnp.exp2(s - m), 0.0)
    return p / jnp.maximum(jnp.sum(p, axis=-1, keepdims=True), 1e-30)


def _init_stats(m_ref, l_ref, acc_ref):
    m_ref[...] = jnp.full(m_ref.shape, NEG_INF, F32)
    l_ref[...] = jnp.zeros(l_ref.shape, F32)
    acc_ref[...] = jnp.zeros(acc_ref.shape, F32)


def _softmax_step(s, m_ref, l_ref, p_dtype=BF16):
    r, k = s.shape
    ps, alphas = [], []
    for r0 in range(0, r, ROW_BLOCK):
        r1 = min(r0 + ROW_BLOCK, r)
        cols = [s[r0:r1, c:c + LANES] for c in range(0, k, LANES)]
        cmax = cols[0]
        for col in cols[1:]:
            cmax = jnp.maximum(cmax, col)
        m_old = m_ref[r0:r1, :]
        m_new = jnp.maximum(m_old, jnp.max(cmax, axis=1, keepdims=True))
        alpha = jnp.exp2(m_old - m_new)
        pcs = [jnp.exp2(col - m_new) for col in cols]
        lsum = pcs[0]
        for pc in pcs[1:]:
            lsum = lsum + pc
        l_ref[r0:r1, :] = alpha * l_ref[r0:r1, :] + lsum
        m_ref[r0:r1, :] = m_new
        ps.append(jnp.concatenate([pc.astype(p_dtype) for pc in pcs], axis=1))
        alphas.append(alpha)
    return jnp.concatenate(ps, axis=0), jnp.concatenate(alphas, axis=0)


def _normalize(acc, l_ref):
    return acc / jnp.maximum(jnp.sum(l_ref[...], axis=1, keepdims=True), 1e-30)


def _top_mask(score, n_real, n_top, axis):
    blk = _iota(score.shape, axis)
    rank = jnp.zeros(score.shape, F32)
    for i in range(n_real):
        one = score[:, i:i + 1] if axis == 1 else score[i:i + 1, :]
        tie = jnp.where(blk > i, 1.0, 0.0)
        rank = rank + jnp.where(one > score, 1.0, jnp.where(one == score, tie, 0.0))
    return jnp.where((rank < n_top) & (blk < n_real), 1.0, 0.0)


def _cover_matrix(shape, cmp_axis, n_cmp, n_sel):
    ci = _iota(shape, cmp_axis)
    sj = _iota(shape, 1 - cmp_axis)
    hit = ((ci * CMP_STRIDE < (sj + 1) * SLC_BLOCK) & (ci * CMP_STRIDE + CMP_LEN > sj * SLC_BLOCK)
           & (ci < n_cmp) & (sj < n_sel))
    return jnp.where(hit, 1.0, 0.0).astype(BF16)


def _selection_scores(imp, pos, n_sel, axis):
    sj = _iota(imp.shape, axis)
    blk_t = pos // SLC_BLOCK
    valid = (sj <= blk_t) & (sj < n_sel)
    forced = (sj == 0) | (blk_t - sj < N_LOCAL_BLOCKS)
    return jnp.where(valid, jnp.where(forced, FORCE_SCORE, imp), -jnp.inf)


def _pad_rows(x, rows):
    rid = _iota((rows, x.shape[1]), 0)
    out = jnp.zeros((rows, x.shape[1]), x.dtype)
    for r in range(x.shape[0]):
        out = jnp.where(rid == r, jnp.broadcast_to(x[r:r + 1], out.shape), out)
    return out


def _to_half(x, src, dst):
    lane_half = _iota(x.shape, 1) // HEAD_DIM
    if isinstance(src, int) and isinstance(dst, int):
        moved = x if src == dst else pltpu.roll(x, HEAD_DIM, 1)
    else:
        moved = jnp.where(src == dst, x, pltpu.roll(x, HEAD_DIM, 1))
    return jnp.where(lane_half == dst, moved, 0.0)


def _head_block(x, h):
    return x[:, (h // 2) * LANES:(h // 2 + 1) * LANES]


def _assemble_heads(pieces):
    return jnp.concatenate([pieces[2 * e] + pieces[2 * e + 1] for e in range(len(pieces) // 2)], axis=1)


def _ffn_kernel(x_ref, gpre_ref, gpost_ref, wg_ref, wu_ref, wo_ref, o_ref, h_ref, acc_ref):
    f = pl.program_id(1)

    @pl.when(f == 0)
    def _():
        h_ref[...] = _rms(x_ref[...], gpre_ref[...]).astype(BF16)
        acc_ref[...] = jnp.zeros_like(acc_ref)

    h = h_ref[...]
    gate = _dot(h, wg_ref[...])
    up = _dot(h, wu_ref[...])
    acc_ref[...] += _dot((_silu(gate) * up).astype(BF16), wo_ref[...])

    @pl.when(f == pl.num_programs(1) - 1)
    def _():
        o_ref[...] = x_ref[...] + 0.5 * _rms(acc_ref[...], gpost_ref[...])


def _ffn(x, g_pre, g_post, w_in, w_out, *, tm, tf=256):
    rows, d = x.shape
    d_ff = w_out.shape[0]
    nf = d_ff // tf
    assert rows % tm == 0 and d_ff % tf == 0
    return pl.pallas_call(
        _ffn_kernel,
        grid=(rows // tm, nf),
        in_specs=[
            pl.BlockSpec((tm, d), lambda i, f: (i, 0)),
            pl.BlockSpec((1, d), lambda i, f: (0, 0)),
            pl.BlockSpec((1, d), lambda i, f: (0, 0)),
            pl.BlockSpec((d, tf), lambda i, f: (0, f)),
            pl.BlockSpec((d, tf), lambda i, f: (0, nf + f)),
            pl.BlockSpec((tf, d), lambda i, f: (f, 0)),
        ],
        out_specs=pl.BlockSpec((tm, d), lambda i, f: (i, 0)),
        out_shape=jax.ShapeDtypeStruct((rows, d), F32),
        scratch_shapes=[pltpu.VMEM((tm, d), BF16), pltpu.VMEM((tm, d), F32)],
        compiler_params=_params("parallel", "arbitrary"),
        name="ffn",
    )(x, g_pre, g_post, w_in, w_in, w_out)


def _oproj_kernel(o_ref, x_ref, w_ref, g_ref, out_ref):
    y = _dot(o_ref[...].astype(BF16), w_ref[...])
    out_ref[...] = x_ref[...] + _rms(y, g_ref[...])


def _oproj(o, x, w, g, *, tm):
    rows, d = x.shape
    return pl.pallas_call(
        _oproj_kernel,
        grid=(rows // tm,),
        in_specs=[
            pl.BlockSpec((tm, o.shape[1]), lambda i: (i, 0)),
            pl.BlockSpec((tm, d), lambda i: (i, 0)),
            pl.BlockSpec(w.shape, lambda i: (0, 0)),
            pl.BlockSpec((1, d), lambda i: (0, 0)),
        ],
        out_specs=pl.BlockSpec((tm, d), lambda i: (i, 0)),
        out_shape=jax.ShapeDtypeStruct((rows, d), F32),
        compiler_params=_params("parallel"),
        name="oproj",
    )(o, x, w, g)


def _log_sigmoid(z):
    return jnp.minimum(z, 0.0) - jnp.log1p(jnp.exp(-jnp.abs(z)))


def _seq_specs(rows, width, tm, seq_len):
    tiles = seq_len // tm
    return (pl.BlockSpec((1, width, tm), lambda i: (i // tiles, 0, i % tiles)),
            jax.ShapeDtypeStruct((rows // seq_len, width, seq_len), F32))


def _fox_proj_kernel(x_ref, g_ref, wqkv_ref, wf_ref, bf_ref, q_ref, k_ref, v_ref, lf_ref, *copies, channel_major):
    h = _rms(x_ref[...], g_ref[...]).astype(BF16)
    d = q_ref.shape[1]
    qkv = _dot(h, wqkv_ref[...])
    q_ref[...] = qkv[:, :d]
    k, v = qkv[:, d:2 * d], qkv[:, 2 * d:]
    if channel_major:
        k_ref[0] = k.T
        v_ref[0] = v.T
        copies[0][...] = k.astype(BF16)
        copies[1][...] = v.astype(BF16)
    else:
        k_ref[...] = k
        v_ref[...] = v
    fl = _dot(h, wf_ref[...])
    lf_ref[...] = _log_sigmoid(fl[:, :FOX_HEADS] + bf_ref[...])


def _fox_proj(x, g, wqkv, wf, bf, *, tm, seq_len=None):
    rows, d = x.shape
    row_spec = pl.BlockSpec((tm, d), lambda i: (i, 0))
    full = lambda a: pl.BlockSpec(a.shape, lambda i: (0, 0))
    row_sds = jax.ShapeDtypeStruct((rows, d), F32)
    if seq_len is None:
        kv_specs, kv_shapes = [row_spec] * 2, [row_sds] * 2
        extra_specs, extra_shapes = [], []
    else:
        spec, sds = _seq_specs(rows, d, tm, seq_len)
        kv_specs, kv_shapes = [spec] * 2, [sds] * 2
        extra_specs, extra_shapes = [row_spec] * 2, [jax.ShapeDtypeStruct((rows, d), BF16)] * 2
    return pl.pallas_call(
        functools.partial(_fox_proj_kernel, channel_major=seq_len is not None),
        grid=(rows // tm,),
        in_specs=[row_spec, full(g), full(wqkv), full(wf), full(bf)],
        out_specs=[row_spec] + kv_specs + [pl.BlockSpec((tm, FOX_HEADS), lambda i: (i, 0))] + extra_specs,
        out_shape=[row_sds] + kv_shapes + [jax.ShapeDtypeStruct((rows, FOX_HEADS), F32)] + extra_shapes,
        compiler_params=_params("parallel"),
        name="fox_proj",
    )(x, g, wqkv, wf, bf)


def _rope_block(x, cos, sin_signed):
    lane = _iota(x.shape, 1)
    first_half = (lane % HEAD_DIM) < (HEAD_DIM // 2)
    rot = jnp.where(first_half, pltpu.roll(x, LANES - HEAD_DIM // 2, 1), pltpu.roll(x, HEAD_DIM // 2, 1))
    return x * cos + rot * sin_signed


def _nsa_proj_kernel(x_ref, g_ref, wr_ref, wp_ref, wg_ref, cos_ref, sin_ref, q_ref, gate_ref, *kv_refs, channel_major):
    h = _rms(x_ref[...], g_ref[...]).astype(BF16)
    cos = cos_ref[...]
    sin = sin_ref[...]
    r = _dot(h, wr_ref[...])
    nq = q_ref.shape[1]
    nkv = NSA_KV_HEADS * HEAD_DIM
    roped = [_rope_block(r[:, b * LANES:(b + 1) * LANES], cos, sin) for b in range(r.shape[1] // LANES)]
    nqb, nkb = nq // LANES, nkv // LANES
    q_ref[...] = jnp.concatenate(roped[:nqb], axis=1)
    kc, ks, kw = (jnp.concatenate(roped[nqb + e * nkb:nqb + (e + 1) * nkb], axis=1) for e in range(3))
    p = _dot(h, wp_ref[...])
    vc, vs, vw = (p[:, e * nkv:(e + 1) * nkv] for e in range(3))
    kv = (kc, vc, ks, vs, kw, vw)
    if channel_major:
        for ref, a in zip(kv_refs[:6], kv):
            ref[0] = a.T
        kv_refs[6][...] = kc
        kv_refs[7][...] = vc
        for ref, a in zip(kv_refs[8:], kv[2:]):
            ref[...] = a.astype(BF16)
    else:
        for ref, a in zip(kv_refs, kv):
            ref[...] = a
    gl = _dot(h, wg_ref[...])
    gate_ref[...] = jax.nn.sigmoid(gl[:, :gate_ref.shape[1]])


def _nsa_proj(x, g, wr, wp, wg, cos, sin, *, tm, seq_len=None):
    rows, d = x.shape
    nq = NSA_HEADS * HEAD_DIM
    nkv = NSA_KV_HEADS * HEAD_DIM
    ngate = 3 * NSA_HEADS
    n_tab = cos.shape[0] // tm
    row = lambda c: pl.BlockSpec((tm, c), lambda i: (i, 0))
    full = lambda a: pl.BlockSpec(a.shape, lambda i: (0, 0))
    tab = pl.BlockSpec((tm, LANES), lambda i: (i % n_tab, 0))
    sds = lambda c, dt=F32: jax.ShapeDtypeStruct((rows, c), dt)
    if seq_len is None:
        kv_specs, kv_shapes = [row(nkv)] * 6, [sds(nkv)] * 6
    else:
        spec, shape = _seq_specs(rows, nkv, tm, seq_len)
        kv_specs = [spec] * 6 + [row(nkv)] * 6
        kv_shapes = [shape] * 6 + [sds(nkv)] * 2 + [sds(nkv, BF16)] * 4
    return pl.pallas_call(
        functools.partial(_nsa_proj_kernel, channel_major=seq_len is not None),
        grid=(rows // tm,),
        in_specs=[row(d), full(g), full(wr), full(wp), full(wg), tab, tab],
        out_specs=[row(nq), row(ngate)] + kv_specs,
        out_shape=[sds(nq), sds(ngate)] + kv_shapes,
        compiler_params=_params("parallel"),
        name="nsa_proj",
    )(x, g, wr, wp, wg, cos, sin)


def _cumsum_kernel(lf_ref, ccol_ref, crow_ref, *, blk):
    t = lf_ref.shape[1]
    lower = jnp.where(_iota((blk, blk), 1) <= _iota((blk, blk), 0), 1.0, 0.0).astype(BF16)
    carry = jnp.zeros((1, lf_ref.shape[2]), F32)
    for b in range(t // blk):
        cs = _dot3_r(lower, lf_ref[0, b * blk:(b + 1) * blk, :]) + carry
        ccol_ref[0, b * blk:(b + 1) * blk, :] = cs
        carry = cs[blk - 1:blk, :]
    nh = lf_ref.shape[2]
    eye = jnp.where(_iota((nh, nh), 0) == _iota((nh, nh), 1), 1.0, 0.0).astype(BF16)
    crow_ref[0] = _dot3_nt_r(eye, ccol_ref[0])


def _cumsum(lf):
    n, t, h = lf.shape
    return pl.pallas_call(
        functools.partial(_cumsum_kernel, blk=256),
        grid=(n,),
        in_specs=[pl.BlockSpec((1, t, h), lambda i: (i, 0, 0))],
        out_specs=[pl.BlockSpec((1, t, h), lambda i: (i, 0, 0)), pl.BlockSpec((1, h, t), lambda i: (i, 0, 0))],
        out_shape=[jax.ShapeDtypeStruct((n, t, h), F32), jax.ShapeDtypeStruct((n, h, t), F32)],
        compiler_params=_params("parallel"),
        name="fox_cumsum",
    )(lf)


def _fox_attn_kernel(q_ref, k_ref, v_ref, ccol_ref, crow_ref, o_ref, m_ref, l_ref, acc_ref, *, tq, tk):
    hp = pl.program_id(1)
    i = pl.program_id(2)
    rows = 2 * tq
    q = q_ref[0] * SCALE2
    lane = _iota((tq, LANES), 1)
    qs = jnp.concatenate([jnp.where(lane < HEAD_DIM, q, 0.0), jnp.where(lane >= HEAD_DIM, q, 0.0)],
                         axis=0).astype(BF16)
    cc = ccol_ref[0]
    hl = _iota(cc.shape, 1)
    ct = jnp.concatenate([jnp.sum(jnp.where(hl == 2 * hp + e, cc, 0.0), axis=1, keepdims=True) for e in range(2)],
                         axis=0)
    ct = jnp.broadcast_to(ct * LOG2E, (rows, LANES))
    lane_minus_row = _iota((rows, LANES), 1) - _iota((rows, LANES), 0) % tq
    _init_stats(m_ref, l_ref, acc_ref)

    def tile(j, causal):
        k0 = pl.multiple_of(j * tk, tk)
        kt = k_ref[0, pl.ds(k0, tk), :]
        vt = v_ref[0, pl.ds(k0, tk), :]
        s = _dot_nt(qs, kt)
        cr = crow_ref[0, 0, :, pl.ds(k0, tk)] * LOG2E
        cols = []
        for c in range(0, tk, LANES):
            ck = jnp.concatenate([jnp.broadcast_to(cr[e:e + 1, c:c + LANES], (tq, LANES)) for e in range(2)], axis=0)
            sc = s[:, c:c + LANES] + (ct - ck)
            if causal:
                sc = jnp.where(lane_minus_row <= i * tq - k0 - c, sc, MASKED)
            cols.append(sc)
        p, alpha = _softmax_step(jnp.concatenate(cols, axis=1), m_ref, l_ref)
        acc_ref[...] = alpha * acc_ref[...] + _dot(p, vt)

    def run(lo, hi, causal):
        def body(j, carry):
            tile(j, causal)
            return carry
        lax.fori_loop(lo, hi, body, 0)

    n_below = (i * tq) // tk
    run(0, n_below, False)
    run(n_below, (i * tq + tq + tk - 1) // tk, True)
    o = _normalize(acc_ref[...], l_ref)
    o_ref[0] = jnp.where(lane < HEAD_DIM, o[:tq], o[tq:])


def _fox_attn(q, k, v, ccol, crow, *, tq=512, tk=512):
    n, t, d = q.shape
    npair = d // LANES
    rows = 2 * tq
    return pl.pallas_call(
        functools.partial(_fox_attn_kernel, tq=tq, tk=tk),
        grid=(n, npair, t // tq),
        in_specs=[
            pl.BlockSpec((1, tq, LANES), lambda b, h, i: (b, i, h)),
            pl.BlockSpec((1, t, LANES), lambda b, h, i: (b, 0, h)),
            pl.BlockSpec((1, t, LANES), lambda b, h, i: (b, 0, h)),
            pl.BlockSpec((1, tq, ccol.shape[2]), lambda b, h, i: (b, i, 0)),
            pl.BlockSpec((1, 1, 2, t), lambda b, h, i: (b, h, 0, 0)),
        ],
        out_specs=pl.BlockSpec((1, tq, LANES), lambda b, h, i: (b, i, h)),
        out_shape=jax.ShapeDtypeStruct((n, t, d), F32),
        scratch_shapes=[pltpu.VMEM((rows, LANES), F32)] * 3,
        compiler_params=_params("parallel", "parallel", "arbitrary"),
        name="fox_attn",
    )(q, k, v, ccol, crow)


def _fox_decode_kernel(pt_ref, q_ref, kn_ref, vn_ref, lfn_ref, *rest, nq, rows):
    npg = PAGES_PER_STEP
    k_refs, v_refs, lf_refs = rest[:npg], rest[npg:2 * npg], rest[2 * npg:3 * npg]
    o_ref = rest[3 * npg]
    qbd_ref, kcat_ref, vcat_ref, m_ref, l_ref, acc_ref, carry_ref, ncum_ref = rest[3 * npg + 1:]
    del pt_ref
    jj = pl.program_id(1)
    d = q_ref.shape[2]
    nh = FOX_HEADS
    pg = PAGE_SIZE
    r2 = _iota((rows, d), 0)
    block_diag = (r2 % nh == _iota((rows, d), 1) // HEAD_DIM) & (r2 < nq * nh)
    re = _iota((rows, nh), 0)
    head_of_row = jnp.where((re % nh == _iota((rows, nh), 1)) & (re < nq * nh), 1.0, 0.0).astype(BF16)

    @pl.when(jj == 0)
    def _():
        q = q_ref[0] * SCALE2
        pad = [jnp.zeros((rows - nq * nh, d), F32)] if rows > nq * nh else []
        qb = jnp.concatenate([jnp.broadcast_to(q[t:t + 1], (nh, d)) for t in range(nq)] + pad, axis=0)
        qbd = jnp.where(block_diag, qb, 0.0).astype(BF16)
        qbd_ref[...] = qbd
        kn = _pad_rows(kn_ref[0], pg).astype(BF16)
        vn = _pad_rows(vn_ref[0], pg).astype(BF16)
        sn = _dot_nt(qbd, kn)
        xn = _dot3_nt_r(head_of_row, _pad_rows(lfn_ref[0], pg)) * LOG2E
        t_r = _iota((rows, 1), 0) // nh
        cum = []
        c = jnp.zeros((rows, 1), F32)
        for s in range(nq):
            c = c + xn[:, s:s + 1]
            cum.append(c)
        lane = _iota((rows, pg), 1)
        ncum = cum[-1]
        cpre = jnp.broadcast_to(cum[-1], (rows, pg))
        for s in reversed(range(nq - 1)):
            ncum = jnp.where(t_r == s, cum[s], ncum)
            cpre = jnp.where(lane == s, cum[s], cpre)
        mask = (lane <= t_r) & (lane < nq)
        logit = jnp.where(mask, sn + ncum - cpre, MASKED)
        m = jnp.maximum(jnp.max(logit, axis=-1, keepdims=True), NEG_INF)
        p = jnp.exp2(logit - m)
        m_ref[...] = jnp.broadcast_to(m, m_ref.shape)
        l_ref[...] = p
        acc_ref[...] = _dot(p.astype(BF16), vn)
        carry_ref[...] = jnp.zeros(carry_ref.shape, F32)
        ncum_ref[...] = jnp.broadcast_to(ncum, ncum_ref.shape)

    for p in range(npg):
        kcat_ref[:, p * pg:(p + 1) * pg] = k_refs[p][...].astype(BF16)
        vcat_ref[:, p * pg:(p + 1) * pg] = v_refs[p][...].astype(BF16)
    s = _dot(qbd_ref[...], kcat_ref[...])
    lfcat = jnp.concatenate([lf_refs[p][...] for p in range(npg)], axis=1)
    x = _dot3_r(head_of_row, lfcat) * LOG2E
    y = jnp.concatenate([x[:, p * pg:(p + 1) * pg] for p in range(npg)], axis=0)
    ri, ci = _iota((pg, 2 * pg), 0), _iota((pg, 2 * pg), 1)
    later_and_total = jnp.where((ri >= ci) | (ci >= pg), 1.0, 0.0).astype(BF16)
    z = _dot3_l(y, later_and_total)
    c = carry_ref[...]
    ncum = ncum_ref[...]
    cols = [None] * npg
    for p in reversed(range(npg)):
        zp = z[p * rows:(p + 1) * rows]
        cols[p] = s[:, p * pg:(p + 1) * pg] + (zp[:, :pg] - y[p * rows:(p + 1) * rows] + c + ncum)
        c = c + zp[:, pg:]
    carry_ref[...] = c
    p, alpha = _softmax_step(jnp.concatenate(cols, axis=1), m_ref, l_ref)
    pv = _dot_nt(p, vcat_ref[...])
    for b in range(0, d, LANES):
        acc_ref[:, b:b + LANES] = alpha * acc_ref[:, b:b + LANES] + pv[:, b:b + LANES]

    @pl.when(jj == pl.num_programs(1) - 1)
    def _():
        o = jnp.where(block_diag, _normalize(acc_ref[...], l_ref), 0.0)
        for t in range(nq):
            o_ref[0, t:t + 1, :] = jnp.sum(o[t * nh:(t + 1) * nh], axis=0, keepdims=True)


def _fox_decode(page_table, q, k_new, v_new, lf_new, cache_kt, cache_vt, cache_lft, layer):
    b, nq, d = q.shape
    n_pages = page_table.shape[1]
    npg = PAGES_PER_STEP
    assert n_pages % npg == 0
    rows = -(-nq * FOX_HEADS // (2 * SUBLANES)) * (2 * SUBLANES)

    def page_spec(height, p):
        return pl.BlockSpec((None, None, height, PAGE_SIZE),
                            lambda i, jj, pt: (layer, pt[i, n_pages - npg * (jj + 1) + p], 0, 0))

    tok = lambda w: pl.BlockSpec((1, nq, w), lambda i, jj, pt: (i, 0, 0))
    grid_spec = pltpu.PrefetchScalarGridSpec(
        num_scalar_prefetch=1,
        grid=(b, n_pages // npg),
        in_specs=[tok(d), tok(d), tok(d), tok(FOX_HEADS)]
        + [page_spec(d, p) for p in range(npg)] * 2 + [page_spec(FOX_HEADS, p) for p in range(npg)],
        out_specs=pl.BlockSpec((1, nq, d), lambda i, jj, pt: (i, 0, 0)),
        scratch_shapes=[
            pltpu.VMEM((rows, d), BF16),
            pltpu.VMEM((d, npg * PAGE_SIZE), BF16),
            pltpu.VMEM((d, npg * PAGE_SIZE), BF16),
            pltpu.VMEM((rows, LANES), F32),
            pltpu.VMEM((rows, LANES), F32),
            pltpu.VMEM((rows, d), F32),
            pltpu.VMEM((rows, LANES), F32),
            pltpu.VMEM((rows, LANES), F32),
        ],
    )
    return pl.pallas_call(
        functools.partial(_fox_decode_kernel, nq=nq, rows=rows),
        grid_spec=grid_spec,
        out_shape=jax.ShapeDtypeStruct((b, nq, d), F32),
        compiler_params=_params("parallel", "arbitrary"),
        name="fox_decode",
    )(page_table, q, k_new, v_new, lf_new, *([cache_kt] * npg), *([cache_vt] * npg), *([cache_lft] * npg))


def _compress_finish(a, b, w2):
    n = a.shape[0]
    hid = _silu(a + pltpu.roll(b, n - 1, 0))
    out = _dot(hid.astype(BF16), w2)
    return jnp.where(_iota(out.shape, 0) < n - 1, out, 0.0)


def _compress_rows(read_rows, n_chunk, pe_ref, w1_ref, w2_ref):
    half = CMP_STRIDE
    a = jnp.zeros((n_chunk, w2_ref.shape[1]), F32)
    b = jnp.zeros((n_chunk, w2_ref.shape[1]), F32)
    for j in range(half):
        xj = read_rows(j)
        a = a + _dot((xj + pe_ref[j:j + 1, :]).astype(BF16), w1_ref[j])
        b = b + _dot((xj + pe_ref[half + j:half + j + 1, :]).astype(BF16), w1_ref[half + j])
    return _compress_finish(a, b, w2_ref[...])


def _compress_kernel(*refs, nblk):
    x_refs = refs[:nblk]
    pe_ref, w1_ref, w2_ref, o_ref = refs[nblk:]
    n_chunk = o_ref.shape[1]
    read = lambda j: jnp.concatenate([r[0, pl.ds(j, n_chunk, stride=CMP_STRIDE), :] for r in x_refs], axis=1)
    o_ref[0] = _compress_rows(read, n_chunk, pe_ref, w1_ref, w2_ref)


def _compress(x, pe, w1, w2):
    n, t, gkv = x.shape
    assert CMP_LEN == 2 * CMP_STRIDE and t % CMP_STRIDE == 0
    c = t // CMP_STRIDE
    nblk = gkv // LANES
    full = lambda a: pl.BlockSpec(a.shape, lambda i: (0,) * a.ndim)
    lane_block = lambda e: pl.BlockSpec((1, t, LANES), lambda i: (i, 0, e))
    return pl.pallas_call(
        functools.partial(_compress_kernel, nblk=nblk),
        grid=(n,),
        in_specs=[lane_block(e) for e in range(nblk)] + [full(pe), full(w1), full(w2)],
        out_specs=pl.BlockSpec((1, c, gkv), lambda i: (i, 0, 0)),
        out_shape=jax.ShapeDtypeStruct((n, c, gkv), F32),
        compiler_params=_params("parallel"),
        name="nsa_compress",
    )(*([x] * nblk), pe, w1, w2)


def _nsa_attn_kernel(q_ref, kc_ref, vc_ref, ks_ref, vs_ref, kw_ref, vw_ref, gate_ref, expand_ref, o_ref,
                     m_ref, l_ref, acc_ref, bias_ref, *, tq, tk, n_cmp, n_sel):
    g = pl.program_id(1)
    i = pl.program_id(2)
    par = g % 2
    hp = NSA_GROUP
    rows = hp * tq
    q = q_ref[0] * SCALE2
    qs = jnp.concatenate([_to_half(_head_block(q, j), j % 2, par) for j in range(hp)], axis=0).astype(BF16)
    qpos = i * tq + _iota((rows, 1), 0) % tq
    row_minus_lane = _iota((rows, LANES), 0) % tq - _iota((rows, LANES), 1)

    ncp = kc_ref.shape[1]
    ci = _iota((1, ncp), 1)
    mask = (ci * CMP_STRIDE + CMP_LEN - 1 <= qpos) & (ci < n_cmp)
    p = _masked_softmax2(_dot_nt(qs, kc_ref[0].astype(BF16)), mask)
    o_cmp = _dot(p.astype(BF16), vc_ref[0].astype(BF16))
    psum = p[0:tq]
    for j in range(1, hp):
        psum = psum + p[j * tq:(j + 1) * tq]
    nsr = -(-n_sel // SUBLANES) * SUBLANES
    imp = _dot3_nt_r(_cover_matrix((nsr, ncp), 1, n_cmp, n_sel), psum)
    tpos = i * tq + _iota((1, tq), 1)
    sel_t = _top_mask(_selection_scores(imp, tpos, n_sel, 0), n_sel, min(SLC_TOP, n_sel), 0)
    sel = jnp.concatenate([sel_t, jnp.zeros((LANES - nsr, tq), F32)], axis=0).T.astype(BF16)
    bias_ref[...] = jnp.where(_dot(sel, expand_ref[...]) > 0.5, 0.0, MASKED)

    def attend(k_ref, v_ref, k0, width, transform):
        kt = k_ref[0, pl.ds(k0, width), :]
        vt = v_ref[0, pl.ds(k0, width), :]
        p, alpha = _softmax_step(transform(_dot_nt(qs, kt)), m_ref, l_ref)
        acc_ref[...] = alpha * acc_ref[...] + _dot(p, vt)

    def loop(lo, hi, fn):
        def body(j, carry):
            fn(j)
            return carry
        lax.fori_loop(lo, hi, body, 0)

    def slc_tile(j, causal):
        k0 = pl.multiple_of(j * tk, tk)

        def transform(s):
            b = bias_ref[:, pl.ds(k0, tk)]
            s = s + jnp.concatenate([b] * hp, axis=0)
            if not causal:
                return s
            cols = [jnp.where(row_minus_lane >= k0 + c - i * tq, s[:, c:c + LANES], MASKED)
                    for c in range(0, tk, LANES)]
            return jnp.concatenate(cols, axis=1)

        attend(ks_ref, vs_ref, k0, tk, transform)

    _init_stats(m_ref, l_ref, acc_ref)
    n_below = (i * tq) // tk
    loop(0, n_below, lambda j: slc_tile(j, False))
    loop(n_below, (i * tq + tq + tk - 1) // tk, lambda j: slc_tile(j, True))
    o_slc = _normalize(acc_ref[...], l_ref)

    win_w = WINDOW + tq
    k0w = pl.multiple_of(jnp.maximum(i * tq - WINDOW, 0), tq)

    def win_transform(s):
        base = i * tq - k0w
        cols = []
        for c in range(0, win_w, LANES):
            sc = jnp.where(row_minus_lane >= c - base, s[:, c:c + LANES], MASKED)
            cols.append(jnp.where(row_minus_lane < WINDOW + c - base, sc, MASKED))
        return jnp.concatenate(cols, axis=1)

    _init_stats(m_ref, l_ref, acc_ref)
    attend(kw_ref, vw_ref, k0w, win_w, win_transform)
    o_win = _normalize(acc_ref[...], l_ref)

    gates = gate_ref[0]
    gl = _iota(gates.shape, 1)
    outs = []
    for j in range(hp):
        head = g * hp + j
        gc = [jnp.sum(jnp.where(gl == c * NSA_HEADS + head, gates, 0.0), axis=1, keepdims=True) for c in range(3)]
        sl = slice(j * tq, (j + 1) * tq)
        outs.append(_to_half(gc[0] * o_cmp[sl] + gc[1] * o_slc[sl] + gc[2] * o_win[sl], par, j % 2))
    o_ref[0] = jnp.concatenate([outs[2 * e] + outs[2 * e + 1] for e in range(hp // 2)], axis=1)


def _nsa_attn(q, kcmp, vcmp, ks, vs, kw, vw, gates, *, tq=256, tk=512):
    n, t, _ = q.shape
    gw = NSA_GROUP * HEAD_DIM
    ncp = kcmp.shape[1]
    n_cmp = t // CMP_STRIDE - CMP_LEN // CMP_STRIDE + 1
    n_sel = -(-t // SLC_BLOCK)
    assert n_sel <= LANES and t % tq == 0 and t % tk == 0 and WINDOW % tq == 0 and t >= WINDOW + tq
    rows = NSA_GROUP * tq
    expand = (jnp.arange(t, dtype=jnp.int32)[None, :] // SLC_BLOCK
              == jnp.arange(LANES, dtype=jnp.int32)[:, None]).astype(BF16)
    pair = lambda r: pl.BlockSpec((1, r, LANES), lambda b, g, i: (b, 0, g // 2))
    return pl.pallas_call(
        functools.partial(_nsa_attn_kernel, tq=tq, tk=tk, n_cmp=n_cmp, n_sel=n_sel),
        grid=(n, NSA_KV_HEADS, t // tq),
        in_specs=[pl.BlockSpec((1, tq, gw), lambda b, g, i: (b, i, g)), pair(ncp), pair(ncp),
                  pair(t), pair(t), pair(t), pair(t),
                  pl.BlockSpec((1, tq, gates.shape[2]), lambda b, g, i: (b, i, 0)),
                  pl.BlockSpec(expand.shape, lambda b, g, i: (0, 0))],
        out_specs=pl.BlockSpec((1, tq, gw), lambda b, g, i: (b, i, g)),
        out_shape=jax.ShapeDtypeStruct(q.shape, F32),
        scratch_shapes=[pltpu.VMEM((rows, LANES), F32)] * 3 + [pltpu.VMEM((tq, t), F32)],
        compiler_params=_params("parallel", "parallel", "arbitrary"),
        name="nsa_attn",
    )(q, kcmp, vcmp, ks, vs, kw, vw, gates, expand)


def _group_queries(q8, g, par):
    return jnp.concatenate([_to_half(_head_block(q8, h), h % 2, par)
                            for h in range(g * NSA_GROUP, (g + 1) * NSA_GROUP)], axis=0).astype(BF16)


def _nsa_decode_cmp_kernel(pt_ref, q_ref, *rest, past_len, n_cmp, n_sel):
    npg = NSA_PAGES_PER_STEP
    kpages, vpages = rest[:npg], rest[npg:2 * npg]
    pek_ref, w1k_ref, w2k_ref, pev_ref, w1v_ref, w2v_ref = rest[2 * npg:2 * npg + 6]
    o_ref, sel_ref = rest[2 * npg + 6:2 * npg + 8]
    xk_ref, xv_ref = rest[2 * npg + 8:]
    del pt_ref
    jj = pl.program_id(1)
    pg = PAGE_SIZE
    r0 = jj * (npg * pg)
    nblk = xk_ref.shape[0]
    for p in range(npg):
        dst = pl.ds(pl.multiple_of(r0 + p * pg, pg), pg)
        xk = kpages[p][...].T
        xv = vpages[p][...].T
        for e in range(nblk):
            xk_ref[e, dst, :] = xk[:, e * LANES:(e + 1) * LANES]
            xv_ref[e, dst, :] = xv[:, e * LANES:(e + 1) * LANES]

    @pl.when(jj == pl.num_programs(1) - 1)
    def _():
        n_chunk = xk_ref.shape[1] // CMP_STRIDE

        def compress(x_ref, pe_ref, w1_ref, w2_ref):
            read = lambda j: jnp.concatenate(
                [x_ref[e, pl.ds(j, n_chunk, stride=CMP_STRIDE), :] for e in range(nblk)], axis=1)
            return _compress_rows(read, n_chunk, pe_ref, w1_ref, w2_ref)

        kcmp = compress(xk_ref, pek_ref, w1k_ref, w2k_ref)
        vcmp = compress(xv_ref, pev_ref, w1v_ref, w2v_ref)
        ncp = kcmp.shape[0]
        q8 = _pad_rows(q_ref[0] * SCALE2, SUBLANES)
        ci = _iota((1, ncp), 1)
        cover = _cover_matrix((ncp, sel_ref.shape[3]), 0, n_cmp, n_sel)
        pos8 = past_len + _iota((SUBLANES, 1), 0)
        pieces = [None] * NSA_HEADS
        for g in range(NSA_KV_HEADS):
            par = g % 2
            lanes = slice((g // 2) * LANES, (g // 2 + 1) * LANES)
            qs = _group_queries(q8, g, par)
            pos = past_len + _iota((qs.shape[0], 1), 0) % SUBLANES
            mask = (ci * CMP_STRIDE + CMP_LEN - 1 <= pos) & (ci < n_cmp)
            p = _masked_softmax2(_dot_nt(qs, kcmp[:, lanes].astype(BF16)), mask)
            o = _dot(p.astype(BF16), vcmp[:, lanes].astype(BF16))
            psum = p[0:SUBLANES]
            for j in range(1, NSA_GROUP):
                psum = psum + p[j * SUBLANES:(j + 1) * SUBLANES]
            imp = _dot3_l(psum, cover)
            sel_ref[0, g] = _top_mask(_selection_scores(imp, pos8, n_sel, 1), n_sel, min(SLC_TOP, n_sel), 1)
            for j in range(NSA_GROUP):
                h = g * NSA_GROUP + j
                pieces[h] = _to_half(o[j * SUBLANES:(j + 1) * SUBLANES], par, h % 2)
        o_ref[0] = _assemble_heads(pieces)


def _nsa_decode_cmp(page_table, q, cache_kt, cache_vt, wk, wv, layer, *, nsp):
    b, nq, dq = q.shape
    n_pages = page_table.shape[1]
    npg = NSA_PAGES_PER_STEP
    past_len = n_pages * PAGE_SIZE
    assert n_pages % npg == 0 and nq <= SUBLANES and nq < CMP_STRIDE and CMP_LEN == 2 * CMP_STRIDE
    n_chunk = (past_len + nq) // CMP_STRIDE
    assert n_chunk == n_pages * CHUNKS_PER_PAGE
    n_cmp = n_chunk - CMP_LEN // CMP_STRIDE + 1
    n_sel = -(-(past_len + nq) // SLC_BLOCK)
    assert n_sel <= nsp
    gkv = NSA_KV_HEADS * HEAD_DIM

    def page_spec(p):
        return pl.BlockSpec((None, None, gkv, PAGE_SIZE), lambda i, jj, pt: (layer, pt[i, npg * jj + p], 0, 0))

    full = lambda a: pl.BlockSpec(a.shape, lambda i, jj, pt: (0,) * a.ndim)
    grid_spec = pltpu.PrefetchScalarGridSpec(
        num_scalar_prefetch=1,
        grid=(b, n_pages // npg),
        in_specs=[pl.BlockSpec((1, nq, dq), lambda i, jj, pt: (i, 0, 0))]
        + [page_spec(p) for p in range(npg)] * 2 + [full(a) for a in wk] + [full(a) for a in wv],
        out_specs=[pl.BlockSpec((1, SUBLANES, dq), lambda i, jj, pt: (i, 0, 0)),
                   pl.BlockSpec((1, NSA_KV_HEADS, SUBLANES, nsp), lambda i, jj, pt: (i, 0, 0, 0))],
        scratch_shapes=[pltpu.VMEM((gkv // LANES, past_len, LANES), F32)] * 2,
    )
    return pl.pallas_call(
        functools.partial(_nsa_decode_cmp_kernel, past_len=past_len, n_cmp=n_cmp, n_sel=n_sel),
        grid_spec=grid_spec,
        out_shape=[jax.ShapeDtypeStruct((b, SUBLANES, dq), F32),
                   jax.ShapeDtypeStruct((b, NSA_KV_HEADS, SUBLANES, nsp), F32)],
        compiler_params=_params("parallel", "arbitrary"),
        name="nsa_decode_cmp",
    )(page_table, q, *([cache_kt] * npg), *([cache_vt] * npg), *wk, *wv)


def _nsa_decode_slc_kernel(pt_ref, q_ref, kn_ref, vn_ref, sel_ref, *rest, nq, past_len):
    npg = NSA_PAGES_PER_STEP
    kpages, vpages = rest[:npg], rest[npg:2 * npg]
    o_ref = rest[2 * npg]
    qs_ref, kcat_ref, vcat_ref, m_ref, l_ref, acc_ref = rest[2 * npg + 1:]
    del pt_ref
    jj = pl.program_id(1)
    pg = PAGE_SIZE
    npair = NSA_KV_HEADS // 2
    nsp = sel_ref.shape[3]
    selrows = [jnp.concatenate([sel_ref[0, 2 * gp + gi] for gi in range(2) for _ in range(NSA_GROUP)], axis=0)
               for gp in range(npair)]
    rows = selrows[0].shape[0]

    @pl.when(jj == 0)
    def _():
        q8 = _pad_rows(q_ref[0] * SCALE2, SUBLANES)
        kn = _pad_rows(kn_ref[0], pg).astype(BF16)
        vn = _pad_rows(vn_ref[0], pg).astype(BF16)
        t_r = _iota((rows, 1), 0) % SUBLANES
        lane = _iota((rows, pg), 1)
        for gp in range(npair):
            lanes = slice(gp * LANES, (gp + 1) * LANES)
            qs = jnp.concatenate([_group_queries(q8, 2 * gp + gi, gi) for gi in range(2)], axis=0)
            qs_ref[gp] = qs
            _init_stats(m_ref.at[gp], l_ref.at[gp], acc_ref.at[gp])
            new_block = selrows[gp][:, past_len // SLC_BLOCK:past_len // SLC_BLOCK + 1]
            mask = (lane <= t_r) & (lane < nq) & (new_block > 0.5)
            s = jnp.where(mask, _dot_nt(qs, kn[:, lanes]), MASKED)
            p, alpha = _softmax_step(s, m_ref.at[gp], l_ref.at[gp])
            acc_ref[gp] = alpha * acc_ref[gp] + _dot(p, vn[:, lanes])

    for p in range(npg):
        kcat_ref[:, p * pg:(p + 1) * pg] = kpages[p][...].astype(BF16)
        vcat_ref[:, p * pg:(p + 1) * pg] = vpages[p][...].astype(BF16)
    width = npg * pg
    block_of_key = jj * (width // SLC_BLOCK) + _iota((nsp, width), 1) // SLC_BLOCK
    expand = jnp.where(block_of_key == _iota((nsp, width), 0), 1.0, 0.0).astype(BF16)
    for gp in range(npair):
        rws = slice(gp * LANES, (gp + 1) * LANES)
        selx = _dot(selrows[gp].astype(BF16), expand)
        s = jnp.where(selx > 0.5, _dot(qs_ref[gp], kcat_ref[rws, :]), MASKED)
        p, alpha = _softmax_step(s, m_ref.at[gp], l_ref.at[gp])
        acc_ref[gp] = alpha * acc_ref[gp] + _dot_nt(p, vcat_ref[rws, :])

    @pl.when(jj == pl.num_programs(1) - 1)
    def _():
        pieces = [None] * NSA_HEADS
        for gp in range(npair):
            o = _normalize(acc_ref[gp], l_ref.at[gp])
            for gi in range(2):
                for j in range(NSA_GROUP):
                    h = (2 * gp + gi) * NSA_GROUP + j
                    r0 = (gi * NSA_GROUP + j) * SUBLANES
                    pieces[h] = _to_half(o[r0:r0 + SUBLANES], gi, h % 2)
        o_ref[0] = _assemble_heads(pieces)


def _nsa_decode_slc(page_table, q, k_new, v_new, sel, cache_kt, cache_vt, layer):
    b, nq, dq = q.shape
    gkv = k_new.shape[2]
    n_pages = page_table.shape[1]
    npg = NSA_PAGES_PER_STEP
    past_len = n_pages * PAGE_SIZE
    assert n_pages % npg == 0 and past_len % SLC_BLOCK == 0 and nq <= SLC_BLOCK
    npair = NSA_KV_HEADS // 2
    rows = 2 * NSA_GROUP * SUBLANES

    def page_spec(p):
        return pl.BlockSpec((None, None, gkv, PAGE_SIZE), lambda i, jj, pt: (layer, pt[i, npg * jj + p], 0, 0))

    tok = lambda w: pl.BlockSpec((1, nq, w), lambda i, jj, pt: (i, 0, 0))
    grid_spec = pltpu.PrefetchScalarGridSpec(
        num_scalar_prefetch=1,
        grid=(b, n_pages // npg),
        in_specs=[tok(dq), tok(gkv), tok(gkv),
                  pl.BlockSpec((1,) + sel.shape[1:], lambda i, jj, pt: (i, 0, 0, 0))]
        + [page_spec(p) for p in range(npg)] * 2,
        out_specs=pl.BlockSpec((1, SUBLANES, dq), lambda i, jj, pt: (i, 0, 0)),
        scratch_shapes=[
            pltpu.VMEM((npair, rows, LANES), BF16),
            pltpu.VMEM((gkv, npg * PAGE_SIZE), BF16),
            pltpu.VMEM((gkv, npg * PAGE_SIZE), BF16),
            pltpu.VMEM((npair, rows, LANES), F32),
            pltpu.VMEM((npair, rows, LANES), F32),
            pltpu.VMEM((npair, rows, LANES), F32),
        ],
    )
    return pl.pallas_call(
        functools.partial(_nsa_decode_slc_kernel, nq=nq, past_len=past_len),
        grid_spec=grid_spec,
        out_shape=jax.ShapeDtypeStruct((b, SUBLANES, dq), F32),
        compiler_params=_params("parallel", "arbitrary"),
        name="nsa_decode_slc",
    )(page_table, q, k_new, v_new, sel, *([cache_kt] * npg), *([cache_vt] * npg))


def _nsa_decode_win_kernel(q_ref, kn_ref, vn_ref, sk_ref, sv_ref, ocmp_ref, oslc_ref, gate_ref,
                           o_ref, nk_ref, nv_ref, *, nq):
    nbuf = sk_ref.shape[1]
    pg = PAGE_SIZE
    q8 = _pad_rows(q_ref[0] * SCALE, SUBLANES)
    kn = _pad_rows(kn_ref[0], pg).astype(BF16)
    vn = _pad_rows(vn_ref[0], pg).astype(BF16)
    gates = _pad_rows(gate_ref[0], SUBLANES)
    gl = _iota(gates.shape, 1)
    rows = NSA_GROUP * SUBLANES
    t_r = _iota((rows, 1), 0) % SUBLANES
    dist_s = t_r + nbuf - _iota((1, nbuf), 1)
    mask_s = (dist_s >= 0) & (dist_s < WINDOW)
    lane_n = _iota((1, pg), 1)
    dist_n = t_r - lane_n
    mask_n = (dist_n >= 0) & (dist_n < WINDOW) & (lane_n < nq)
    pieces = [None] * NSA_HEADS
    for g in range(NSA_KV_HEADS):
        par = g % 2
        lanes = slice((g // 2) * LANES, (g // 2 + 1) * LANES)
        qs = _group_queries(q8, g, par)
        ss = jnp.where(mask_s, _dot_nt(qs, sk_ref[0, :, lanes].astype(BF16)), NEG_INF)
        sn = jnp.where(mask_n, _dot_nt(qs, kn[:, lanes]), NEG_INF)
        m = jnp.maximum(jnp.max(ss, axis=-1, keepdims=True), jnp.max(sn, axis=-1, keepdims=True))
        ps = jnp.where(mask_s, jnp.exp(ss - m), 0.0)
        pn = jnp.where(mask_n, jnp.exp(sn - m), 0.0)
        den = jnp.sum(ps, axis=-1, keepdims=True) + jnp.sum(pn, axis=-1, keepdims=True)
        ow = (_dot(ps.astype(BF16), sv_ref[0, :, lanes].astype(BF16)) + _dot(pn.astype(BF16), vn[:, lanes]))
        ow = ow / jnp.maximum(den, 1e-30)
        for j in range(NSA_GROUP):
            h = g * NSA_GROUP + j
            pieces[h] = _to_half(ow[j * SUBLANES:(j + 1) * SUBLANES], par, h % 2)
    o_win = _assemble_heads(pieces)
    head_of_lane = _iota((SUBLANES, o_win.shape[1]), 1) // HEAD_DIM
    branch_gate = []
    for c in range(3):
        gfull = jnp.zeros(o_win.shape, F32)
        for h in range(NSA_HEADS):
            col = jnp.sum(jnp.where(gl == c * NSA_HEADS + h, gates, 0.0), axis=1, keepdims=True)
            gfull = jnp.where(head_of_lane == h, col, gfull)
        branch_gate.append(gfull)
    o = branch_gate[0] * ocmp_ref[0] + branch_gate[1] * oslc_ref[0] + branch_gate[2] * o_win
    o_ref[0] = o[:nq]
    nk_ref[0, 0:nbuf - nq, :] = sk_ref[0, nq:nbuf, :]
    nk_ref[0, nbuf - nq:nbuf, :] = kn_ref[0]
    nv_ref[0, 0:nbuf - nq, :] = sv_ref[0, nq:nbuf, :]
    nv_ref[0, nbuf - nq:nbuf, :] = vn_ref[0]


def _nsa_decode_win(q, k_new, v_new, state_k, state_v, o_cmp, o_slc, gates):
    b, nq, dq = q.shape
    nbuf, gkv = state_k.shape[1:]
    assert nbuf == WINDOW and nq <= SUBLANES
    blk = lambda a: pl.BlockSpec((1,) + a.shape[1:], lambda i: (i, 0, 0))
    return pl.pallas_call(
        functools.partial(_nsa_decode_win_kernel, nq=nq),
        grid=(b,),
        in_specs=[blk(a) for a in (q, k_new, v_new, state_k, state_v, o_cmp, o_slc, gates)],
        out_specs=[blk(q), blk(state_k), blk(state_v)],
        out_shape=[jax.ShapeDtypeStruct(q.shape, F32), jax.ShapeDtypeStruct(state_k.shape, F32),
                   jax.ShapeDtypeStruct(state_v.shape, F32)],
        compiler_params=_params("parallel"),
        name="nsa_decode_win",
    )(q, k_new, v_new, state_k, state_v, o_cmp, o_slc, gates)


def _rope_tables(pos):
    half = HEAD_DIM // 2
    inv = ROPE_THETA ** (-jnp.arange(half, dtype=F32) / half)
    ang = pos.astype(F32)[:, None] * inv[None, :]
    cos, sin = jnp.cos(ang), jnp.sin(ang)
    reps = LANES // HEAD_DIM
    return (jnp.tile(jnp.concatenate([cos, cos], axis=1), (1, reps)),
            jnp.tile(jnp.concatenate([-sin, sin], axis=1), (1, reps)))


def _compress_weights(pe, w1, w2):
    ng = NSA_KV_HEADS
    gkv = ng * HEAD_DIM
    eye = jnp.eye(ng, dtype=F32)
    w1_bd = jnp.einsum('jde,gh->jgdhe', w1, eye).reshape(CMP_LEN, gkv, gkv).astype(BF16)
    return jnp.tile(pe, (1, ng)), w1_bd, jnp.kron(eye, w2).astype(BF16)


def _pad_cols(w, cols):
    return jnp.pad(w, ((0, 0), (0, cols - w.shape[1])))


def _transposed_pages(cache):
    l, pool, pg, nh, hd = cache.shape
    return jnp.transpose(cache, (0, 1, 3, 4, 2)).reshape(l, pool, nh * hd, pg)


TM_PROMPT_FFN = 1024
TM_PROMPT = 512


def kernel(x_prompt, x_sample, cache_fox_k, cache_fox_v, cache_fox_logf, cache_nsa_cmp_k, cache_nsa_cmp_v, cache_nsa_slc_k, cache_nsa_slc_v, state_nsa_win_k, state_nsa_win_v, page_table, norm_pre, norm_post, ffn1_w_in, ffn1_w_out, ffn2_w_in, ffn2_w_out, fox_w_in, fox_b_f, fox_w_out, nsa_w_in, nsa_cmp_pe_k, nsa_cmp_w1_k, nsa_cmp_w2_k, nsa_cmp_pe_v, nsa_cmp_w1_v, nsa_cmp_w2_v, nsa_w_out):
    n, t, d = x_prompt.shape
    b, nq, _ = x_sample.shape
    depth = norm_pre.shape[0]
    assert depth == 2 and fox_w_in.shape[0] == 1 and nsa_w_in.shape[0] == 1
    n_pages = page_table.shape[1]
    past_len = n_pages * PAGE_SIZE
    rs = b * nq
    row = lambda a: a.reshape(1, -1)
    xp = x_prompt.reshape(n * t, d)
    xs = x_sample.reshape(rs, d)

    def ffn_pair(xp, xs, layer, slot, w_in, w_out):
        wi, wo = w_in[layer].astype(BF16), w_out[layer].astype(BF16)
        gpre, gpost = row(norm_pre[layer, slot]), row(norm_post[layer, slot])
        return (_ffn(xp, gpre, gpost, wi, wo, tm=TM_PROMPT_FFN), _ffn(xs, gpre, gpost, wi, wo, tm=rs))

    xp, xs = ffn_pair(xp, xs, 0, 0, ffn1_w_in, ffn1_w_out)
    dh = FOX_HEADS * HEAD_DIM
    wqkv = fox_w_in[0][:, :3 * dh].astype(BF16)
    wf = _pad_cols(fox_w_in[0][:, 3 * dh:], LANES).astype(BF16)
    bf = row(fox_b_f[0])
    wout = fox_w_out[0].astype(BF16)
    gpre, gpost = row(norm_pre[0, 1]), row(norm_post[0, 1])

    qp, fox_kt_p, fox_vt_p, fox_lf_p, kb, vb = _fox_proj(xp, gpre, wqkv, wf, bf, tm=TM_PROMPT, seq_len=t)
    ccol, crow = _cumsum(fox_lf_p.reshape(n, t, FOX_HEADS))
    op = _fox_attn(qp.reshape(n, t, dh), kb.reshape(n, t, dh), vb.reshape(n, t, dh),
                   ccol, crow.reshape(n, FOX_HEADS // 2, 2, t))
    xp = _oproj(op.reshape(n * t, dh), xp, wout, gpost, tm=TM_PROMPT)

    qs, fox_k_s, fox_v_s, fox_lf_s = _fox_proj(xs, gpre, wqkv, wf, bf, tm=rs)
    os_ = _fox_decode(page_table, qs.reshape(b, nq, dh), fox_k_s.reshape(b, nq, dh), fox_v_s.reshape(b, nq, dh),
                      fox_lf_s.reshape(b, nq, FOX_HEADS),
                      _transposed_pages(cache_fox_k), _transposed_pages(cache_fox_v),
                      jnp.transpose(cache_fox_logf, (0, 1, 3, 2)), 0)
    xs = _oproj(os_.reshape(rs, dh), xs, wout, gpost, tm=rs)
    xp, xs = ffn_pair(xp, xs, 0, 2, ffn2_w_in, ffn2_w_out)

    xp, xs = ffn_pair(xp, xs, 1, 0, ffn1_w_in, ffn1_w_out)
    dq = NSA_HEADS * HEAD_DIM
    gkv = NSA_KV_HEADS * HEAD_DIM
    w = nsa_w_in[0]
    cut = lambda k: w[:, dq + k * gkv:dq + (k + 1) * gkv]
    wr = jnp.concatenate([w[:, :dq], cut(0), cut(2), cut(4)], axis=1).astype(BF16)
    wp = jnp.concatenate([cut(1), cut(3), cut(5)], axis=1).astype(BF16)
    wg = _pad_cols(w[:, dq + 6 * gkv:], LANES).astype(BF16)
    wout = nsa_w_out[0].astype(BF16)
    wk = _compress_weights(nsa_cmp_pe_k[0], nsa_cmp_w1_k[0], nsa_cmp_w2_k[0])
    wv = _compress_weights(nsa_cmp_pe_v[0], nsa_cmp_w1_v[0], nsa_cmp_w2_v[0])
    gpre, gpost = row(norm_pre[1, 1]), row(norm_post[1, 1])

    cos_p, sin_p = _rope_tables(jnp.arange(t, dtype=jnp.int32))
    q, gates, *kv = _nsa_proj(xp, gpre, wr, wp, wg, cos_p, sin_p, tm=TM_PROMPT, seq_len=t)
    nsa_t_p = kv[:6]
    kc, vc, ksb, vsb, kwb, vwb = kv[6:]
    seq = lambda a: a.reshape(n, t, a.shape[1])
    kcmp = _compress(seq(kc), *wk)
    vcmp = _compress(seq(vc), *wv)
    op = _nsa_attn(seq(q), kcmp, vcmp, seq(ksb), seq(vsb), seq(kwb), seq(vwb), seq(gates))
    xp = _oproj(op.reshape(n * t, dq), xp, wout, gpost, tm=TM_PROMPT)

    cos_s, sin_s = _rope_tables(past_len + jnp.arange(nq, dtype=jnp.int32))
    cos_s, sin_s = jnp.tile(cos_s, (b, 1)), jnp.tile(sin_s, (b, 1))
    q, gates, kc, vc, ks, vs, kw, vw = _nsa_proj(xs, gpre, wr, wp, wg, cos_s, sin_s, tm=rs)
    tok = lambda a: a.reshape(b, nq, a.shape[1])
    n_sel = -(-(past_len + nq) // SLC_BLOCK)
    nsp = -(-n_sel // LANES) * LANES
    o_cmp, sel = _nsa_decode_cmp(page_table, tok(q), _transposed_pages(cache_nsa_cmp_k),
                                 _transposed_pages(cache_nsa_cmp_v), wk, wv, 0, nsp=nsp)
    o_slc = _nsa_decode_slc(page_table, tok(q), tok(ks), tok(vs), sel,
                            _transposed_pages(cache_nsa_slc_k), _transposed_pages(cache_nsa_slc_v), 0)
    nbuf = state_nsa_win_k.shape[2]
    os_, win_k_s, win_v_s = _nsa_decode_win(tok(q), tok(kw), tok(vw), state_nsa_win_k.reshape(b, nbuf, gkv),
                                            state_nsa_win_v.reshape(b, nbuf, gkv), o_cmp, o_slc, tok(gates))
    xs = _oproj(os_.reshape(rs, dq), xs, wout, gpost, tm=rs)
    nsa_s = (kc, vc, ks, vs)
    xp, xs = ffn_pair(xp, xs, 1, 2, ffn2_w_in, ffn2_w_out)

    keep_p = min(WINDOW, t)
    fox_shape = lambda a, m, r: a.reshape(1, m, r, FOX_HEADS, HEAD_DIM)
    nsa_shape = lambda a, m, r: a.reshape(1, m, r, NSA_KV_HEADS, HEAD_DIM)

    def token_major(a, heads):
        return jnp.transpose(a.reshape(1, n, heads, HEAD_DIM, a.shape[2]), (0, 1, 4, 2, 3))

    win_p = lambda a: token_major(a[:, :, t - keep_p:], NSA_KV_HEADS)
    return (xp.reshape(n, t, d), xs.reshape(b, nq, d),
            token_major(fox_kt_p, FOX_HEADS), token_major(fox_vt_p, FOX_HEADS), fox_lf_p.reshape(1, n, t, FOX_HEADS),
            fox_shape(fox_k_s, b, nq), fox_shape(fox_v_s, b, nq), fox_lf_s.reshape(1, b, nq, FOX_HEADS),
            token_major(nsa_t_p[0], NSA_KV_HEADS), token_major(nsa_t_p[1], NSA_KV_HEADS),
            token_major(nsa_t_p[2], NSA_KV_HEADS), token_major(nsa_t_p[3], NSA_KV_HEADS),
            win_p(nsa_t_p[4]), win_p(nsa_t_p[5]),
            nsa_shape(nsa_s[0], b, nq), nsa_shape(nsa_s[1], b, nq), nsa_shape(nsa_s[2], b, nq), nsa_shape(nsa_s[3], b, nq),
            nsa_shape(win_k_s, b, nbuf), nsa_shape(win_v_s, b, nbuf))
```

```python
import functools

import jax
import jax.numpy as jnp
from jax import lax
from jax.experimental import pallas as pl
from jax.experimental.pallas import tpu as pltpu

F32 = jnp.float32
BF16 = jnp.bfloat16

HEAD_DIM = 64
FOX_HEADS = 16
NSA_HEADS = 16
NSA_KV_HEADS = 4
NSA_GROUP = NSA_HEADS // NSA_KV_HEADS
CMP_LEN = 32
CMP_STRIDE = 16
SLC_BLOCK = 64
SLC_TOP = 16
N_LOCAL_BLOCKS = 2
FORCE_SCORE = 1e4
WINDOW = 512
ROPE_THETA = 10000.0
NORM_EPS = 1e-6
NEG_INF = -1e30
MASKED = 2 * NEG_INF
SCALE = HEAD_DIM ** -0.5
LOG2E = 1.4426950408889634
SCALE2 = SCALE * LOG2E
PAGE_SIZE = 128

LANES = 128
SUBLANES = 8
VMEM_LIMIT = 56 * 1024 * 1024

PAGES_PER_STEP = 16
NSA_PAGES_PER_STEP = 16
CHUNKS_PER_PAGE = PAGE_SIZE // CMP_STRIDE
ROW_BLOCK = 128


def _params(*sem):
    return pltpu.CompilerParams(dimension_semantics=sem, vmem_limit_bytes=VMEM_LIMIT)


def _dot(a, b):
    return jnp.dot(a, b, preferred_element_type=F32)


def _dot_nt(a, b):
    return lax.dot_general(a, b, (((1,), (1,)), ((), ())), preferred_element_type=F32)


def _split3(x):
    hi = x.astype(BF16)
    r1 = x - hi.astype(F32)
    mid = r1.astype(BF16)
    lo = (r1 - mid.astype(F32)).astype(BF16)
    return hi, mid, lo


def _dot3_l(a, b):
    hi, mid, lo = _split3(a)
    return _dot(hi, b) + _dot(mid, b) + _dot(lo, b)


def _dot3_r(a, b):
    hi, mid, lo = _split3(b)
    return _dot(a, hi) + _dot(a, mid) + _dot(a, lo)


def _dot3_nt_r(a, b):
    hi, mid, lo = _split3(b)
    return _dot_nt(a, hi) + _dot_nt(a, mid) + _dot_nt(a, lo)


def _rms(x, g):
    return x * lax.rsqrt(jnp.mean(x * x, axis=-1, keepdims=True) + NORM_EPS) * g


def _silu(x):
    return x * jax.nn.sigmoid(x)


def _iota(shape, dim):
    return lax.broadcasted_iota(jnp.int32, shape, dim)


def _masked_softmax2(s, mask):
    s = jnp.where(mask, s, NEG_INF)
    m = jnp.max(s, axis=-1, keepdims=True)
    p = jnp.where(mask, jnp.exp2(s - m), 0.0)
    return p / jnp.maximum(jnp.sum(p, axis=-1, keepdims=True), 1e-30)


def _init_stats(m_ref, l_ref, acc_ref):
    m_ref[...] = jnp.full(m_ref.shape, NEG_INF, F32)
    l_ref[...] = jnp.zeros(l_ref.shape, F32)
    acc_ref[...] = jnp.zeros(acc_ref.shape, F32)


def _softmax_step(s, m_ref, l_ref, p_dtype=BF16):
    r, k = s.shape
    ps, alphas = [], []
    for r0 in range(0, r, ROW_BLOCK):
        r1 = min(r0 + ROW_BLOCK, r)
        cols = [s[r0:r1, c:c + LANES] for c in range(0, k, LANES)]
        cmax = cols[0]
        for col in cols[1:]:
            cmax = jnp.maximum(cmax, col)
        m_old = m_ref[r0:r1, :]
        m_new = jnp.maximum(m_old, jnp.max(cmax, axis=1, keepdims=True))
        alpha = jnp.exp2(m_old - m_new)
        pcs = [jnp.exp2(col - m_new) for col in cols]
        lsum = pcs[0]
        for pc in pcs[1:]:
            lsum = lsum + pc
        l_ref[r0:r1, :] = alpha * l_ref[r0:r1, :] + lsum
        m_ref[r0:r1, :] = m_new
        ps.append(jnp.concatenate([pc.astype(p_dtype) for pc in pcs], axis=1))
        alphas.append(alpha)
    return jnp.concatenate(ps, axis=0), jnp.concatenate(alphas, axis=0)


def _normalize(acc, l_ref):
    return acc / jnp.maximum(jnp.sum(l_ref[...], axis=1, keepdims=True), 1e-30)


def _top_mask(score, n_real, n_top, axis):
    blk = _iota(score.shape, axis)
    rank = jnp.zeros(score.shape, F32)
    for i in range(n_real):
        one = score[:, i:i + 1] if axis == 1 else score[i:i + 1, :]
        tie = jnp.where(blk > i, 1.0, 0.0)
        rank = rank + jnp.where(one > score, 1.0, jnp.where(one == score, tie, 0.0))
    return jnp.where((rank < n_top) & (blk < n_real), 1.0, 0.0)


def _cover_matrix(shape, cmp_axis, n_cmp, n_sel):
    ci = _iota(shape, cmp_axis)
    sj = _iota(shape, 1 - cmp_axis)
    hit = ((ci * CMP_STRIDE < (sj + 1) * SLC_BLOCK) & (ci * CMP_STRIDE + CMP_LEN > sj * SLC_BLOCK)
           & (ci < n_cmp) & (sj < n_sel))
    return jnp.where(hit, 1.0, 0.0).astype(BF16)


def _selection_scores(imp, pos, n_sel, axis):
    sj = _iota(imp.shape, axis)
    blk_t = pos // SLC_BLOCK
    valid = (sj <= blk_t) & (sj < n_sel)
    forced = (sj == 0) | (blk_t - sj < N_LOCAL_BLOCKS)
    return jnp.where(valid, jnp.where(forced, FORCE_SCORE, imp), -jnp.inf)


def _pad_rows(x, rows):
    rid = _iota((rows, x.shape[1]), 0)
    out = jnp.zeros((rows, x.shape[1]), x.dtype)
    for r in range(x.shape[0]):
        out = jnp.where(rid == r, jnp.broadcast_to(x[r:r + 1], out.shape), out)
    return out


def _to_half(x, src, dst):
    lane_half = _iota(x.shape, 1) // HEAD_DIM
    if isinstance(src, int) and isinstance(dst, int):
        moved = x if src == dst else pltpu.roll(x, HEAD_DIM, 1)
    else:
        moved = jnp.where(src == dst, x, pltpu.roll(x, HEAD_DIM, 1))
    return jnp.where(lane_half == dst, moved, 0.0)


def _head_block(x, h):
    return x[:, (h // 2) * LANES:(h // 2 + 1) * LANES]


def _assemble_heads(pieces):
    return jnp.concatenate([pieces[2 * e] + pieces[2 * e + 1] for e in range(len(pieces) // 2)], axis=1)


def _ffn_kernel(x_ref, gpre_ref, gpost_ref, wg_ref, wu_ref, wo_ref, o_ref, h_ref, acc_ref):
    f = pl.program_id(1)

    @pl.when(f == 0)
    def _():
        h_ref[...] = _rms(x_ref[...], gpre_ref[...]).astype(BF16)
        acc_ref[...] = jnp.zeros_like(acc_ref)

    h = h_ref[...]
    gate = _dot(h, wg_ref[...])
    up = _dot(h, wu_ref[...])
    acc_ref[...] += _dot((_silu(gate) * up).astype(BF16), wo_ref[...])

    @pl.when(f == pl.num_programs(1) - 1)
    def _():
        o_ref[...] = x_ref[...] + 0.5 * _rms(acc_ref[...], gpost_ref[...])


def _ffn(x, g_pre, g_post, w_in, w_out, *, tm, tf=256):
    rows, d = x.shape
    d_ff = w_out.shape[0]
    nf = d_ff // tf
    assert rows % tm == 0 and d_ff % tf == 0
    return pl.pallas_call(
        _ffn_kernel,
        grid=(rows // tm, nf),
        in_specs=[
            pl.BlockSpec((tm, d), lambda i, f: (i, 0)),
            pl.BlockSpec((1, d), lambda i, f: (0, 0)),
            pl.BlockSpec((1, d), lambda i, f: (0, 0)),
            pl.BlockSpec((d, tf), lambda i, f: (0, f)),
            pl.BlockSpec((d, tf), lambda i, f: (0, nf + f)),
            pl.BlockSpec((tf, d), lambda i, f: (f, 0)),
        ],
        out_specs=pl.BlockSpec((tm, d), lambda i, f: (i, 0)),
        out_shape=jax.ShapeDtypeStruct((rows, d), F32),
        scratch_shapes=[pltpu.VMEM((tm, d), BF16), pltpu.VMEM((tm, d), F32)],
        compiler_params=_params("parallel", "arbitrary"),
        name="ffn",
    )(x, g_pre, g_post, w_in, w_in, w_out)


def _oproj_kernel(o_ref, x_ref, w_ref, g_ref, out_ref):
    y = _dot(o_ref[...].astype(BF16), w_ref[...])
    out_ref[...] = x_ref[...] + _rms(y, g_ref[...])


def _oproj(o, x, w, g, *, tm):
    rows, d = x.shape
    return pl.pallas_call(
        _oproj_kernel,
        grid=(rows // tm,),
        in_specs=[
            pl.BlockSpec((tm, o.shape[1]), lambda i: (i, 0)),
            pl.BlockSpec((tm, d), lambda i: (i, 0)),
            pl.BlockSpec(w.shape, lambda i: (0, 0)),
            pl.BlockSpec((1, d), lambda i: (0, 0)),
        ],
        out_specs=pl.BlockSpec((tm, d), lambda i: (i, 0)),
        out_shape=jax.ShapeDtypeStruct((rows, d), F32),
        compiler_params=_params("parallel"),
        name="oproj",
    )(o, x, w, g)


def _log_sigmoid(z):
    return jnp.minimum(z, 0.0) - jnp.log1p(jnp.exp(-jnp.abs(z)))


def _seq_specs(rows, width, tm, seq_len):
    tiles = seq_len // tm
    return (pl.BlockSpec((1, width, tm), lambda i: (i // tiles, 0, i % tiles)),
            jax.ShapeDtypeStruct((rows // seq_len, width, seq_len), F32))


def _fox_proj_kernel(x_ref, g_ref, wqkv_ref, wf_ref, bf_ref, q_ref, k_ref, v_ref, lf_ref, *copies, channel_major):
    h = _rms(x_ref[...], g_ref[...]).astype(BF16)
    d = q_ref.shape[1]
    qkv = _dot(h, wqkv_ref[...])
    q_ref[...] = qkv[:, :d]
    k, v = qkv[:, d:2 * d], qkv[:, 2 * d:]
    if channel_major:
        k_ref[0] = k.T
        v_ref[0] = v.T
        copies[0][...] = k.astype(BF16)
        copies[1][...] = v.astype(BF16)
    else:
        k_ref[...] = k
        v_ref[...] = v
    fl = _dot(h, wf_ref[...])
    lf_ref[...] = _log_sigmoid(fl[:, :FOX_HEADS] + bf_ref[...])


def _fox_proj(x, g, wqkv, wf, bf, *, tm, seq_len=None):
    rows, d = x.shape
    row_spec = pl.BlockSpec((tm, d), lambda i: (i, 0))
    full = lambda a: pl.BlockSpec(a.shape, lambda i: (0, 0))
    row_sds = jax.ShapeDtypeStruct((rows, d), F32)
    if seq_len is None:
        kv_specs, kv_shapes = [row_spec] * 2, [row_sds] * 2
        extra_specs, extra_shapes = [], []
    else:
        spec, sds = _seq_specs(rows, d, tm, seq_len)
        kv_specs, kv_shapes = [spec] * 2, [sds] * 2
        extra_specs, extra_shapes = [row_spec] * 2, [jax.ShapeDtypeStruct((rows, d), BF16)] * 2
    return pl.pallas_call(
        functools.partial(_fox_proj_kernel, channel_major=seq_len is not None),
        grid=(rows // tm,),
        in_specs=[row_spec, full(g), full(wqkv), full(wf), full(bf)],
        out_specs=[row_spec] + kv_specs + [pl.BlockSpec((tm, FOX_HEADS), lambda i: (i, 0))] + extra_specs,
        out_shape=[row_sds] + kv_shapes + [jax.ShapeDtypeStruct((rows, FOX_HEADS), F32)] + extra_shapes,
        compiler_params=_params("parallel"),
        name="fox_proj",
    )(x, g, wqkv, wf, bf)


def _rope_block(x, cos, sin_signed):
    lane = _iota(x.shape, 1)
    first_half = (lane % HEAD_DIM) < (HEAD_DIM // 2)
    rot = jnp.where(first_half, pltpu.roll(x, LANES - HEAD_DIM // 2, 1), pltpu.roll(x, HEAD_DIM // 2, 1))
    return x * cos + rot * sin_signed


def _nsa_proj_kernel(x_ref, g_ref, wr_ref, wp_ref, wg_ref, cos_ref, sin_ref, q_ref, gate_ref, *kv_refs, channel_major):
    h = _rms(x_ref[...], g_ref[...]).astype(BF16)
    cos = cos_ref[...]
    sin = sin_ref[...]
    r = _dot(h, wr_ref[...])
    nq = q_ref.shape[1]
    nkv = NSA_KV_HEADS * HEAD_DIM
    roped = [_rope_block(r[:, b * LANES:(b + 1) * LANES], cos, sin) for b in range(r.shape[1] // LANES)]
    nqb, nkb = nq // LANES, nkv // LANES
    q_ref[...] = jnp.concatenate(roped[:nqb], axis=1)
    kc, ks, kw = (jnp.concatenate(roped[nqb + e * nkb:nqb + (e + 1) * nkb], axis=1) for e in range(3))
    p = _dot(h, wp_ref[...])
    vc, vs, vw = (p[:, e * nkv:(e + 1) * nkv] for e in range(3))
    kv = (kc, vc, ks, vs, kw, vw)
    if channel_major:
        for ref, a in zip(kv_refs[:6], kv):
            ref[0] = a.T
        kv_refs[6][...] = kc
        kv_refs[7][...] = vc
        for ref, a in zip(kv_refs[8:], kv[2:]):
            ref[...] = a.astype(BF16)
    else:
        for ref, a in zip(kv_refs, kv):
            ref[...] = a
    gl = _dot(h, wg_ref[...])
    gate_ref[...] = jax.nn.sigmoid(gl[:, :gate_ref.shape[1]])


def _nsa_proj(x, g, wr, wp, wg, cos, sin, *, tm, seq_len=None):
    rows, d = x.shape
    nq = NSA_HEADS * HEAD_DIM
    nkv = NSA_KV_HEADS * HEAD_DIM
    ngate = 3 * NSA_HEADS
    n_tab = cos.shape[0] // tm
    row = lambda c: pl.BlockSpec((tm, c), lambda i: (i, 0))
    full = lambda a: pl.BlockSpec(a.shape, lambda i: (0, 0))
    tab = pl.BlockSpec((tm, LANES), lambda i: (i % n_tab, 0))
    sds = lambda c, dt=F32: jax.ShapeDtypeStruct((rows, c), dt)
    if seq_len is None:
        kv_specs, kv_shapes = [row(nkv)] * 6, [sds(nkv)] * 6
    else:
        spec, shape = _seq_specs(rows, nkv, tm, seq_len)
        kv_specs = [spec] * 6 + [row(nkv)] * 6
        kv_shapes = [shape] * 6 + [sds(nkv)] * 2 + [sds(nkv, BF16)] * 4
    return pl.pallas_call(
        functools.partial(_nsa_proj_kernel, channel_major=seq_len is not None),
        grid=(rows // tm,),
        in_specs=[row(d), full(g), full(wr), full(wp), full(wg), tab, tab],
        out_specs=[row(nq), row(ngate)] + kv_specs,
        out_shape=[sds(nq), sds(ngate)] + kv_shapes,
        compiler_params=_params("parallel"),
        name="nsa_proj",
    )(x, g, wr, wp, wg, cos, sin)


def _cumsum_kernel(lf_ref, ccol_ref, kaug_ref, *, blk):
    t = lf_ref.shape[1]
    lower = jnp.where(_iota((blk, blk), 1) <= _iota((blk, blk), 0), 1.0, 0.0).astype(BF16)
    carry = jnp.zeros((1, lf_ref.shape[2]), F32)
    for b in range(t // blk):
        cs = _dot3_r(lower, lf_ref[0, b * blk:(b + 1) * blk, :]) + carry
        ccol_ref[0, b * blk:(b + 1) * blk, :] = cs
        carry = cs[blk - 1:blk, :]
    nh = lf_ref.shape[2]
    width = kaug_ref.shape[2]
    col = _iota((nh, width), 1)
    head = _iota((nh, width), 0)
    target = LANES * (head // 2) + _bias_lane(head % 2, 1)
    pieces = _split3(-LOG2E * ccol_ref[0])
    kaug = jnp.where(_iota((t, width), 1) % LANES % BIAS_LANES_PER_HEAD < 3, 1.0, 0.0)
    kaug = jnp.where(_iota((t, width), 1) % LANES < 2 * BIAS_LANES_PER_HEAD, kaug, 0.0)
    for e, piece in enumerate(pieces):
        kaug = kaug + _dot(piece, jnp.where(col == target + e, 1.0, 0.0).astype(BF16))
    kaug_ref[0] = kaug.astype(BF16)


BIAS_LANES_PER_HEAD = 8


def _bias_lane(head_in_pair, side):
    return BIAS_LANES_PER_HEAD * head_in_pair + 3 * side


def _cumsum(lf, nlanes):
    n, t, h = lf.shape
    return pl.pallas_call(
        functools.partial(_cumsum_kernel, blk=256),
        grid=(n,),
        in_specs=[pl.BlockSpec((1, t, h), lambda i: (i, 0, 0))],
        out_specs=[pl.BlockSpec((1, t, h), lambda i: (i, 0, 0)), pl.BlockSpec((1, t, nlanes), lambda i: (i, 0, 0))],
        out_shape=[jax.ShapeDtypeStruct((n, t, h), F32), jax.ShapeDtypeStruct((n, t, nlanes), BF16)],
        compiler_params=_params("parallel"),
        name="fox_cumsum",
    )(lf)


def _fox_attn_kernel(q_ref, k_ref, v_ref, ccol_ref, kaug_ref, o_ref, m_ref, l_ref, acc_ref, *, tq, tk):
    hp = pl.program_id(1)
    i = pl.program_id(2)
    rows = 2 * tq
    q = q_ref[0] * SCALE2
    lane = _iota((tq, LANES), 1)
    qs = jnp.concatenate([jnp.where(lane < HEAD_DIM, q, 0.0), jnp.where(lane >= HEAD_DIM, q, 0.0)], axis=0)
    nh = ccol_ref.shape[2]
    head = _iota((nh, LANES), 0)
    col = _iota((nh, LANES), 1)
    qaug = []
    for e in range(2):
        a = jnp.where((lane >= _bias_lane(e, 1)) & (lane < _bias_lane(e, 1) + 3), 1.0, 0.0)
        for pi, piece in enumerate(_split3(LOG2E * ccol_ref[0])):
            place = jnp.where((head == 2 * hp + e) & (col == _bias_lane(e, 0) + pi), 1.0, 0.0).astype(BF16)
            a = a + _dot(piece, place)
        qaug.append(a)
    qs = jnp.concatenate([qs, jnp.concatenate(qaug, axis=0)], axis=1).astype(BF16)
    lane_minus_row = _iota((rows, LANES), 1) - _iota((rows, LANES), 0) % tq
    _init_stats(m_ref, l_ref, acc_ref)

    def tile(j, causal):
        k0 = pl.multiple_of(j * tk, tk)
        kt = jnp.concatenate([k_ref[0, pl.ds(k0, tk), :], kaug_ref[0, pl.ds(k0, tk), :]], axis=1)
        vt = v_ref[0, pl.ds(k0, tk), :]
        s = _dot_nt(qs, kt)
        if causal:
            s = jnp.concatenate([jnp.where(lane_minus_row <= i * tq - k0 - c, s[:, c:c + LANES], MASKED)
                                 for c in range(0, tk, LANES)], axis=1)
        p, alpha = _softmax_step(s, m_ref, l_ref)
        acc_ref[...] = alpha * acc_ref[...] + _dot(p, vt)

    def run(lo, hi, causal):
        def body(j, carry):
            tile(j, causal)
            return carry
        lax.fori_loop(lo, hi, body, 0)

    n_below = (i * tq) // tk
    run(0, n_below, False)
    run(n_below, (i * tq + tq + tk - 1) // tk, True)
    o = _normalize(acc_ref[...], l_ref)
    o_ref[0] = jnp.where(lane < HEAD_DIM, o[:tq], o[tq:])


def _fox_attn(q, k, v, ccol, kaug, *, tq=512, tk=512):
    n, t, d = q.shape
    npair = d // LANES
    rows = 2 * tq
    return pl.pallas_call(
        functools.partial(_fox_attn_kernel, tq=tq, tk=tk),
        grid=(n, npair, t // tq),
        in_specs=[
            pl.BlockSpec((1, tq, LANES), lambda b, h, i: (b, i, h)),
            pl.BlockSpec((1, t, LANES), lambda b, h, i: (b, 0, h)),
            pl.BlockSpec((1, t, LANES), lambda b, h, i: (b, 0, h)),
            pl.BlockSpec((1, tq, ccol.shape[2]), lambda b, h, i: (b, i, 0)),
            pl.BlockSpec((1, t, LANES), lambda b, h, i: (b, 0, h)),
        ],
        out_specs=pl.BlockSpec((1, tq, LANES), lambda b, h, i: (b, i, h)),
        out_shape=jax.ShapeDtypeStruct((n, t, d), F32),
        scratch_shapes=[pltpu.VMEM((rows, LANES), F32)] * 3,
        compiler_params=_params("parallel", "parallel", "arbitrary"),
        name="fox_attn",
    )(q, k, v, ccol, kaug)


def _fox_decode_kernel(pt_ref, q_ref, kn_ref, vn_ref, lfn_ref, *rest, nq, rows):
    npg = PAGES_PER_STEP
    k_refs, v_refs, lf_refs = rest[:npg], rest[npg:2 * npg], rest[2 * npg:3 * npg]
    o_ref = rest[3 * npg]
    qbd_ref, kcat_ref, vcat_ref, m_ref, l_ref, acc_ref, carry_ref, ncum_ref = rest[3 * npg + 1:]
    del pt_ref
    jj = pl.program_id(1)
    d = q_ref.shape[2]
    nh = FOX_HEADS
    pg = PAGE_SIZE
    r2 = _iota((rows, d), 0)
    block_diag = (r2 % nh == _iota((rows, d), 1) // HEAD_DIM) & (r2 < nq * nh)
    re = _iota((rows, nh), 0)
    head_of_row = jnp.where((re % nh == _iota((rows, nh), 1)) & (re < nq * nh), 1.0, 0.0).astype(BF16)

    @pl.when(jj == 0)
    def _():
        q = q_ref[0] * SCALE2
        pad = [jnp.zeros((rows - nq * nh, d), F32)] if rows > nq * nh else []
        qb = jnp.concatenate([jnp.broadcast_to(q[t:t + 1], (nh, d)) for t in range(nq)] + pad, axis=0)
        qbd = jnp.where(block_diag, qb, 0.0).astype(BF16)
        qbd_ref[...] = qbd
        kn = _pad_rows(kn_ref[0], pg).astype(BF16)
        vn = _pad_rows(vn_ref[0], pg).astype(BF16)
        sn = _dot_nt(qbd, kn)
        xn = _dot3_nt_r(head_of_row, _pad_rows(lfn_ref[0], pg)) * LOG2E
        t_r = _iota((rows, 1), 0) // nh
        cum = []
        c = jnp.zeros((rows, 1), F32)
        for s in range(nq):
            c = c + xn[:, s:s + 1]
            cum.append(c)
        lane = _iota((rows, pg), 1)
        ncum = cum[-1]
        cpre = jnp.broadcast_to(cum[-1], (rows, pg))
        for s in reversed(range(nq - 1)):
            ncum = jnp.where(t_r == s, cum[s], ncum)
            cpre = jnp.where(lane == s, cum[s], cpre)
        mask = (lane <= t_r) & (lane < nq)
        logit = jnp.where(mask, sn + ncum - cpre, MASKED)
        m = jnp.maximum(jnp.max(logit, axis=-1, keepdims=True), NEG_INF)
        p = jnp.exp2(logit - m)
        m_ref[...] = jnp.broadcast_to(m, m_ref.shape)
        l_ref[...] = p
        acc_ref[...] = _dot(p.astype(BF16), vn)
        carry_ref[...] = jnp.zeros(carry_ref.shape, F32)
        ncum_ref[...] = jnp.broadcast_to(ncum, ncum_ref.shape)

    for p in range(npg):
        kcat_ref[:, p * pg:(p + 1) * pg] = k_refs[p][...].astype(BF16)
        vcat_ref[:, p * pg:(p + 1) * pg] = v_refs[p][...].astype(BF16)
    s = _dot(qbd_ref[...], kcat_ref[...])
    lfcat = jnp.concatenate([lf_refs[p][...] for p in range(npg)], axis=1)
    x = _dot3_r(head_of_row, lfcat) * LOG2E
    y = jnp.concatenate([x[:, p * pg:(p + 1) * pg] for p in range(npg)], axis=0)
    ri, ci = _iota((pg, 2 * pg), 0), _iota((pg, 2 * pg), 1)
    later_and_total = jnp.where((ri >= ci) | (ci >= pg), 1.0, 0.0).astype(BF16)
    z = _dot3_l(y, later_and_total)
    c = carry_ref[...]
    ncum = ncum_ref[...]
    cols = [None] * npg
    for p in reversed(range(npg)):
        zp = z[p * rows:(p + 1) * rows]
        cols[p] = s[:, p * pg:(p + 1) * pg] + (zp[:, :pg] - y[p * rows:(p + 1) * rows] + c + ncum)
        c = c + zp[:, pg:]
    carry_ref[...] = c
    p, alpha = _softmax_step(jnp.concatenate(cols, axis=1), m_ref, l_ref)
    pv = _dot_nt(p, vcat_ref[...])
    for b in range(0, d, LANES):
        acc_ref[:, b:b + LANES] = alpha * acc_ref[:, b:b + LANES] + pv[:, b:b + LANES]

    @pl.when(jj == pl.num_programs(1) - 1)
    def _():
        o = jnp.where(block_diag, _normalize(acc_ref[...], l_ref), 0.0)
        for t in range(nq):
            o_ref[0, t:t + 1, :] = jnp.sum(o[t * nh:(t + 1) * nh], axis=0, keepdims=True)


def _fox_decode(page_table, q, k_new, v_new, lf_new, cache_kt, cache_vt, cache_lft, layer):
    b, nq, d = q.shape
    n_pages = page_table.shape[1]
    npg = PAGES_PER_STEP
    assert n_pages % npg == 0
    rows = -(-nq * FOX_HEADS // (2 * SUBLANES)) * (2 * SUBLANES)

    def page_spec(height, p):
        return pl.BlockSpec((None, None, height, PAGE_SIZE),
                            lambda i, jj, pt: (layer, pt[i, n_pages - npg * (jj + 1) + p], 0, 0))

    tok = lambda w: pl.BlockSpec((1, nq, w), lambda i, jj, pt: (i, 0, 0))
    grid_spec = pltpu.PrefetchScalarGridSpec(
        num_scalar_prefetch=1,
        grid=(b, n_pages // npg),
        in_specs=[tok(d), tok(d), tok(d), tok(FOX_HEADS)]
        + [page_spec(d, p) for p in range(npg)] * 2 + [page_spec(FOX_HEADS, p) for p in range(npg)],
        out_specs=pl.BlockSpec((1, nq, d), lambda i, jj, pt: (i, 0, 0)),
        scratch_shapes=[
            pltpu.VMEM((rows, d), BF16),
            pltpu.VMEM((d, npg * PAGE_SIZE), BF16),
            pltpu.VMEM((d, npg * PAGE_SIZE), BF16),
            pltpu.VMEM((rows, LANES), F32),
            pltpu.VMEM((rows, LANES), F32),
            pltpu.VMEM((rows, d), F32),
            pltpu.VMEM((rows, LANES), F32),
            pltpu.VMEM((rows, LANES), F32),
        ],
    )
    return pl.pallas_call(
        functools.partial(_fox_decode_kernel, nq=nq, rows=rows),
        grid_spec=grid_spec,
        out_shape=jax.ShapeDtypeStruct((b, nq, d), F32),
        compiler_params=_params("parallel", "arbitrary"),
        name="fox_decode",
    )(page_table, q, k_new, v_new, lf_new, *([cache_kt] * npg), *([cache_vt] * npg), *([cache_lft] * npg))


def _compress_finish(a, b, w2):
    n = a.shape[0]
    hid = _silu(a + pltpu.roll(b, n - 1, 0))
    out = _dot(hid.astype(BF16), w2)
    return jnp.where(_iota(out.shape, 0) < n - 1, out, 0.0)


def _compress_rows(read_rows, n_chunk, pe_ref, w1_ref, w2_ref):
    half = CMP_STRIDE
    a = jnp.zeros((n_chunk, w2_ref.shape[1]), F32)
    b = jnp.zeros((n_chunk, w2_ref.shape[1]), F32)
    for j in range(half):
        xj = read_rows(j)
        a = a + _dot((xj + pe_ref[j:j + 1, :]).astype(BF16), w1_ref[j])
        b = b + _dot((xj + pe_ref[half + j:half + j + 1, :]).astype(BF16), w1_ref[half + j])
    return _compress_finish(a, b, w2_ref[...])


def _compress_kernel(*refs, nblk):
    x_refs = refs[:nblk]
    pe_ref, w1_ref, w2_ref, o_ref = refs[nblk:]
    n_chunk = o_ref.shape[1]
    read = lambda j: jnp.concatenate([r[0, pl.ds(j, n_chunk, stride=CMP_STRIDE), :] for r in x_refs], axis=1)
    o_ref[0] = _compress_rows(read, n_chunk, pe_ref, w1_ref, w2_ref)


def _compress(x, pe, w1, w2):
    n, t, gkv = x.shape
    assert CMP_LEN == 2 * CMP_STRIDE and t % CMP_STRIDE == 0
    c = t // CMP_STRIDE
    nblk = gkv // LANES
    full = lambda a: pl.BlockSpec(a.shape, lambda i: (0,) * a.ndim)
    lane_block = lambda e: pl.BlockSpec((1, t, LANES), lambda i: (i, 0, e))
    return pl.pallas_call(
        functools.partial(_compress_kernel, nblk=nblk),
        grid=(n,),
        in_specs=[lane_block(e) for e in range(nblk)] + [full(pe), full(w1), full(w2)],
        out_specs=pl.BlockSpec((1, c, gkv), lambda i: (i, 0, 0)),
        out_shape=jax.ShapeDtypeStruct((n, c, gkv), F32),
        compiler_params=_params("parallel"),
        name="nsa_compress",
    )(*([x] * nblk), pe, w1, w2)


def _nsa_attn_kernel(q_ref, kc_ref, vc_ref, ks_ref, vs_ref, kw_ref, vw_ref, gate_ref, blockmask_ref, o_ref,
                     m_ref, l_ref, acc_ref, *, tq, tk, n_cmp, n_sel):
    g = pl.program_id(1)
    i = pl.program_id(2)
    par = g % 2
    hp = NSA_GROUP
    rows = hp * tq
    q = q_ref[0] * SCALE2
    qs = jnp.concatenate([_to_half(_head_block(q, j), j % 2, par) for j in range(hp)], axis=0).astype(BF16)
    qpos = i * tq + _iota((rows, 1), 0) % tq
    row_minus_lane = _iota((rows, LANES), 0) % tq - _iota((rows, LANES), 1)

    ncp = kc_ref.shape[1]
    ci = _iota((1, ncp), 1)
    mask = (ci * CMP_STRIDE + CMP_LEN - 1 <= qpos) & (ci < n_cmp)
    p = _masked_softmax2(_dot_nt(qs, kc_ref[0].astype(BF16)), mask)
    o_cmp = _dot(p.astype(BF16), vc_ref[0].astype(BF16))
    psum = p[0:tq]
    for j in range(1, hp):
        psum = psum + p[j * tq:(j + 1) * tq]
    nsr = -(-n_sel // SUBLANES) * SUBLANES
    imp = _dot3_nt_r(_cover_matrix((nsr, ncp), 1, n_cmp, n_sel), psum)
    tpos = i * tq + _iota((1, tq), 1)
    sel_t = _top_mask(_selection_scores(imp, tpos, n_sel, 0), n_sel, min(SLC_TOP, n_sel), 0)
    unsel = jnp.concatenate([1.0 - sel_t, jnp.zeros((LANES - nsr, tq), F32)], axis=0).T.astype(BF16)
    qs_slc = jnp.concatenate([qs, jnp.concatenate([unsel] * hp, axis=0)], axis=1)

    def attend(lhs, kt, vt, transform):
        p, alpha = _softmax_step(transform(_dot_nt(lhs, kt)), m_ref, l_ref)
        acc_ref[...] = alpha * acc_ref[...] + _dot(p, vt)

    def loop(lo, hi, fn):
        def body(j, carry):
            fn(j)
            return carry
        lax.fori_loop(lo, hi, body, 0)

    def slc_tile(j, causal):
        k0 = pl.multiple_of(j * tk, tk)

        def transform(s):
            if not causal:
                return s
            cols = [jnp.where(row_minus_lane >= k0 + c - i * tq, s[:, c:c + LANES], MASKED)
                    for c in range(0, tk, LANES)]
            return jnp.concatenate(cols, axis=1)

        kt = jnp.concatenate([ks_ref[0, pl.ds(k0, tk), :], blockmask_ref[pl.ds(k0, tk), :]], axis=1)
        attend(qs_slc, kt, vs_ref[0, pl.ds(k0, tk), :], transform)

    _init_stats(m_ref, l_ref, acc_ref)
    n_below = (i * tq) // tk
    loop(0, n_below, lambda j: slc_tile(j, False))
    loop(n_below, (i * tq + tq + tk - 1) // tk, lambda j: slc_tile(j, True))
    o_slc = _normalize(acc_ref[...], l_ref)

    win_w = WINDOW + tq
    k0w = pl.multiple_of(jnp.maximum(i * tq - WINDOW, 0), tq)

    def win_transform(s):
        base = i * tq - k0w
        cols = []
        for c in range(0, win_w, LANES):
            sc = jnp.where(row_minus_lane >= c - base, s[:, c:c + LANES], MASKED)
            cols.append(jnp.where(row_minus_lane < WINDOW + c - base, sc, MASKED))
        return jnp.concatenate(cols, axis=1)

    _init_stats(m_ref, l_ref, acc_ref)
    attend(qs, kw_ref[0, pl.ds(k0w, win_w), :], vw_ref[0, pl.ds(k0w, win_w), :], win_transform)
    o_win = _normalize(acc_ref[...], l_ref)

    gates = gate_ref[0]
    gl = _iota(gates.shape, 1)
    outs = []
    for j in range(hp):
        head = g * hp + j
        gc = [jnp.sum(jnp.where(gl == c * NSA_HEADS + head, gates, 0.0), axis=1, keepdims=True) for c in range(3)]
        sl = slice(j * tq, (j + 1) * tq)
        outs.append(_to_half(gc[0] * o_cmp[sl] + gc[1] * o_slc[sl] + gc[2] * o_win[sl], par, j % 2))
    o_ref[0] = jnp.concatenate([outs[2 * e] + outs[2 * e + 1] for e in range(hp // 2)], axis=1)


def _nsa_attn(q, kcmp, vcmp, ks, vs, kw, vw, gates, *, tq=256, tk=512):
    n, t, _ = q.shape
    gw = NSA_GROUP * HEAD_DIM
    ncp = kcmp.shape[1]
    n_cmp = t // CMP_STRIDE - CMP_LEN // CMP_STRIDE + 1
    n_sel = -(-t // SLC_BLOCK)
    assert n_sel <= LANES and t % tq == 0 and t % tk == 0 and WINDOW % tq == 0 and t >= WINDOW + tq
    rows = NSA_GROUP * tq
    blockmask = jnp.where(jnp.arange(t, dtype=jnp.int32)[:, None] // SLC_BLOCK
                          == jnp.arange(LANES, dtype=jnp.int32)[None, :], MASKED, 0.0).astype(BF16)
    pair = lambda r: pl.BlockSpec((1, r, LANES), lambda b, g, i: (b, 0, g // 2))
    return pl.pallas_call(
        functools.partial(_nsa_attn_kernel, tq=tq, tk=tk, n_cmp=n_cmp, n_sel=n_sel),
        grid=(n, NSA_KV_HEADS, t // tq),
        in_specs=[pl.BlockSpec((1, tq, gw), lambda b, g, i: (b, i, g)), pair(ncp), pair(ncp),
                  pair(t), pair(t), pair(t), pair(t),
                  pl.BlockSpec((1, tq, gates.shape[2]), lambda b, g, i: (b, i, 0)),
                  pl.BlockSpec(blockmask.shape, lambda b, g, i: (0, 0))],
        out_specs=pl.BlockSpec((1, tq, gw), lambda b, g, i: (b, i, g)),
        out_shape=jax.ShapeDtypeStruct(q.shape, F32),
        scratch_shapes=[pltpu.VMEM((rows, LANES), F32)] * 3,
        compiler_params=_params("parallel", "parallel", "arbitrary"),
        name="nsa_attn",
    )(q, kcmp, vcmp, ks, vs, kw, vw, gates, blockmask)


def _group_queries(q8, g, par):
    return jnp.concatenate([_to_half(_head_block(q8, h), h % 2, par)
                            for h in range(g * NSA_GROUP, (g + 1) * NSA_GROUP)], axis=0).astype(BF16)


def _nsa_decode_cmp_kernel(pt_ref, q_ref, *rest, past_len, n_cmp, n_sel):
    npg = NSA_PAGES_PER_STEP
    kpages, vpages = rest[:npg], rest[npg:2 * npg]
    pek_ref, w1k_ref, w2k_ref, pev_ref, w1v_ref, w2v_ref = rest[2 * npg:2 * npg + 6]
    o_ref, sel_ref = rest[2 * npg + 6:2 * npg + 8]
    xk_ref, xv_ref = rest[2 * npg + 8:]
    del pt_ref
    jj = pl.program_id(1)
    pg = PAGE_SIZE
    r0 = jj * (npg * pg)
    nblk = xk_ref.shape[0]
    for p in range(npg):
        dst = pl.ds(pl.multiple_of(r0 + p * pg, pg), pg)
        xk = kpages[p][...].T
        xv = vpages[p][...].T
        for e in range(nblk):
            xk_ref[e, dst, :] = xk[:, e * LANES:(e + 1) * LANES]
            xv_ref[e, dst, :] = xv[:, e * LANES:(e + 1) * LANES]

    @pl.when(jj == pl.num_programs(1) - 1)
    def _():
        n_chunk = xk_ref.shape[1] // CMP_STRIDE

        def compress(x_ref, pe_ref, w1_ref, w2_ref):
            read = lambda j: jnp.concatenate(
                [x_ref[e, pl.ds(j, n_chunk, stride=CMP_STRIDE), :] for e in range(nblk)], axis=1)
            return _compress_rows(read, n_chunk, pe_ref, w1_ref, w2_ref)

        kcmp = compress(xk_ref, pek_ref, w1k_ref, w2k_ref)
        vcmp = compress(xv_ref, pev_ref, w1v_ref, w2v_ref)
        ncp = kcmp.shape[0]
        q8 = _pad_rows(q_ref[0] * SCALE2, SUBLANES)
        ci = _iota((1, ncp), 1)
        cover = _cover_matrix((ncp, sel_ref.shape[3]), 0, n_cmp, n_sel)
        pos8 = past_len + _iota((SUBLANES, 1), 0)
        pieces = [None] * NSA_HEADS
        for g in range(NSA_KV_HEADS):
            par = g % 2
            lanes = slice((g // 2) * LANES, (g // 2 + 1) * LANES)
            qs = _group_queries(q8, g, par)
            pos = past_len + _iota((qs.shape[0], 1), 0) % SUBLANES
            mask = (ci * CMP_STRIDE + CMP_LEN - 1 <= pos) & (ci < n_cmp)
            p = _masked_softmax2(_dot_nt(qs, kcmp[:, lanes].astype(BF16)), mask)
            o = _dot(p.astype(BF16), vcmp[:, lanes].astype(BF16))
            psum = p[0:SUBLANES]
            for j in range(1, NSA_GROUP):
                psum = psum + p[j * SUBLANES:(j + 1) * SUBLANES]
            imp = _dot3_l(psum, cover)
            sel_ref[0, g] = _top_mask(_selection_scores(imp, pos8, n_sel, 1), n_sel, min(SLC_TOP, n_sel), 1)
            for j in range(NSA_GROUP):
                h = g * NSA_GROUP + j
                pieces[h] = _to_half(o[j * SUBLANES:(j + 1) * SUBLANES], par, h % 2)
        o_ref[0] = _assemble_heads(pieces)


def _nsa_decode_cmp(page_table, q, cache_kt, cache_vt, wk, wv, layer, *, nsp):
    b, nq, dq = q.shape
    n_pages = page_table.shape[1]
    npg = NSA_PAGES_PER_STEP
    past_len = n_pages * PAGE_SIZE
    assert n_pages % npg == 0 and nq <= SUBLANES and nq < CMP_STRIDE and CMP_LEN == 2 * CMP_STRIDE
    n_chunk = (past_len + nq) // CMP_STRIDE
    assert n_chunk == n_pages * CHUNKS_PER_PAGE
    n_cmp = n_chunk - CMP_LEN // CMP_STRIDE + 1
    n_sel = -(-(past_len + nq) // SLC_BLOCK)
    assert n_sel <= nsp
    gkv = NSA_KV_HEADS * HEAD_DIM

    def page_spec(p):
        return pl.BlockSpec((None, None, gkv, PAGE_SIZE), lambda i, jj, pt: (layer, pt[i, npg * jj + p], 0, 0))

    full = lambda a: pl.BlockSpec(a.shape, lambda i, jj, pt: (0,) * a.ndim)
    grid_spec = pltpu.PrefetchScalarGridSpec(
        num_scalar_prefetch=1,
        grid=(b, n_pages // npg),
        in_specs=[pl.BlockSpec((1, nq, dq), lambda i, jj, pt: (i, 0, 0))]
        + [page_spec(p) for p in range(npg)] * 2 + [full(a) for a in wk] + [full(a) for a in wv],
        out_specs=[pl.BlockSpec((1, SUBLANES, dq), lambda i, jj, pt: (i, 0, 0)),
                   pl.BlockSpec((1, NSA_KV_HEADS, SUBLANES, nsp), lambda i, jj, pt: (i, 0, 0, 0))],
        scratch_shapes=[pltpu.VMEM((gkv // LANES, past_len, LANES), F32)] * 2,
    )
    return pl.pallas_call(
        functools.partial(_nsa_decode_cmp_kernel, past_len=past_len, n_cmp=n_cmp, n_sel=n_sel),
        grid_spec=grid_spec,
        out_shape=[jax.ShapeDtypeStruct((b, SUBLANES, dq), F32),
                   jax.ShapeDtypeStruct((b, NSA_KV_HEADS, SUBLANES, nsp), F32)],
        compiler_params=_params("parallel", "arbitrary"),
        name="nsa_decode_cmp",
    )(page_table, q, *([cache_kt] * npg), *([cache_vt] * npg), *wk, *wv)


def _nsa_decode_slc_kernel(pt_ref, q_ref, kn_ref, vn_ref, sel_ref, *rest, nq, past_len):
    npg = NSA_PAGES_PER_STEP
    kpages, vpages = rest[:npg], rest[npg:2 * npg]
    o_ref = rest[2 * npg]
    qs_ref, kcat_ref, vcat_ref, m_ref, l_ref, acc_ref = rest[2 * npg + 1:]
    del pt_ref
    jj = pl.program_id(1)
    pg = PAGE_SIZE
    npair = NSA_KV_HEADS // 2
    nsp = sel_ref.shape[3]
    selrows = [jnp.concatenate([sel_ref[0, 2 * gp + gi] for gi in range(2) for _ in range(NSA_GROUP)], axis=0)
               for gp in range(npair)]
    rows = selrows[0].shape[0]

    @pl.when(jj == 0)
    def _():
        q8 = _pad_rows(q_ref[0] * SCALE2, SUBLANES)
        kn = _pad_rows(kn_ref[0], pg).astype(BF16)
        vn = _pad_rows(vn_ref[0], pg).astype(BF16)
        t_r = _iota((rows, 1), 0) % SUBLANES
        lane = _iota((rows, pg), 1)
        for gp in range(npair):
            lanes = slice(gp * LANES, (gp + 1) * LANES)
            qs = jnp.concatenate([_group_queries(q8, 2 * gp + gi, gi) for gi in range(2)], axis=0)
            qs_ref[gp] = qs
            _init_stats(m_ref.at[gp], l_ref.at[gp], acc_ref.at[gp])
            new_block = selrows[gp][:, past_len // SLC_BLOCK:past_len // SLC_BLOCK + 1]
            mask = (lane <= t_r) & (lane < nq) & (new_block > 0.5)
            s = jnp.where(mask, _dot_nt(qs, kn[:, lanes]), MASKED)
            p, alpha = _softmax_step(s, m_ref.at[gp], l_ref.at[gp])
            acc_ref[gp] = alpha * acc_ref[gp] + _dot(p, vn[:, lanes])

    for p in range(npg):
        kcat_ref[:, p * pg:(p + 1) * pg] = kpages[p][...].astype(BF16)
        vcat_ref[:, p * pg:(p + 1) * pg] = vpages[p][...].astype(BF16)
    width = npg * pg
    block_of_key = jj * (width // SLC_BLOCK) + _iota((nsp, width), 1) // SLC_BLOCK
    expand = jnp.where(block_of_key == _iota((nsp, width), 0), 1.0, 0.0).astype(BF16)
    for gp in range(npair):
        rws = slice(gp * LANES, (gp + 1) * LANES)
        selx = _dot(selrows[gp].astype(BF16), expand)
        s = jnp.where(selx > 0.5, _dot(qs_ref[gp], kcat_ref[rws, :]), MASKED)
        p, alpha = _softmax_step(s, m_ref.at[gp], l_ref.at[gp])
        acc_ref[gp] = alpha * acc_ref[gp] + _dot_nt(p, vcat_ref[rws, :])

    @pl.when(jj == pl.num_programs(1) - 1)
    def _():
        pieces = [None] * NSA_HEADS
        for gp in range(npair):
            o = _normalize(acc_ref[gp], l_ref.at[gp])
            for gi in range(2):
                for j in range(NSA_GROUP):
                    h = (2 * gp + gi) * NSA_GROUP + j
                    r0 = (gi * NSA_GROUP + j) * SUBLANES
                    pieces[h] = _to_half(o[r0:r0 + SUBLANES], gi, h % 2)
        o_ref[0] = _assemble_heads(pieces)


def _nsa_decode_slc(page_table, q, k_new, v_new, sel, cache_kt, cache_vt, layer):
    b, nq, dq = q.shape
    gkv = k_new.shape[2]
    n_pages = page_table.shape[1]
    npg = NSA_PAGES_PER_STEP
    past_len = n_pages * PAGE_SIZE
    assert n_pages % npg == 0 and past_len % SLC_BLOCK == 0 and nq <= SLC_BLOCK
    npair = NSA_KV_HEADS // 2
    rows = 2 * NSA_GROUP * SUBLANES

    def page_spec(p):
        return pl.BlockSpec((None, None, gkv, PAGE_SIZE), lambda i, jj, pt: (layer, pt[i, npg * jj + p], 0, 0))

    tok = lambda w: pl.BlockSpec((1, nq, w), lambda i, jj, pt: (i, 0, 0))
    grid_spec = pltpu.PrefetchScalarGridSpec(
        num_scalar_prefetch=1,
        grid=(b, n_pages // npg),
        in_specs=[tok(dq), tok(gkv), tok(gkv),
                  pl.BlockSpec((1,) + sel.shape[1:], lambda i, jj, pt: (i, 0, 0, 0))]
        + [page_spec(p) for p in range(npg)] * 2,
        out_specs=pl.BlockSpec((1, SUBLANES, dq), lambda i, jj, pt: (i, 0, 0)),
        scratch_shapes=[
            pltpu.VMEM((npair, rows, LANES), BF16),
            pltpu.VMEM((gkv, npg * PAGE_SIZE), BF16),
            pltpu.VMEM((gkv, npg * PAGE_SIZE), BF16),
            pltpu.VMEM((npair, rows, LANES), F32),
            pltpu.VMEM((npair, rows, LANES), F32),
            pltpu.VMEM((npair, rows, LANES), F32),
        ],
    )
    return pl.pallas_call(
        functools.partial(_nsa_decode_slc_kernel, nq=nq, past_len=past_len),
        grid_spec=grid_spec,
        out_shape=jax.ShapeDtypeStruct((b, SUBLANES, dq), F32),
        compiler_params=_params("parallel", "arbitrary"),
        name="nsa_decode_slc",
    )(page_table, q, k_new, v_new, sel, *([cache_kt] * npg), *([cache_vt] * npg))


def _nsa_decode_win_kernel(q_ref, kn_ref, vn_ref, sk_ref, sv_ref, ocmp_ref, oslc_ref, gate_ref,
                           o_ref, nk_ref, nv_ref, *, nq):
    nbuf = sk_ref.shape[1]
    pg = PAGE_SIZE
    q8 = _pad_rows(q_ref[0] * SCALE, SUBLANES)
    kn = _pad_rows(kn_ref[0], pg).astype(BF16)
    vn = _pad_rows(vn_ref[0], pg).astype(BF16)
    gates = _pad_rows(gate_ref[0], SUBLANES)
    gl = _iota(gates.shape, 1)
    rows = NSA_GROUP * SUBLANES
    t_r = _iota((rows, 1), 0) % SUBLANES
    dist_s = t_r + nbuf - _iota((1, nbuf), 1)
    mask_s = (dist_s >= 0) & (dist_s < WINDOW)
    lane_n = _iota((1, pg), 1)
    dist_n = t_r - lane_n
    mask_n = (dist_n >= 0) & (dist_n < WINDOW) & (lane_n < nq)
    pieces = [None] * NSA_HEADS
    for g in range(NSA_KV_HEADS):
        par = g % 2
        lanes = slice((g // 2) * LANES, (g // 2 + 1) * LANES)
        qs = _group_queries(q8, g, par)
        ss = jnp.where(mask_s, _dot_nt(qs, sk_ref[0, :, lanes].astype(BF16)), NEG_INF)
        sn = jnp.where(mask_n, _dot_nt(qs, kn[:, lanes]), NEG_INF)
        m = jnp.maximum(jnp.max(ss, axis=-1, keepdims=True), jnp.max(sn, axis=-1, keepdims=True))
        ps = jnp.where(mask_s, jnp.exp(ss - m), 0.0)
        pn = jnp.where(mask_n, jnp.exp(sn - m), 0.0)
        den = jnp.sum(ps, axis=-1, keepdims=True) + jnp.sum(pn, axis=-1, keepdims=True)
        ow = (_dot(ps.astype(BF16), sv_ref[0, :, lanes].astype(BF16)) + _dot(pn.astype(BF16), vn[:, lanes]))
        ow = ow / jnp.maximum(den, 1e-30)
        for j in range(NSA_GROUP):
            h = g * NSA_GROUP + j
            pieces[h] = _to_half(ow[j * SUBLANES:(j + 1) * SUBLANES], par, h % 2)
    o_win = _assemble_heads(pieces)
    head_of_lane = _iota((SUBLANES, o_win.shape[1]), 1) // HEAD_DIM
    branch_gate = []
    for c in range(3):
        gfull = jnp.zeros(o_win.shape, F32)
        for h in range(NSA_HEADS):
            col = jnp.sum(jnp.where(gl == c * NSA_HEADS + h, gates, 0.0), axis=1, keepdims=True)
            gfull = jnp.where(head_of_lane == h, col, gfull)
        branch_gate.append(gfull)
    o = branch_gate[0] * ocmp_ref[0] + branch_gate[1] * oslc_ref[0] + branch_gate[2] * o_win
    o_ref[0] = o[:nq]
    nk_ref[0, 0:nbuf - nq, :] = sk_ref[0, nq:nbuf, :]
    nk_ref[0, nbuf - nq:nbuf, :] = kn_ref[0]
    nv_ref[0, 0:nbuf - nq, :] = sv_ref[0, nq:nbuf, :]
    nv_ref[0, nbuf - nq:nbuf, :] = vn_ref[0]


def _nsa_decode_win(q, k_new, v_new, state_k, state_v, o_cmp, o_slc, gates):
    b, nq, dq = q.shape
    nbuf, gkv = state_k.shape[1:]
    assert nbuf == WINDOW and nq <= SUBLANES
    blk = lambda a: pl.BlockSpec((1,) + a.shape[1:], lambda i: (i, 0, 0))
    return pl.pallas_call(
        functools.partial(_nsa_decode_win_kernel, nq=nq),
        grid=(b,),
        in_specs=[blk(a) for a in (q, k_new, v_new, state_k, state_v, o_cmp, o_slc, gates)],
        out_specs=[blk(q), blk(state_k), blk(state_v)],
        out_shape=[jax.ShapeDtypeStruct(q.shape, F32), jax.ShapeDtypeStruct(state_k.shape, F32),
                   jax.ShapeDtypeStruct(state_v.shape, F32)],
        compiler_params=_params("parallel"),
        name="nsa_decode_win",
    )(q, k_new, v_new, state_k, state_v, o_cmp, o_slc, gates)


def _rope_tables(pos):
    half = HEAD_DIM // 2
    inv = ROPE_THETA ** (-jnp.arange(half, dtype=F32) / half)
    ang = pos.astype(F32)[:, None] * inv[None, :]
    cos, sin = jnp.cos(ang), jnp.sin(ang)
    reps = LANES // HEAD_DIM
    return (jnp.tile(jnp.concatenate([cos, cos], axis=1), (1, reps)),
            jnp.tile(jnp.concatenate([-sin, sin], axis=1), (1, reps)))


def _compress_weights(pe, w1, w2):
    ng = NSA_KV_HEADS
    gkv = ng * HEAD_DIM
    eye = jnp.eye(ng, dtype=F32)
    w1_bd = jnp.einsum('jde,gh->jgdhe', w1, eye).reshape(CMP_LEN, gkv, gkv).astype(BF16)
    return jnp.tile(pe, (1, ng)), w1_bd, jnp.kron(eye, w2).astype(BF16)


def _pad_cols(w, cols):
    return jnp.pad(w, ((0, 0), (0, cols - w.shape[1])))


def _transposed_pages(cache):
    l, pool, pg, nh, hd = cache.shape
    return jnp.transpose(cache, (0, 1, 3, 4, 2)).reshape(l, pool, nh * hd, pg)


TM_PROMPT_FFN = 1024
TM_PROMPT = 512


def kernel(x_prompt, x_sample, cache_fox_k, cache_fox_v, cache_fox_logf, cache_nsa_cmp_k, cache_nsa_cmp_v, cache_nsa_slc_k, cache_nsa_slc_v, state_nsa_win_k, state_nsa_win_v, page_table, norm_pre, norm_post, ffn1_w_in, ffn1_w_out, ffn2_w_in, ffn2_w_out, fox_w_in, fox_b_f, fox_w_out, nsa_w_in, nsa_cmp_pe_k, nsa_cmp_w1_k, nsa_cmp_w2_k, nsa_cmp_pe_v, nsa_cmp_w1_v, nsa_cmp_w2_v, nsa_w_out):
    n, t, d = x_prompt.shape
    b, nq, _ = x_sample.shape
    depth = norm_pre.shape[0]
    assert depth == 2 and fox_w_in.shape[0] == 1 and nsa_w_in.shape[0] == 1
    n_pages = page_table.shape[1]
    past_len = n_pages * PAGE_SIZE
    rs = b * nq
    row = lambda a: a.reshape(1, -1)
    xp = x_prompt.reshape(n * t, d)
    xs = x_sample.reshape(rs, d)

    def ffn_pair(xp, xs, layer, slot, w_in, w_out):
        wi, wo = w_in[layer].astype(BF16), w_out[layer].astype(BF16)
        gpre, gpost = row(norm_pre[layer, slot]), row(norm_post[layer, slot])
        return (_ffn(xp, gpre, gpost, wi, wo, tm=TM_PROMPT_FFN), _ffn(xs, gpre, gpost, wi, wo, tm=rs))

    xp, xs = ffn_pair(xp, xs, 0, 0, ffn1_w_in, ffn1_w_out)
    dh = FOX_HEADS * HEAD_DIM
    wqkv = fox_w_in[0][:, :3 * dh].astype(BF16)
    wf = _pad_cols(fox_w_in[0][:, 3 * dh:], LANES).astype(BF16)
    bf = row(fox_b_f[0])
    wout = fox_w_out[0].astype(BF16)
    gpre, gpost = row(norm_pre[0, 1]), row(norm_post[0, 1])

    qp, fox_kt_p, fox_vt_p, fox_lf_p, kb, vb = _fox_proj(xp, gpre, wqkv, wf, bf, tm=TM_PROMPT, seq_len=t)
    ccol, kaug = _cumsum(fox_lf_p.reshape(n, t, FOX_HEADS), dh)
    op = _fox_attn(qp.reshape(n, t, dh), kb.reshape(n, t, dh), vb.reshape(n, t, dh), ccol, kaug)
    xp = _oproj(op.reshape(n * t, dh), xp, wout, gpost, tm=TM_PROMPT)

    qs, fox_k_s, fox_v_s, fox_lf_s = _fox_proj(xs, gpre, wqkv, wf, bf, tm=rs)
    os_ = _fox_decode(page_table, qs.reshape(b, nq, dh), fox_k_s.reshape(b, nq, dh), fox_v_s.reshape(b, nq, dh),
                      fox_lf_s.reshape(b, nq, FOX_HEADS),
                      _transposed_pages(cache_fox_k), _transposed_pages(cache_fox_v),
                      jnp.transpose(cache_fox_logf, (0, 1, 3, 2)), 0)
    xs = _oproj(os_.reshape(rs, dh), xs, wout, gpost, tm=rs)
    xp, xs = ffn_pair(xp, xs, 0, 2, ffn2_w_in, ffn2_w_out)

    xp, xs = ffn_pair(xp, xs, 1, 0, ffn1_w_in, ffn1_w_out)
    dq = NSA_HEADS * HEAD_DIM
    gkv = NSA_KV_HEADS * HEAD_DIM
    w = nsa_w_in[0]
    cut = lambda k: w[:, dq + k * gkv:dq + (k + 1) * gkv]
    wr = jnp.concatenate([w[:, :dq], cut(0), cut(2), cut(4)], axis=1).astype(BF16)
    wp = jnp.concatenate([cut(1), cut(3), cut(5)], axis=1).astype(BF16)
    wg = _pad_cols(w[:, dq + 6 * gkv:], LANES).astype(BF16)
    wout = nsa_w_out[0].astype(BF16)
    wk = _compress_weights(nsa_cmp_pe_k[0], nsa_cmp_w1_k[0], nsa_cmp_w2_k[0])
    wv = _compress_weights(nsa_cmp_pe_v[0], nsa_cmp_w1_v[0], nsa_cmp_w2_v[0])
    gpre, gpost = row(norm_pre[1, 1]), row(norm_post[1, 1])

    cos_p, sin_p = _rope_tables(jnp.arange(t, dtype=jnp.int32))
    q, gates, *kv = _nsa_proj(xp, gpre, wr, wp, wg, cos_p, sin_p, tm=TM_PROMPT, seq_len=t)
    nsa_t_p = kv[:6]
    kc, vc, ksb, vsb, kwb, vwb = kv[6:]
    seq = lambda a: a.reshape(n, t, a.shape[1])
    kcmp = _compress(seq(kc), *wk)
    vcmp = _compress(seq(vc), *wv)
    op = _nsa_attn(seq(q), kcmp, vcmp, seq(ksb), seq(vsb), seq(kwb), seq(vwb), seq(gates))
    xp = _oproj(op.reshape(n * t, dq), xp, wout, gpost, tm=TM_PROMPT)

    cos_s, sin_s = _rope_tables(past_len + jnp.arange(nq, dtype=jnp.int32))
    cos_s, sin_s = jnp.tile(cos_s, (b, 1)), jnp.tile(sin_s, (b, 1))
    q, gates, kc, vc, ks, vs, kw, vw = _nsa_proj(xs, gpre, wr, wp, wg, cos_s, sin_s, tm=rs)
    tok = lambda a: a.reshape(b, nq, a.shape[1])
    n_sel = -(-(past_len + nq) // SLC_BLOCK)
    nsp = -(-n_sel // LANES) * LANES
    o_cmp, sel = _nsa_decode_cmp(page_table, tok(q), _transposed_pages(cache_nsa_cmp_k),
                                 _transposed_pages(cache_nsa_cmp_v), wk, wv, 0, nsp=nsp)
    o_slc = _nsa_decode_slc(page_table, tok(q), tok(ks), tok(vs), sel,
                            _transposed_pages(cache_nsa_slc_k), _transposed_pages(cache_nsa_slc_v), 0)
    nbuf = state_nsa_win_k.shape[2]
    os_, win_k_s, win_v_s = _nsa_decode_win(tok(q), tok(kw), tok(vw), state_nsa_win_k.reshape(b, nbuf, gkv),
                                            state_nsa_win_v.reshape(b, nbuf, gkv), o_cmp, o_slc, tok(gates))
    xs = _oproj(os_.reshape(rs, dq), xs, wout, gpost, tm=rs)
    nsa_s = (kc, vc, ks, vs)
    xp, xs = ffn_pair(xp, xs, 1, 2, ffn2_w_in, ffn2_w_out)

    keep_p = min(WINDOW, t)
    fox_shape = lambda a, m, r: a.reshape(1, m, r, FOX_HEADS, HEAD_DIM)
    nsa_shape = lambda a, m, r: a.reshape(1, m, r, NSA_KV_HEADS, HEAD_DIM)

    def token_major(a, heads):
        return jnp.transpose(a.reshape(1, n, heads, HEAD_DIM, a.shape[2]), (0, 1, 4, 2, 3))

    win_p = lambda a: token_major(a[:, :, t - keep_p:], NSA_KV_HEADS)
    return (xp.reshape(n, t, d), xs.reshape(b, nq, d),
            token_major(fox_kt_p, FOX_HEADS), token_major(fox_vt_p, FOX_HEADS), fox_lf_p.reshape(1, n, t, FOX_HEADS),
            fox_shape(fox_k_s, b, nq), fox_shape(fox_v_s, b, nq), fox_lf_s.reshape(1, b, nq, FOX_HEADS),
            token_major(nsa_t_p[0], NSA_KV_HEADS), token_major(nsa_t_p[1], NSA_KV_HEADS),
            token_major(nsa_t_p[2], NSA_KV_HEADS), token_major(nsa_t_p[3], NSA_KV_HEADS),
            win_p(nsa_t_p[4]), win_p(nsa_t_p[5]),
            nsa_shape(nsa_s[0], b, nq), nsa_shape(nsa_s[1], b, nq), nsa_shape(nsa_s[2], b, nq), nsa_shape(nsa_s[3], b, nq),
            nsa_shape(win_k_s, b, nbuf), nsa_shape(win_v_s, b, nbuf))
```

```python
import functools

import jax
import jax.numpy as jnp
from jax import lax
from jax.experimental import pallas as pl
from jax.experimental.pallas import tpu as pltpu

F32 = jnp.float32
BF16 = jnp.bfloat16

HEAD_DIM = 64
FOX_HEADS = 16
NSA_HEADS = 16
NSA_KV_HEADS = 4
NSA_GROUP = NSA_HEADS // NSA_KV_HEADS
CMP_LEN = 32
CMP_STRIDE = 16
SLC_BLOCK = 64
SLC_TOP = 16
N_LOCAL_BLOCKS = 2
FORCE_SCORE = 1e4
WINDOW = 512
ROPE_THETA = 10000.0
NORM_EPS = 1e-6
NEG_INF = -1e30
MASKED = 2 * NEG_INF
SCALE = HEAD_DIM ** -0.5
LOG2E = 1.4426950408889634
SCALE2 = SCALE * LOG2E
PAGE_SIZE = 128

LANES = 128
SUBLANES = 8
VMEM_LIMIT = 56 * 1024 * 1024

PAGES_PER_STEP = 16
NSA_PAGES_PER_STEP = 16
CHUNKS_PER_PAGE = PAGE_SIZE // CMP_STRIDE
ROW_BLOCK = 128


def _params(*sem):
    return pltpu.CompilerParams(dimension_semantics=sem, vmem_limit_bytes=VMEM_LIMIT)


def _dot(a, b):
    return jnp.dot(a, b, preferred_element_type=F32)


def _dot_nt(a, b):
    return lax.dot_general(a, b, (((1,), (1,)), ((), ())), preferred_element_type=F32)


def _split3(x):
    hi = x.astype(BF16)
    r1 = x - hi.astype(F32)
    mid = r1.astype(BF16)
    lo = (r1 - mid.astype(F32)).astype(BF16)
    return hi, mid, lo


def _dot3_l(a, b):
    hi, mid, lo = _split3(a)
    return _dot(hi, b) + _dot(mid, b) + _dot(lo, b)


def _dot3_r(a, b):
    hi, mid, lo = _split3(b)
    return _dot(a, hi) + _dot(a, mid) + _dot(a, lo)


def _dot3_nt_r(a, b):
    hi, mid, lo = _split3(b)
    return _dot_nt(a, hi) + _dot_nt(a, mid) + _dot_nt(a, lo)


def _rms(x, g):
    return x * lax.rsqrt(jnp.mean(x * x, axis=-1, keepdims=True) + NORM_EPS) * g


def _silu(x):
    return x * jax.nn.sigmoid(x)


def _iota(shape, dim):
    return lax.broadcasted_iota(jnp.int32, shape, dim)


def _masked_softmax2(s, mask):
    s = jnp.where(mask, s, NEG_INF)
    m = jnp.max(s, axis=-1, keepdims=True)
    p = jnp.where(mask, jnp.exp2(s - m), 0.0)
    return p / jnp.maximum(jnp.sum(p, axis=-1, keepdims=True), 1e-30)


def _init_stats(m_ref, l_ref, acc_ref):
    m_ref[...] = jnp.full(m_ref.shape, NEG_INF, F32)
    l_ref[...] = jnp.zeros(l_ref.shape, F32)
    acc_ref[...] = jnp.zeros(acc_ref.shape, F32)


def _softmax_step(s, m_ref, l_ref, p_dtype=BF16):
    r, k = s.shape
    ps, alphas = [], []
    for r0 in range(0, r, ROW_BLOCK):
        r1 = min(r0 + ROW_BLOCK, r)
        cols = [s[r0:r1, c:c + LANES] for c in range(0, k, LANES)]
        cmax = cols[0]
        for col in cols[1:]:
            cmax = jnp.maximum(cmax, col)
        m_old = m_ref[r0:r1, :]
        m_new = jnp.maximum(m_old, jnp.max(cmax, axis=1, keepdims=True))
        alpha = jnp.exp2(m_old - m_new)
        pcs = [jnp.exp2(col - m_new) for col in cols]
        lsum = pcs[0]
        for pc in pcs[1:]:
            lsum = lsum + pc
        l_ref[r0:r1, :] = alpha * l_ref[r0:r1, :] + lsum
        m_ref[r0:r1, :] = m_new
        ps.append(jnp.concatenate([pc.astype(p_dtype) for pc in pcs], axis=1))
        alphas.append(alpha)
    return jnp.concatenate(ps, axis=0), jnp.concatenate(alphas, axis=0)


def _normalize(acc, l_ref):
    return acc / jnp.maximum(jnp.sum(l_ref[...], axis=1, keepdims=True), 1e-30)


def _top_mask(score, n_real, n_top, axis):
    blk = _iota(score.shape, axis)
    rank = jnp.zeros(score.shape, F32)
    for i in range(n_real):
        one = score[:, i:i + 1] if axis == 1 else score[i:i + 1, :]
        tie = jnp.where(blk > i, 1.0, 0.0)
        rank = rank + jnp.where(one > score, 1.0, jnp.where(one == score, tie, 0.0))
    return jnp.where((rank < n_top) & (blk < n_real), 1.0, 0.0)


def _cover_matrix(shape, cmp_axis, n_cmp, n_sel):
    ci = _iota(shape, cmp_axis)
    sj = _iota(shape, 1 - cmp_axis)
    hit = ((ci * CMP_STRIDE < (sj + 1) * SLC_BLOCK) & (ci * CMP_STRIDE + CMP_LEN > sj * SLC_BLOCK)
           & (ci < n_cmp) & (sj < n_sel))
    return jnp.where(hit, 1.0, 0.0).astype(BF16)


def _selection_scores(imp, pos, n_sel, axis):
    sj = _iota(imp.shape, axis)
    blk_t = pos // SLC_BLOCK
    valid = (sj <= blk_t) & (sj < n_sel)
    forced = (sj == 0) | (blk_t - sj < N_LOCAL_BLOCKS)
    return jnp.where(valid, jnp.where(forced, FORCE_SCORE, imp), -jnp.inf)


def _pad_rows(x, rows):
    rid = _iota((rows, x.shape[1]), 0)
    out = jnp.zeros((rows, x.shape[1]), x.dtype)
    for r in range(x.shape[0]):
        out = jnp.where(rid == r, jnp.broadcast_to(x[r:r + 1], out.shape), out)
    return out


def _to_half(x, src, dst):
    lane_half = _iota(x.shape, 1) // HEAD_DIM
    if isinstance(src, int) and isinstance(dst, int):
        moved = x if src == dst else pltpu.roll(x, HEAD_DIM, 1)
    else:
        moved = jnp.where(src == dst, x, pltpu.roll(x, HEAD_DIM, 1))
    return jnp.where(lane_half == dst, moved, 0.0)


def _head_block(x, h):
    return x[:, (h // 2) * LANES:(h // 2 + 1) * LANES]


def _assemble_heads(pieces):
    return jnp.concatenate([pieces[2 * e] + pieces[2 * e + 1] for e in range(len(pieces) // 2)], axis=1)


def _ffn_kernel(x_ref, gpre_ref, gpost_ref, wg_ref, wu_ref, wo_ref, o_ref, h_ref, acc_ref):
    f = pl.program_id(1)

    @pl.when(f == 0)
    def _():
        h_ref[...] = _rms(x_ref[...], gpre_ref[...]).astype(BF16)
        acc_ref[...] = jnp.zeros_like(acc_ref)

    h = h_ref[...]
    gate = _dot(h, wg_ref[...])
    up = _dot(h, wu_ref[...])
    acc_ref[...] += _dot((_silu(gate) * up).astype(BF16), wo_ref[...])

    @pl.when(f == pl.num_programs(1) - 1)
    def _():
        o_ref[...] = x_ref[...] + 0.5 * _rms(acc_ref[...], gpost_ref[...])


def _ffn(x, g_pre, g_post, w_in, w_out, *, tm, tf=256):
    rows, d = x.shape
    d_ff = w_out.shape[0]
    nf = d_ff // tf
    assert rows % tm == 0 and d_ff % tf == 0
    return pl.pallas_call(
        _ffn_kernel,
        grid=(rows // tm, nf),
        in_specs=[
            pl.BlockSpec((tm, d), lambda i, f: (i, 0)),
            pl.BlockSpec((1, d), lambda i, f: (0, 0)),
            pl.BlockSpec((1, d), lambda i, f: (0, 0)),
            pl.BlockSpec((d, tf), lambda i, f: (0, f)),
            pl.BlockSpec((d, tf), lambda i, f: (0, nf + f)),
            pl.BlockSpec((tf, d), lambda i, f: (f, 0)),
        ],
        out_specs=pl.BlockSpec((tm, d), lambda i, f: (i, 0)),
        out_shape=jax.ShapeDtypeStruct((rows, d), F32),
        scratch_shapes=[pltpu.VMEM((tm, d), BF16), pltpu.VMEM((tm, d), F32)],
        compiler_params=_params("parallel", "arbitrary"),
        name="ffn",
    )(x, g_pre, g_post, w_in, w_in, w_out)


def _oproj_kernel(o_ref, x_ref, w_ref, g_ref, out_ref):
    y = _dot(o_ref[...].astype(BF16), w_ref[...])
    out_ref[...] = x_ref[...] + _rms(y, g_ref[...])


def _oproj(o, x, w, g, *, tm):
    rows, d = x.shape
    return pl.pallas_call(
        _oproj_kernel,
        grid=(rows // tm,),
        in_specs=[
            pl.BlockSpec((tm, o.shape[1]), lambda i: (i, 0)),
            pl.BlockSpec((tm, d), lambda i: (i, 0)),
            pl.BlockSpec(w.shape, lambda i: (0, 0)),
            pl.BlockSpec((1, d), lambda i: (0, 0)),
        ],
        out_specs=pl.BlockSpec((tm, d), lambda i: (i, 0)),
        out_shape=jax.ShapeDtypeStruct((rows, d), F32),
        compiler_params=_params("parallel"),
        name="oproj",
    )(o, x, w, g)


def _log_sigmoid(z):
    return jnp.minimum(z, 0.0) - jnp.log1p(jnp.exp(-jnp.abs(z)))


def _seq_specs(rows, width, tm, seq_len):
    tiles = seq_len // tm
    return (pl.BlockSpec((1, width, tm), lambda i: (i // tiles, 0, i % tiles)),
            jax.ShapeDtypeStruct((rows // seq_len, width, seq_len), F32))


def _fox_proj_kernel(x_ref, g_ref, wqkv_ref, wf_ref, bf_ref, q_ref, k_ref, v_ref, lf_ref, *copies, channel_major):
    h = _rms(x_ref[...], g_ref[...]).astype(BF16)
    d = q_ref.shape[1]
    qkv = _dot(h, wqkv_ref[...])
    q_ref[...] = qkv[:, :d]
    k, v = qkv[:, d:2 * d], qkv[:, 2 * d:]
    if channel_major:
        k_ref[0] = k.T
        v_ref[0] = v.T
        copies[0][...] = k.astype(BF16)
        copies[1][...] = v.astype(BF16)
    else:
        k_ref[...] = k
        v_ref[...] = v
    fl = _dot(h, wf_ref[...])
    lf_ref[...] = _log_sigmoid(fl[:, :FOX_HEADS] + bf_ref[...])


def _fox_proj(x, g, wqkv, wf, bf, *, tm, seq_len=None):
    rows, d = x.shape
    row_spec = pl.BlockSpec((tm, d), lambda i: (i, 0))
    full = lambda a: pl.BlockSpec(a.shape, lambda i: (0, 0))
    row_sds = jax.ShapeDtypeStruct((rows, d), F32)
    if seq_len is None:
        kv_specs, kv_shapes = [row_spec] * 2, [row_sds] * 2
        extra_specs, extra_shapes = [], []
    else:
        spec, sds = _seq_specs(rows, d, tm, seq_len)
        kv_specs, kv_shapes = [spec] * 2, [sds] * 2
        extra_specs, extra_shapes = [row_spec] * 2, [jax.ShapeDtypeStruct((rows, d), BF16)] * 2
    return pl.pallas_call(
        functools.partial(_fox_proj_kernel, channel_major=seq_len is not None),
        grid=(rows // tm,),
        in_specs=[row_spec, full(g), full(wqkv), full(wf), full(bf)],
        out_specs=[row_spec] + kv_specs + [pl.BlockSpec((tm, FOX_HEADS), lambda i: (i, 0))] + extra_specs,
        out_shape=[row_sds] + kv_shapes + [jax.ShapeDtypeStruct((rows, FOX_HEADS), F32)] + extra_shapes,
        compiler_params=_params("parallel"),
        name="fox_proj",
    )(x, g, wqkv, wf, bf)


def _rope_block(x, cos, sin_signed):
    lane = _iota(x.shape, 1)
    first_half = (lane % HEAD_DIM) < (HEAD_DIM // 2)
    rot = jnp.where(first_half, pltpu.roll(x, LANES - HEAD_DIM // 2, 1), pltpu.roll(x, HEAD_DIM // 2, 1))
    return x * cos + rot * sin_signed


def _nsa_proj_kernel(x_ref, g_ref, wr_ref, wp_ref, wg_ref, cos_ref, sin_ref, q_ref, gate_ref, *kv_refs, channel_major):
    h = _rms(x_ref[...], g_ref[...]).astype(BF16)
    cos = cos_ref[...]
    sin = sin_ref[...]
    r = _dot(h, wr_ref[...])
    nq = q_ref.shape[1]
    nkv = NSA_KV_HEADS * HEAD_DIM
    roped = [_rope_block(r[:, b * LANES:(b + 1) * LANES], cos, sin) for b in range(r.shape[1] // LANES)]
    nqb, nkb = nq // LANES, nkv // LANES
    q_ref[...] = jnp.concatenate(roped[:nqb], axis=1)
    kc, ks, kw = (jnp.concatenate(roped[nqb + e * nkb:nqb + (e + 1) * nkb], axis=1) for e in range(3))
    p = _dot(h, wp_ref[...])
    vc, vs, vw = (p[:, e * nkv:(e + 1) * nkv] for e in range(3))
    kv = (kc, vc, ks, vs, kw, vw)
    if channel_major:
        for ref, a in zip(kv_refs[:6], kv):
            ref[0] = a.T
        kv_refs[6][...] = kc
        kv_refs[7][...] = vc
        for ref, a in zip(kv_refs[8:], kv[2:]):
            ref[...] = a.astype(BF16)
    else:
        for ref, a in zip(kv_refs, kv):
            ref[...] = a
    gl = _dot(h, wg_ref[...])
    gate_ref[...] = jax.nn.sigmoid(gl[:, :gate_ref.shape[1]])


def _nsa_proj(x, g, wr, wp, wg, cos, sin, *, tm, seq_len=None):
    rows, d = x.shape
    nq = NSA_HEADS * HEAD_DIM
    nkv = NSA_KV_HEADS * HEAD_DIM
    ngate = 3 * NSA_HEADS
    n_tab = cos.shape[0] // tm
    row = lambda c: pl.BlockSpec((tm, c), lambda i: (i, 0))
    full = lambda a: pl.BlockSpec(a.shape, lambda i: (0, 0))
    tab = pl.BlockSpec((tm, LANES), lambda i: (i % n_tab, 0))
    sds = lambda c, dt=F32: jax.ShapeDtypeStruct((rows, c), dt)
    if seq_len is None:
        kv_specs, kv_shapes = [row(nkv)] * 6, [sds(nkv)] * 6
    else:
        spec, shape = _seq_specs(rows, nkv, tm, seq_len)
        kv_specs = [spec] * 6 + [row(nkv)] * 6
        kv_shapes = [shape] * 6 + [sds(nkv)] * 2 + [sds(nkv, BF16)] * 4
    return pl.pallas_call(
        functools.partial(_nsa_proj_kernel, channel_major=seq_len is not None),
        grid=(rows // tm,),
        in_specs=[row(d), full(g), full(wr), full(wp), full(wg), tab, tab],
        out_specs=[row(nq), row(ngate)] + kv_specs,
        out_shape=[sds(nq), sds(ngate)] + kv_shapes,
        compiler_params=_params("parallel"),
        name="nsa_proj",
    )(x, g, wr, wp, wg, cos, sin)


def _cumsum_kernel(lf_ref, ccol_ref, kaug_ref, *, blk):
    t = lf_ref.shape[1]
    lower = jnp.where(_iota((blk, blk), 1) <= _iota((blk, blk), 0), 1.0, 0.0).astype(BF16)
    carry = jnp.zeros((1, lf_ref.shape[2]), F32)
    for b in range(t // blk):
        cs = _dot3_r(lower, lf_ref[0, b * blk:(b + 1) * blk, :]) + carry
        ccol_ref[0, b * blk:(b + 1) * blk, :] = cs
        carry = cs[blk - 1:blk, :]
    nh = lf_ref.shape[2]
    width = kaug_ref.shape[2]
    col = _iota((nh, width), 1)
    head = _iota((nh, width), 0)
    target = LANES * (head // 2) + _bias_lane(head % 2, 1)
    pieces = _split3(-LOG2E * ccol_ref[0])
    kaug = jnp.where(_iota((t, width), 1) % LANES % BIAS_LANES_PER_HEAD < 3, 1.0, 0.0)
    kaug = jnp.where(_iota((t, width), 1) % LANES < 2 * BIAS_LANES_PER_HEAD, kaug, 0.0)
    for e, piece in enumerate(pieces):
        kaug = kaug + _dot(piece, jnp.where(col == target + e, 1.0, 0.0).astype(BF16))
    kaug_ref[0] = kaug.astype(BF16)


BIAS_LANES_PER_HEAD = 8


def _bias_lane(head_in_pair, side):
    return BIAS_LANES_PER_HEAD * head_in_pair + 3 * side


def _cumsum(lf, nlanes):
    n, t, h = lf.shape
    return pl.pallas_call(
        functools.partial(_cumsum_kernel, blk=256),
        grid=(n,),
        in_specs=[pl.BlockSpec((1, t, h), lambda i: (i, 0, 0))],
        out_specs=[pl.BlockSpec((1, t, h), lambda i: (i, 0, 0)), pl.BlockSpec((1, t, nlanes), lambda i: (i, 0, 0))],
        out_shape=[jax.ShapeDtypeStruct((n, t, h), F32), jax.ShapeDtypeStruct((n, t, nlanes), BF16)],
        compiler_params=_params("parallel"),
        name="fox_cumsum",
    )(lf)


def _fox_attn_kernel(q_ref, k_ref, v_ref, ccol_ref, kaug_ref, o_ref, m_ref, l_ref, acc_ref, *, tq, tk):
    hp = pl.program_id(1)
    i = pl.program_id(2)
    rows = 2 * tq
    q = q_ref[0] * SCALE2
    lane = _iota((tq, LANES), 1)
    qs = jnp.concatenate([jnp.where(lane < HEAD_DIM, q, 0.0), jnp.where(lane >= HEAD_DIM, q, 0.0)], axis=0)
    nh = ccol_ref.shape[2]
    head = _iota((nh, LANES), 0)
    col = _iota((nh, LANES), 1)
    qaug = []
    for e in range(2):
        a = jnp.where((lane >= _bias_lane(e, 1)) & (lane < _bias_lane(e, 1) + 3), 1.0, 0.0)
        for pi, piece in enumerate(_split3(LOG2E * ccol_ref[0])):
            place = jnp.where((head == 2 * hp + e) & (col == _bias_lane(e, 0) + pi), 1.0, 0.0).astype(BF16)
            a = a + _dot(piece, place)
        qaug.append(a)
    qs = jnp.concatenate([qs, jnp.concatenate(qaug, axis=0)], axis=1).astype(BF16)
    lane_minus_row = _iota((rows, LANES), 1) - _iota((rows, LANES), 0) % tq
    _init_stats(m_ref, l_ref, acc_ref)

    def tile(j, causal):
        k0 = pl.multiple_of(j * tk, tk)
        kt = jnp.concatenate([k_ref[0, pl.ds(k0, tk), :], kaug_ref[0, pl.ds(k0, tk), :]], axis=1)
        vt = v_ref[0, pl.ds(k0, tk), :]
        s = _dot_nt(qs, kt)
        if causal:
            s = jnp.concatenate([jnp.where(lane_minus_row <= i * tq - k0 - c, s[:, c:c + LANES], MASKED)
                                 for c in range(0, tk, LANES)], axis=1)
        p, alpha = _softmax_step(s, m_ref, l_ref)
        acc_ref[...] = alpha * acc_ref[...] + _dot(p, vt)

    def run(lo, hi, causal):
        def body(j, carry):
            tile(j, causal)
            return carry
        lax.fori_loop(lo, hi, body, 0)

    n_below = (i * tq) // tk
    run(0, n_below, False)
    run(n_below, (i * tq + tq + tk - 1) // tk, True)
    o = _normalize(acc_ref[...], l_ref)
    o_ref[0] = jnp.where(lane < HEAD_DIM, o[:tq], o[tq:])


def _fox_attn(q, k, v, ccol, kaug, *, tq=512, tk=512):
    n, t, d = q.shape
    npair = d // LANES
    rows = 2 * tq
    return pl.pallas_call(
        functools.partial(_fox_attn_kernel, tq=tq, tk=tk),
        grid=(n, npair, t // tq),
        in_specs=[
            pl.BlockSpec((1, tq, LANES), lambda b, h, i: (b, i, h)),
            pl.BlockSpec((1, t, LANES), lambda b, h, i: (b, 0, h)),
            pl.BlockSpec((1, t, LANES), lambda b, h, i: (b, 0, h)),
            pl.BlockSpec((1, tq, ccol.shape[2]), lambda b, h, i: (b, i, 0)),
            pl.BlockSpec((1, t, LANES), lambda b, h, i: (b, 0, h)),
        ],
        out_specs=pl.BlockSpec((1, tq, LANES), lambda b, h, i: (b, i, h)),
        out_shape=jax.ShapeDtypeStruct((n, t, d), F32),
        scratch_shapes=[pltpu.VMEM((rows, LANES), F32)] * 3,
        compiler_params=_params("parallel", "parallel", "arbitrary"),
        name="fox_attn",
    )(q, k, v, ccol, kaug)


def _fox_decode_kernel(pt_ref, q_ref, kn_ref, vn_ref, lfn_ref, *rest, nq, rows):
    npg = PAGES_PER_STEP
    k_refs, v_refs, lf_refs = rest[:npg], rest[npg:2 * npg], rest[2 * npg:3 * npg]
    o_ref = rest[3 * npg]
    qbd_ref, kcat_ref, vcat_ref, m_ref, l_ref, acc_ref, carry_ref, ncum_ref = rest[3 * npg + 1:]
    del pt_ref
    jj = pl.program_id(1)
    d = q_ref.shape[2]
    nh = FOX_HEADS
    pg = PAGE_SIZE
    r2 = _iota((rows, d), 0)
    block_diag = (r2 % nh == _iota((rows, d), 1) // HEAD_DIM) & (r2 < nq * nh)
    re = _iota((rows, nh), 0)
    head_of_row = jnp.where((re % nh == _iota((rows, nh), 1)) & (re < nq * nh), 1.0, 0.0).astype(BF16)

    @pl.when(jj == 0)
    def _():
        q = q_ref[0] * SCALE2
        pad = [jnp.zeros((rows - nq * nh, d), F32)] if rows > nq * nh else []
        qb = jnp.concatenate([jnp.broadcast_to(q[t:t + 1], (nh, d)) for t in range(nq)] + pad, axis=0)
        qbd = jnp.where(block_diag, qb, 0.0).astype(BF16)
        qbd_ref[...] = qbd
        kn = _pad_rows(kn_ref[0], pg).astype(BF16)
        vn = _pad_rows(vn_ref[0], pg).astype(BF16)
        sn = _dot_nt(qbd, kn)
        xn = _dot3_nt_r(head_of_row, _pad_rows(lfn_ref[0], pg)) * LOG2E
        t_r = _iota((rows, 1), 0) // nh
        cum = []
        c = jnp.zeros((rows, 1), F32)
        for s in range(nq):
            c = c + xn[:, s:s + 1]
            cum.append(c)
        lane = _iota((rows, pg), 1)
        ncum = cum[-1]
        cpre = jnp.broadcast_to(cum[-1], (rows, pg))
        for s in reversed(range(nq - 1)):
            ncum = jnp.where(t_r == s, cum[s], ncum)
            cpre = jnp.where(lane == s, cum[s], cpre)
        mask = (lane <= t_r) & (lane < nq)
        logit = jnp.where(mask, sn + ncum - cpre, MASKED)
        m = jnp.maximum(jnp.max(logit, axis=-1, keepdims=True), NEG_INF)
        p = jnp.exp2(logit - m)
        m_ref[...] = jnp.broadcast_to(m, m_ref.shape)
        l_ref[...] = p
        acc_ref[...] = _dot(p.astype(BF16), vn)
        carry_ref[...] = jnp.zeros(carry_ref.shape, F32)
        ncum_ref[...] = jnp.broadcast_to(ncum, ncum_ref.shape)

    for p in range(npg):
        kcat_ref[:, p * pg:(p + 1) * pg] = k_refs[p][...].astype(BF16)
        vcat_ref[:, p * pg:(p + 1) * pg] = v_refs[p][...].astype(BF16)
    s = _dot(qbd_ref[...], kcat_ref[...])
    lfcat = jnp.concatenate([lf_refs[p][...] for p in range(npg)], axis=1)
    x = _dot3_r(head_of_row, lfcat) * LOG2E
    y = jnp.concatenate([x[:, p * pg:(p + 1) * pg] for p in range(npg)], axis=0)
    ri, ci = _iota((pg, 2 * pg), 0), _iota((pg, 2 * pg), 1)
    later_and_total = jnp.where((ri >= ci) | (ci >= pg), 1.0, 0.0).astype(BF16)
    z = _dot3_l(y, later_and_total)
    c = carry_ref[...]
    ncum = ncum_ref[...]
    cols = [None] * npg
    for p in reversed(range(npg)):
        zp = z[p * rows:(p + 1) * rows]
        cols[p] = s[:, p * pg:(p + 1) * pg] + (zp[:, :pg] - y[p * rows:(p + 1) * rows] + c + ncum)
        c = c + zp[:, pg:]
    carry_ref[...] = c
    p, alpha = _softmax_step(jnp.concatenate(cols, axis=1), m_ref, l_ref)
    pv = _dot_nt(p, vcat_ref[...])
    for b in range(0, d, LANES):
        acc_ref[:, b:b + LANES] = alpha * acc_ref[:, b:b + LANES] + pv[:, b:b + LANES]

    @pl.when(jj == pl.num_programs(1) - 1)
    def _():
        o = jnp.where(block_diag, _normalize(acc_ref[...], l_ref), 0.0)
        for t in range(nq):
            o_ref[0, t:t + 1, :] = jnp.sum(o[t * nh:(t + 1) * nh], axis=0, keepdims=True)


def _fox_decode(page_table, q, k_new, v_new, lf_new, cache_kt, cache_vt, cache_lft, layer):
    b, nq, d = q.shape
    n_pages = page_table.shape[1]
    npg = PAGES_PER_STEP
    assert n_pages % npg == 0
    rows = -(-nq * FOX_HEADS // (2 * SUBLANES)) * (2 * SUBLANES)

    def page_spec(height, p):
        return pl.BlockSpec((None, None, height, PAGE_SIZE),
                            lambda i, jj, pt: (layer, pt[i, n_pages - npg * (jj + 1) + p], 0, 0))

    tok = lambda w: pl.BlockSpec((1, nq, w), lambda i, jj, pt: (i, 0, 0))
    grid_spec = pltpu.PrefetchScalarGridSpec(
        num_scalar_prefetch=1,
        grid=(b, n_pages // npg),
        in_specs=[tok(d), tok(d), tok(d), tok(FOX_HEADS)]
        + [page_spec(d, p) for p in range(npg)] * 2 + [page_spec(FOX_HEADS, p) for p in range(npg)],
        out_specs=pl.BlockSpec((1, nq, d), lambda i, jj, pt: (i, 0, 0)),
        scratch_shapes=[
            pltpu.VMEM((rows, d), BF16),
            pltpu.VMEM((d, npg * PAGE_SIZE), BF16),
            pltpu.VMEM((d, npg * PAGE_SIZE), BF16),
            pltpu.VMEM((rows, LANES), F32),
            pltpu.VMEM((rows, LANES), F32),
            pltpu.VMEM((rows, d), F32),
            pltpu.VMEM((rows, LANES), F32),
            pltpu.VMEM((rows, LANES), F32),
        ],
    )
    return pl.pallas_call(
        functools.partial(_fox_decode_kernel, nq=nq, rows=rows),
        grid_spec=grid_spec,
        out_shape=jax.ShapeDtypeStruct((b, nq, d), F32),
        compiler_params=_params("parallel", "arbitrary"),
        name="fox_decode",
    )(page_table, q, k_new, v_new, lf_new, *([cache_kt] * npg), *([cache_vt] * npg), *([cache_lft] * npg))


def _compress_finish(a, b, w2):
    n = a.shape[0]
    hid = _silu(a + pltpu.roll(b, n - 1, 0))
    out = _dot(hid.astype(BF16), w2)
    return jnp.where(_iota(out.shape, 0) < n - 1, out, 0.0)


def _compress_rows(read_rows, n_chunk, pe_ref, w1_ref, w2_ref):
    half = CMP_STRIDE
    a = jnp.zeros((n_chunk, w2_ref.shape[1]), F32)
    b = jnp.zeros((n_chunk, w2_ref.shape[1]), F32)
    for j in range(half):
        xj = read_rows(j)
        a = a + _dot((xj + pe_ref[j:j + 1, :]).astype(BF16), w1_ref[j])
        b = b + _dot((xj + pe_ref[half + j:half + j + 1, :]).astype(BF16), w1_ref[half + j])
    return _compress_finish(a, b, w2_ref[...])


def _compress_kernel(*refs, nblk):
    x_refs = refs[:nblk]
    pe_ref, w1_ref, w2_ref, o_ref = refs[nblk:]
    n_chunk = o_ref.shape[1]
    read = lambda j: jnp.concatenate([r[0, pl.ds(j, n_chunk, stride=CMP_STRIDE), :] for r in x_refs], axis=1)
    o_ref[0] = _compress_rows(read, n_chunk, pe_ref, w1_ref, w2_ref)


def _compress(x, pe, w1, w2):
    n, t, gkv = x.shape
    assert CMP_LEN == 2 * CMP_STRIDE and t % CMP_STRIDE == 0
    c = t // CMP_STRIDE
    nblk = gkv // LANES
    full = lambda a: pl.BlockSpec(a.shape, lambda i: (0,) * a.ndim)
    lane_block = lambda e: pl.BlockSpec((1, t, LANES), lambda i: (i, 0, e))
    return pl.pallas_call(
        functools.partial(_compress_kernel, nblk=nblk),
        grid=(n,),
        in_specs=[lane_block(e) for e in range(nblk)] + [full(pe), full(w1), full(w2)],
        out_specs=pl.BlockSpec((1, c, gkv), lambda i: (i, 0, 0)),
        out_shape=jax.ShapeDtypeStruct((n, c, gkv), F32),
        compiler_params=_params("parallel"),
        name="nsa_compress",
    )(*([x] * nblk), pe, w1, w2)


def _nsa_attn_kernel(q_ref, kc_ref, vc_ref, ks_ref, vs_ref, kw_ref, vw_ref, gate_ref, blockmask_ref, o_ref,
                     m_ref, l_ref, acc_ref, *, tq, tk, n_cmp, n_sel):
    g = pl.program_id(1)
    i = pl.program_id(2)
    par = g % 2
    hp = NSA_GROUP
    rows = hp * tq
    q = q_ref[0] * SCALE2
    qs = jnp.concatenate([_to_half(_head_block(q, j), j % 2, par) for j in range(hp)], axis=0).astype(BF16)
    qpos = i * tq + _iota((rows, 1), 0) % tq
    row_minus_lane = _iota((rows, LANES), 0) % tq - _iota((rows, LANES), 1)

    ncp = kc_ref.shape[1]
    ci = _iota((1, ncp), 1)
    mask = (ci * CMP_STRIDE + CMP_LEN - 1 <= qpos) & (ci < n_cmp)
    p = _masked_softmax2(_dot_nt(qs, kc_ref[0].astype(BF16)), mask)
    o_cmp = _dot(p.astype(BF16), vc_ref[0].astype(BF16))
    psum = p[0:tq]
    for j in range(1, hp):
        psum = psum + p[j * tq:(j + 1) * tq]
    nsr = -(-n_sel // SUBLANES) * SUBLANES
    imp = _dot3_nt_r(_cover_matrix((nsr, ncp), 1, n_cmp, n_sel), psum)
    tpos = i * tq + _iota((1, tq), 1)
    sel_t = _top_mask(_selection_scores(imp, tpos, n_sel, 0), n_sel, min(SLC_TOP, n_sel), 0)
    unsel = jnp.concatenate([1.0 - sel_t, jnp.zeros((LANES - nsr, tq), F32)], axis=0).T.astype(BF16)
    qs_slc = jnp.concatenate([qs, jnp.concatenate([unsel] * hp, axis=0)], axis=1)

    def attend(lhs, kt, vt, transform):
        p, alpha = _softmax_step(transform(_dot_nt(lhs, kt)), m_ref, l_ref)
        acc_ref[...] = alpha * acc_ref[...] + _dot(p, vt)

    def loop(lo, hi, fn):
        def body(j, carry):
            fn(j)
            return carry
        lax.fori_loop(lo, hi, body, 0)

    def slc_tile(j, causal):
        k0 = pl.multiple_of(j * tk, tk)

        def transform(s):
            if not causal:
                return s
            cols = [jnp.where(row_minus_lane >= k0 + c - i * tq, s[:, c:c + LANES], MASKED)
                    for c in range(0, tk, LANES)]
            return jnp.concatenate(cols, axis=1)

        kt = jnp.concatenate([ks_ref[0, pl.ds(k0, tk), :], blockmask_ref[pl.ds(k0, tk), :]], axis=1)
        attend(qs_slc, kt, vs_ref[0, pl.ds(k0, tk), :], transform)

    _init_stats(m_ref, l_ref, acc_ref)
    n_below = (i * tq) // tk
    loop(0, n_below, lambda j: slc_tile(j, False))
    loop(n_below, (i * tq + tq + tk - 1) // tk, lambda j: slc_tile(j, True))
    o_slc = _normalize(acc_ref[...], l_ref)

    win_w = WINDOW + tq
    k0w = pl.multiple_of(jnp.maximum(i * tq - WINDOW, 0), tq)

    def win_transform(s):
        base = i * tq - k0w
        cols = []
        for c in range(0, win_w, LANES):
            sc = jnp.where(row_minus_lane >= c - base, s[:, c:c + LANES], MASKED)
            cols.append(jnp.where(row_minus_lane < WINDOW + c - base, sc, MASKED))
        return jnp.concatenate(cols, axis=1)

    _init_stats(m_ref, l_ref, acc_ref)
    attend(qs, kw_ref[0, pl.ds(k0w, win_w), :], vw_ref[0, pl.ds(k0w, win_w), :], win_transform)
    o_win = _normalize(acc_ref[...], l_ref)

    gates = gate_ref[0]
    gl = _iota(gates.shape, 1)
    outs = []
    for j in range(hp):
        head = g * hp + j
        gc = [jnp.sum(jnp.where(gl == c * NSA_HEADS + head, gates, 0.0), axis=1, keepdims=True) for c in range(3)]
        sl = slice(j * tq, (j + 1) * tq)
        outs.append(_to_half(gc[0] * o_cmp[sl] + gc[1] * o_slc[sl] + gc[2] * o_win[sl], par, j % 2))
    o_ref[0] = jnp.concatenate([outs[2 * e] + outs[2 * e + 1] for e in range(hp // 2)], axis=1)


def _nsa_attn(q, kcmp, vcmp, ks, vs, kw, vw, gates, *, tq=512, tk=512):
    n, t, _ = q.shape
    gw = NSA_GROUP * HEAD_DIM
    ncp = kcmp.shape[1]
    n_cmp = t // CMP_STRIDE - CMP_LEN // CMP_STRIDE + 1
    n_sel = -(-t // SLC_BLOCK)
    assert n_sel <= LANES and t % tq == 0 and t % tk == 0 and WINDOW % tq == 0 and t >= WINDOW + tq
    rows = NSA_GROUP * tq
    blockmask = jnp.where(jnp.arange(t, dtype=jnp.int32)[:, None] // SLC_BLOCK
                          == jnp.arange(LANES, dtype=jnp.int32)[None, :], MASKED, 0.0).astype(BF16)
    pair = lambda r: pl.BlockSpec((1, r, LANES), lambda b, g, i: (b, 0, g // 2))
    return pl.pallas_call(
        functools.partial(_nsa_attn_kernel, tq=tq, tk=tk, n_cmp=n_cmp, n_sel=n_sel),
        grid=(n, NSA_KV_HEADS, t // tq),
        in_specs=[pl.BlockSpec((1, tq, gw), lambda b, g, i: (b, i, g)), pair(ncp), pair(ncp),
                  pair(t), pair(t), pair(t), pair(t),
                  pl.BlockSpec((1, tq, gates.shape[2]), lambda b, g, i: (b, i, 0)),
                  pl.BlockSpec(blockmask.shape, lambda b, g, i: (0, 0))],
        out_specs=pl.BlockSpec((1, tq, gw), lambda b, g, i: (b, i, g)),
        out_shape=jax.ShapeDtypeStruct(q.shape, F32),
        scratch_shapes=[pltpu.VMEM((rows, LANES), F32)] * 3,
        compiler_params=_params("parallel", "parallel", "arbitrary"),
        name="nsa_attn",
    )(q, kcmp, vcmp, ks, vs, kw, vw, gates, blockmask)


def _group_queries(q8, g, par):
    return jnp.concatenate([_to_half(_head_block(q8, h), h % 2, par)
                            for h in range(g * NSA_GROUP, (g + 1) * NSA_GROUP)], axis=0).astype(BF16)


def _nsa_decode_cmp_kernel(pt_ref, q_ref, *rest, past_len, n_cmp, n_sel):
    npg = NSA_PAGES_PER_STEP
    kpages, vpages = rest[:npg], rest[npg:2 * npg]
    pek_ref, w1k_ref, w2k_ref, pev_ref, w1v_ref, w2v_ref = rest[2 * npg:2 * npg + 6]
    o_ref, sel_ref = rest[2 * npg + 6:2 * npg + 8]
    xk_ref, xv_ref = rest[2 * npg + 8:]
    del pt_ref
    jj = pl.program_id(1)
    pg = PAGE_SIZE
    r0 = jj * (npg * pg)
    nblk = xk_ref.shape[0]
    for p in range(npg):
        dst = pl.ds(pl.multiple_of(r0 + p * pg, pg), pg)
        xk = kpages[p][...].T
        xv = vpages[p][...].T
        for e in range(nblk):
            xk_ref[e, dst, :] = xk[:, e * LANES:(e + 1) * LANES]
            xv_ref[e, dst, :] = xv[:, e * LANES:(e + 1) * LANES]

    @pl.when(jj == pl.num_programs(1) - 1)
    def _():
        n_chunk = xk_ref.shape[1] // CMP_STRIDE

        def compress(x_ref, pe_ref, w1_ref, w2_ref):
            read = lambda j: jnp.concatenate(
                [x_ref[e, pl.ds(j, n_chunk, stride=CMP_STRIDE), :] for e in range(nblk)], axis=1)
            return _compress_rows(read, n_chunk, pe_ref, w1_ref, w2_ref)

        kcmp = compress(xk_ref, pek_ref, w1k_ref, w2k_ref)
        vcmp = compress(xv_ref, pev_ref, w1v_ref, w2v_ref)
        ncp = kcmp.shape[0]
        q8 = _pad_rows(q_ref[0] * SCALE2, SUBLANES)
        ci = _iota((1, ncp), 1)
        cover = _cover_matrix((ncp, sel_ref.shape[3]), 0, n_cmp, n_sel)
        pos8 = past_len + _iota((SUBLANES, 1), 0)
        pieces = [None] * NSA_HEADS
        for g in range(NSA_KV_HEADS):
            par = g % 2
            lanes = slice((g // 2) * LANES, (g // 2 + 1) * LANES)
            qs = _group_queries(q8, g, par)
            pos = past_len + _iota((qs.shape[0], 1), 0) % SUBLANES
            mask = (ci * CMP_STRIDE + CMP_LEN - 1 <= pos) & (ci < n_cmp)
            p = _masked_softmax2(_dot_nt(qs, kcmp[:, lanes].astype(BF16)), mask)
            o = _dot(p.astype(BF16), vcmp[:, lanes].astype(BF16))
            psum = p[0:SUBLANES]
            for j in range(1, NSA_GROUP):
                psum = psum + p[j * SUBLANES:(j + 1) * SUBLANES]
            imp = _dot3_l(psum, cover)
            sel_ref[0, g] = _top_mask(_selection_scores(imp, pos8, n_sel, 1), n_sel, min(SLC_TOP, n_sel), 1)
            for j in range(NSA_GROUP):
                h = g * NSA_GROUP + j
                pieces[h] = _to_half(o[j * SUBLANES:(j + 1) * SUBLANES], par, h % 2)
        o_ref[0] = _assemble_heads(pieces)


def _nsa_decode_cmp(page_table, q, cache_kt, cache_vt, wk, wv, layer, *, nsp):
    b, nq, dq = q.shape
    n_pages = page_table.shape[1]
    npg = NSA_PAGES_PER_STEP
    past_len = n_pages * PAGE_SIZE
    assert n_pages % npg == 0 and nq <= SUBLANES and nq < CMP_STRIDE and CMP_LEN == 2 * CMP_STRIDE
    n_chunk = (past_len + nq) // CMP_STRIDE
    assert n_chunk == n_pages * CHUNKS_PER_PAGE
    n_cmp = n_chunk - CMP_LEN // CMP_STRIDE + 1
    n_sel = -(-(past_len + nq) // SLC_BLOCK)
    assert n_sel <= nsp
    gkv = NSA_KV_HEADS * HEAD_DIM

    def page_spec(p):
        return pl.BlockSpec((None, None, gkv, PAGE_SIZE), lambda i, jj, pt: (layer, pt[i, npg * jj + p], 0, 0))

    full = lambda a: pl.BlockSpec(a.shape, lambda i, jj, pt: (0,) * a.ndim)
    grid_spec = pltpu.PrefetchScalarGridSpec(
        num_scalar_prefetch=1,
        grid=(b, n_pages // npg),
        in_specs=[pl.BlockSpec((1, nq, dq), lambda i, jj, pt: (i, 0, 0))]
        + [page_spec(p) for p in range(npg)] * 2 + [full(a) for a in wk] + [full(a) for a in wv],
        out_specs=[pl.BlockSpec((1, SUBLANES, dq), lambda i, jj, pt: (i, 0, 0)),
                   pl.BlockSpec((1, NSA_KV_HEADS, SUBLANES, nsp), lambda i, jj, pt: (i, 0, 0, 0))],
        scratch_shapes=[pltpu.VMEM((gkv // LANES, past_len, LANES), F32)] * 2,
    )
    return pl.pallas_call(
        functools.partial(_nsa_decode_cmp_kernel, past_len=past_len, n_cmp=n_cmp, n_sel=n_sel),
        grid_spec=grid_spec,
        out_shape=[jax.ShapeDtypeStruct((b, SUBLANES, dq), F32),
                   jax.ShapeDtypeStruct((b, NSA_KV_HEADS, SUBLANES, nsp), F32)],
        compiler_params=_params("parallel", "arbitrary"),
        name="nsa_decode_cmp",
    )(page_table, q, *([cache_kt] * npg), *([cache_vt] * npg), *wk, *wv)


def _nsa_decode_slc_kernel(pt_ref, q_ref, kn_ref, vn_ref, sel_ref, *rest, nq, past_len):
    npg = NSA_PAGES_PER_STEP
    kpages, vpages = rest[:npg], rest[npg:2 * npg]
    o_ref = rest[2 * npg]
    qs_ref, kcat_ref, vcat_ref, m_ref, l_ref, acc_ref = rest[2 * npg + 1:]
    del pt_ref
    jj = pl.program_id(1)
    pg = PAGE_SIZE
    npair = NSA_KV_HEADS // 2
    nsp = sel_ref.shape[3]
    selrows = [jnp.concatenate([sel_ref[0, 2 * gp + gi] for gi in range(2) for _ in range(NSA_GROUP)], axis=0)
               for gp in range(npair)]
    rows = selrows[0].shape[0]

    @pl.when(jj == 0)
    def _():
        q8 = _pad_rows(q_ref[0] * SCALE2, SUBLANES)
        kn = _pad_rows(kn_ref[0], pg).astype(BF16)
        vn = _pad_rows(vn_ref[0], pg).astype(BF16)
        t_r = _iota((rows, 1), 0) % SUBLANES
        lane = _iota((rows, pg), 1)
        for gp in range(npair):
            lanes = slice(gp * LANES, (gp + 1) * LANES)
            qs = jnp.concatenate([_group_queries(q8, 2 * gp + gi, gi) for gi in range(2)], axis=0)
            qs_ref[gp] = qs
            _init_stats(m_ref.at[gp], l_ref.at[gp], acc_ref.at[gp])
            new_block = selrows[gp][:, past_len // SLC_BLOCK:past_len // SLC_BLOCK + 1]
            mask = (lane <= t_r) & (lane < nq) & (new_block > 0.5)
            s = jnp.where(mask, _dot_nt(qs, kn[:, lanes]), MASKED)
            p, alpha = _softmax_step(s, m_ref.at[gp], l_ref.at[gp])
            acc_ref[gp] = alpha * acc_ref[gp] + _dot(p, vn[:, lanes])

    for p in range(npg):
        kcat_ref[:, p * pg:(p + 1) * pg] = kpages[p][...].astype(BF16)
        vcat_ref[:, p * pg:(p + 1) * pg] = vpages[p][...].astype(BF16)
    width = npg * pg
    block_of_key = jj * (width // SLC_BLOCK) + _iota((nsp, width), 1) // SLC_BLOCK
    expand = jnp.where(block_of_key == _iota((nsp, width), 0), 1.0, 0.0).astype(BF16)
    for gp in range(npair):
        rws = slice(gp * LANES, (gp + 1) * LANES)
        selx = _dot(selrows[gp].astype(BF16), expand)
        s = jnp.where(selx > 0.5, _dot(qs_ref[gp], kcat_ref[rws, :]), MASKED)
        p, alpha = _softmax_step(s, m_ref.at[gp], l_ref.at[gp])
        acc_ref[gp] = alpha * acc_ref[gp] + _dot_nt(p, vcat_ref[rws, :])

    @pl.when(jj == pl.num_programs(1) - 1)
    def _():
        pieces = [None] * NSA_HEADS
        for gp in range(npair):
            o = _normalize(acc_ref[gp], l_ref.at[gp])
            for gi in range(2):
                for j in range(NSA_GROUP):
                    h = (2 * gp + gi) * NSA_GROUP + j
                    r0 = (gi * NSA_GROUP + j) * SUBLANES
                    pieces[h] = _to_half(o[r0:r0 + SUBLANES], gi, h % 2)
        o_ref[0] = _assemble_heads(pieces)


def _nsa_decode_slc(page_table, q, k_new, v_new, sel, cache_kt, cache_vt, layer):
    b, nq, dq = q.shape
    gkv = k_new.shape[2]
    n_pages = page_table.shape[1]
    npg = NSA_PAGES_PER_STEP
    past_len = n_pages * PAGE_SIZE
    assert n_pages % npg == 0 and past_len % SLC_BLOCK == 0 and nq <= SLC_BLOCK
    npair = NSA_KV_HEADS // 2
    rows = 2 * NSA_GROUP * SUBLANES

    def page_spec(p):
        return pl.BlockSpec((None, None, gkv, PAGE_SIZE), lambda i, jj, pt: (layer, pt[i, npg * jj + p], 0, 0))

    tok = lambda w: pl.BlockSpec((1, nq, w), lambda i, jj, pt: (i, 0, 0))
    grid_spec = pltpu.PrefetchScalarGridSpec(
        num_scalar_prefetch=1,
        grid=(b, n_pages // npg),
        in_specs=[tok(dq), tok(gkv), tok(gkv),
                  pl.BlockSpec((1,) + sel.shape[1:], lambda i, jj, pt: (i, 0, 0, 0))]
        + [page_spec(p) for p in range(npg)] * 2,
        out_specs=pl.BlockSpec((1, SUBLANES, dq), lambda i, jj, pt: (i, 0, 0)),
        scratch_shapes=[
            pltpu.VMEM((npair, rows, LANES), BF16),
            pltpu.VMEM((gkv, npg * PAGE_SIZE), BF16),
            pltpu.VMEM((gkv, npg * PAGE_SIZE), BF16),
            pltpu.VMEM((npair, rows, LANES), F32),
            pltpu.VMEM((npair, rows, LANES), F32),
            pltpu.VMEM((npair, rows, LANES), F32),
        ],
    )
    return pl.pallas_call(
        functools.partial(_nsa_decode_slc_kernel, nq=nq, past_len=past_len),
        grid_spec=grid_spec,
        out_shape=jax.ShapeDtypeStruct((b, SUBLANES, dq), F32),
        compiler_params=_params("parallel", "arbitrary"),
        name="nsa_decode_slc",
    )(page_table, q, k_new, v_new, sel, *([cache_kt] * npg), *([cache_vt] * npg))


def _nsa_decode_win_kernel(q_ref, kn_ref, vn_ref, sk_ref, sv_ref, ocmp_ref, oslc_ref, gate_ref,
                           o_ref, nk_ref, nv_ref, *, nq):
    nbuf = sk_ref.shape[1]
    pg = PAGE_SIZE
    q8 = _pad_rows(q_ref[0] * SCALE, SUBLANES)
    kn = _pad_rows(kn_ref[0], pg).astype(BF16)
    vn = _pad_rows(vn_ref[0], pg).astype(BF16)
    gates = _pad_rows(gate_ref[0], SUBLANES)
    gl = _iota(gates.shape, 1)
    rows = NSA_GROUP * SUBLANES
    t_r = _iota((rows, 1), 0) % SUBLANES
    dist_s = t_r + nbuf - _iota((1, nbuf), 1)
    mask_s = (dist_s >= 0) & (dist_s < WINDOW)
    lane_n = _iota((1, pg), 1)
    dist_n = t_r - lane_n
    mask_n = (dist_n >= 0) & (dist_n < WINDOW) & (lane_n < nq)
    pieces = [None] * NSA_HEADS
    for g in range(NSA_KV_HEADS):
        par = g % 2
        lanes = slice((g // 2) * LANES, (g // 2 + 1) * LANES)
        qs = _group_queries(q8, g, par)
        ss = jnp.where(mask_s, _dot_nt(qs, sk_ref[0, :, lanes].astype(BF16)), NEG_INF)
        sn = jnp.where(mask_n, _dot_nt(qs, kn[:, lanes]), NEG_INF)
        m = jnp.maximum(jnp.max(ss, axis=-1, keepdims=True), jnp.max(sn, axis=-1, keepdims=True))
        ps = jnp.where(mask_s, jnp.exp(ss - m), 0.0)
        pn = jnp.where(mask_n, jnp.exp(sn - m), 0.0)
        den = jnp.sum(ps, axis=-1, keepdims=True) + jnp.sum(pn, axis=-1, keepdims=True)
        ow = (_dot(ps.astype(BF16), sv_ref[0, :, lanes].astype(BF16)) + _dot(pn.astype(BF16), vn[:, lanes]))
        ow = ow / jnp.maximum(den, 1e-30)
        for j in range(NSA_GROUP):
            h = g * NSA_GROUP + j
            pieces[h] = _to_half(ow[j * SUBLANES:(j + 1) * SUBLANES], par, h % 2)
    o_win = _assemble_heads(pieces)
    head_of_lane = _iota((SUBLANES, o_win.shape[1]), 1) // HEAD_DIM
    branch_gate = []
    for c in range(3):
        gfull = jnp.zeros(o_win.shape, F32)
        for h in range(NSA_HEADS):
            col = jnp.sum(jnp.where(gl == c * NSA_HEADS + h, gates, 0.0), axis=1, keepdims=True)
            gfull = jnp.where(head_of_lane == h, col, gfull)
        branch_gate.append(gfull)
    o = branch_gate[0] * ocmp_ref[0] + branch_gate[1] * oslc_ref[0] + branch_gate[2] * o_win
    o_ref[0] = o[:nq]
    nk_ref[0, 0:nbuf - nq, :] = sk_ref[0, nq:nbuf, :]
    nk_ref[0, nbuf - nq:nbuf, :] = kn_ref[0]
    nv_ref[0, 0:nbuf - nq, :] = sv_ref[0, nq:nbuf, :]
    nv_ref[0, nbuf - nq:nbuf, :] = vn_ref[0]


def _nsa_decode_win(q, k_new, v_new, state_k, state_v, o_cmp, o_slc, gates):
    b, nq, dq = q.shape
    nbuf, gkv = state_k.shape[1:]
    assert nbuf == WINDOW and nq <= SUBLANES
    blk = lambda a: pl.BlockSpec((1,) + a.shape[1:], lambda i: (i, 0, 0))
    return pl.pallas_call(
        functools.partial(_nsa_decode_win_kernel, nq=nq),
        grid=(b,),
        in_specs=[blk(a) for a in (q, k_new, v_new, state_k, state_v, o_cmp, o_slc, gates)],
        out_specs=[blk(q), blk(state_k), blk(state_v)],
        out_shape=[jax.ShapeDtypeStruct(q.shape, F32), jax.ShapeDtypeStruct(state_k.shape, F32),
                   jax.ShapeDtypeStruct(state_v.shape, F32)],
        compiler_params=_params("parallel"),
        name="nsa_decode_win",
    )(q, k_new, v_new, state_k, state_v, o_cmp, o_slc, gates)


def _rope_tables(pos):
    half = HEAD_DIM // 2
    inv = ROPE_THETA ** (-jnp.arange(half, dtype=F32) / half)
    ang = pos.astype(F32)[:, None] * inv[None, :]
    cos, sin = jnp.cos(ang), jnp.sin(ang)
    reps = LANES // HEAD_DIM
    return (jnp.tile(jnp.concatenate([cos, cos], axis=1), (1, reps)),
            jnp.tile(jnp.concatenate([-sin, sin], axis=1), (1, reps)))


def _compress_weights(pe, w1, w2):
    ng = NSA_KV_HEADS
    gkv = ng * HEAD_DIM
    eye = jnp.eye(ng, dtype=F32)
    w1_bd = jnp.einsum('jde,gh->jgdhe', w1, eye).reshape(CMP_LEN, gkv, gkv).astype(BF16)
    return jnp.tile(pe, (1, ng)), w1_bd, jnp.kron(eye, w2).astype(BF16)


def _pad_cols(w, cols):
    return jnp.pad(w, ((0, 0), (0, cols - w.shape[1])))


def _transposed_pages(cache):
    l, pool, pg, nh, hd = cache.shape
    return jnp.transpose(cache, (0, 1, 3, 4, 2)).reshape(l, pool, nh * hd, pg)


TM_PROMPT_FFN = 1024
TM_PROMPT = 512


def kernel(x_prompt, x_sample, cache_fox_k, cache_fox_v, cache_fox_logf, cache_nsa_cmp_k, cache_nsa_cmp_v, cache_nsa_slc_k, cache_nsa_slc_v, state_nsa_win_k, state_nsa_win_v, page_table, norm_pre, norm_post, ffn1_w_in, ffn1_w_out, ffn2_w_in, ffn2_w_out, fox_w_in, fox_b_f, fox_w_out, nsa_w_in, nsa_cmp_pe_k, nsa_cmp_w1_k, nsa_cmp_w2_k, nsa_cmp_pe_v, nsa_cmp_w1_v, nsa_cmp_w2_v, nsa_w_out):
    n, t, d = x_prompt.shape
    b, nq, _ = x_sample.shape
    depth = norm_pre.shape[0]
    assert depth == 2 and fox_w_in.shape[0] == 1 and nsa_w_in.shape[0] == 1
    n_pages = page_table.shape[1]
    past_len = n_pages * PAGE_SIZE
    rs = b * nq
    row = lambda a: a.reshape(1, -1)
    xp = x_prompt.reshape(n * t, d)
    xs = x_sample.reshape(rs, d)

    def ffn_pair(xp, xs, layer, slot, w_in, w_out):
        wi, wo = w_in[layer].astype(BF16), w_out[layer].astype(BF16)
        gpre, gpost = row(norm_pre[layer, slot]), row(norm_post[layer, slot])
        return (_ffn(xp, gpre, gpost, wi, wo, tm=TM_PROMPT_FFN), _ffn(xs, gpre, gpost, wi, wo, tm=rs))

    xp, xs = ffn_pair(xp, xs, 0, 0, ffn1_w_in, ffn1_w_out)
    dh = FOX_HEADS * HEAD_DIM
    wqkv = fox_w_in[0][:, :3 * dh].astype(BF16)
    wf = _pad_cols(fox_w_in[0][:, 3 * dh:], LANES).astype(BF16)
    bf = row(fox_b_f[0])
    wout = fox_w_out[0].astype(BF16)
    gpre, gpost = row(norm_pre[0, 1]), row(norm_post[0, 1])

    qp, fox_kt_p, fox_vt_p, fox_lf_p, kb, vb = _fox_proj(xp, gpre, wqkv, wf, bf, tm=TM_PROMPT, seq_len=t)
    ccol, kaug = _cumsum(fox_lf_p.reshape(n, t, FOX_HEADS), dh)
    op = _fox_attn(qp.reshape(n, t, dh), kb.reshape(n, t, dh), vb.reshape(n, t, dh), ccol, kaug)
    xp = _oproj(op.reshape(n * t, dh), xp, wout, gpost, tm=TM_PROMPT)

    qs, fox_k_s, fox_v_s, fox_lf_s = _fox_proj(xs, gpre, wqkv, wf, bf, tm=rs)
    os_ = _fox_decode(page_table, qs.reshape(b, nq, dh), fox_k_s.reshape(b, nq, dh), fox_v_s.reshape(b, nq, dh),
                      fox_lf_s.reshape(b, nq, FOX_HEADS),
                      _transposed_pages(cache_fox_k), _transposed_pages(cache_fox_v),
                      jnp.transpose(cache_fox_logf, (0, 1, 3, 2)), 0)
    xs = _oproj(os_.reshape(rs, dh), xs, wout, gpost, tm=rs)
    xp, xs = ffn_pair(xp, xs, 0, 2, ffn2_w_in, ffn2_w_out)

    xp, xs = ffn_pair(xp, xs, 1, 0, ffn1_w_in, ffn1_w_out)
    dq = NSA_HEADS * HEAD_DIM
    gkv = NSA_KV_HEADS * HEAD_DIM
    w = nsa_w_in[0]
    cut = lambda k: w[:, dq + k * gkv:dq + (k + 1) * gkv]
    wr = jnp.concatenate([w[:, :dq], cut(0), cut(2), cut(4)], axis=1).astype(BF16)
    wp = jnp.concatenate([cut(1), cut(3), cut(5)], axis=1).astype(BF16)
    wg = _pad_cols(w[:, dq + 6 * gkv:], LANES).astype(BF16)
    wout = nsa_w_out[0].astype(BF16)
    wk = _compress_weights(nsa_cmp_pe_k[0], nsa_cmp_w1_k[0], nsa_cmp_w2_k[0])
    wv = _compress_weights(nsa_cmp_pe_v[0], nsa_cmp_w1_v[0], nsa_cmp_w2_v[0])
    gpre, gpost = row(norm_pre[1, 1]), row(norm_post[1, 1])

    cos_p, sin_p = _rope_tables(jnp.arange(t, dtype=jnp.int32))
    q, gates, *kv = _nsa_proj(xp, gpre, wr, wp, wg, cos_p, sin_p, tm=TM_PROMPT, seq_len=t)
    nsa_t_p = kv[:6]
    kc, vc, ksb, vsb, kwb, vwb = kv[6:]
    seq = lambda a: a.reshape(n, t, a.shape[1])
    kcmp = _compress(seq(kc), *wk)
    vcmp = _compress(seq(vc), *wv)
    op = _nsa_attn(seq(q), kcmp, vcmp, seq(ksb), seq(vsb), seq(kwb), seq(vwb), seq(gates))
    xp = _oproj(op.reshape(n * t, dq), xp, wout, gpost, tm=TM_PROMPT)

    cos_s, sin_s = _rope_tables(past_len + jnp.arange(nq, dtype=jnp.int32))
    cos_s, sin_s = jnp.tile(cos_s, (b, 1)), jnp.tile(sin_s, (b, 1))
    q, gates, kc, vc, ks, vs, kw, vw = _nsa_proj(xs, gpre, wr, wp, wg, cos_s, sin_s, tm=rs)
    tok = lambda a: a.reshape(b, nq, a.shape[1])
    n_sel = -(-(past_len + nq) // SLC_BLOCK)
    nsp = -(-n_sel // LANES) * LANES
    o_cmp, sel = _nsa_decode_cmp(page_table, tok(q), _transposed_pages(cache_nsa_cmp_k),
                                 _transposed_pages(cache_nsa_cmp_v), wk, wv, 0, nsp=nsp)
    o_slc = _nsa_decode_slc(page_table, tok(q), tok(ks), tok(vs), sel,
                            _transposed_pages(cache_nsa_slc_k), _transposed_pages(cache_nsa_slc_v), 0)
    nbuf = state_nsa_win_k.shape[2]
    os_, win_k_s, win_v_s = _nsa_decode_win(tok(q), tok(kw), tok(vw), state_nsa_win_k.reshape(b, nbuf, gkv),
                                            state_nsa_win_v.reshape(b, nbuf, gkv), o_cmp, o_slc, tok(gates))
    xs = _oproj(os_.reshape(rs, dq), xs, wout, gpost, tm=rs)
    nsa_s = (kc, vc, ks, vs)
    xp, xs = ffn_pair(xp, xs, 1, 2, ffn2_w_in, ffn2_w_out)

    keep_p = min(WINDOW, t)
    fox_shape = lambda a, m, r: a.reshape(1, m, r, FOX_HEADS, HEAD_DIM)
    nsa_shape = lambda a, m, r: a.reshape(1, m, r, NSA_KV_HEADS, HEAD_DIM)

    def token_major(a, heads):
        return jnp.transpose(a.reshape(1, n, heads, HEAD_DIM, a.shape[2]), (0, 1, 4, 2, 3))

    win_p = lambda a: token_major(a[:, :, t - keep_p:], NSA_KV_HEADS)
    return (xp.reshape(n, t, d), xs.reshape(b, nq, d),
            token_major(fox_kt_p, FOX_HEADS), token_major(fox_vt_p, FOX_HEADS), fox_lf_p.reshape(1, n, t, FOX_HEADS),
            fox_shape(fox_k_s, b, nq), fox_shape(fox_v_s, b, nq), fox_lf_s.reshape(1, b, nq, FOX_HEADS),
            token_major(nsa_t_p[0], NSA_KV_HEADS), token_major(nsa_t_p[1], NSA_KV_HEADS),
            token_major(nsa_t_p[2], NSA_KV_HEADS), token_major(nsa_t_p[3], NSA_KV_HEADS),
            win_p(nsa_t_p[4]), win_p(nsa_t_p[5]),
            nsa_shape(nsa_s[0], b, nq), nsa_shape(nsa_s[1], b, nq), nsa_shape(nsa_s[2], b, nq), nsa_shape(nsa_s[3], b, nq),
            nsa_shape(win_k_s, b, nbuf), nsa_shape(win_v_s, b, nbuf))
```

```python
import functools

import jax
import jax.numpy as jnp
from jax import lax
from jax.experimental import pallas as pl
from jax.experimental.pallas import tpu as pltpu

F32 = jnp.float32
BF16 = jnp.bfloat16

HEAD_DIM = 64
FOX_HEADS = 16
NSA_HEADS = 16
NSA_KV_HEADS = 4
NSA_GROUP = NSA_HEADS // NSA_KV_HEADS
CMP_LEN = 32
CMP_STRIDE = 16
SLC_BLOCK = 64
SLC_TOP = 16
N_LOCAL_BLOCKS = 2
FORCE_SCORE = 1e4
WINDOW = 512
ROPE_THETA = 10000.0
NORM_EPS = 1e-6
NEG_INF = -1e30
MASKED = 2 * NEG_INF
SCALE = HEAD_DIM ** -0.5
LOG2E = 1.4426950408889634
SCALE2 = SCALE * LOG2E
PAGE_SIZE = 128

LANES = 128
SUBLANES = 8
VMEM_LIMIT = 56 * 1024 * 1024

PAGES_PER_STEP = 16
NSA_PAGES_PER_STEP = 16
SLC_PAGES_PER_STEP = 32
CHUNKS_PER_PAGE = PAGE_SIZE // CMP_STRIDE
ROW_BLOCK = 128


def _params(*sem):
    return pltpu.CompilerParams(dimension_semantics=sem, vmem_limit_bytes=VMEM_LIMIT)


def _dot(a, b):
    return jnp.dot(a, b, preferred_element_type=F32)


def _dot_nt(a, b):
    return lax.dot_general(a, b, (((1,), (1,)), ((), ())), preferred_element_type=F32)


def _split3(x):
    hi = x.astype(BF16)
    r1 = x - hi.astype(F32)
    mid = r1.astype(BF16)
    lo = (r1 - mid.astype(F32)).astype(BF16)
    return hi, mid, lo


def _dot3_l(a, b):
    hi, mid, lo = _split3(a)
    return _dot(hi, b) + _dot(mid, b) + _dot(lo, b)


def _dot3_r(a, b):
    hi, mid, lo = _split3(b)
    return _dot(a, hi) + _dot(a, mid) + _dot(a, lo)


def _dot3_nt_r(a, b):
    hi, mid, lo = _split3(b)
    return _dot_nt(a, hi) + _dot_nt(a, mid) + _dot_nt(a, lo)


def _rms(x, g):
    return x * lax.rsqrt(jnp.mean(x * x, axis=-1, keepdims=True) + NORM_EPS) * g


def _silu(x):
    return x * jax.nn.sigmoid(x)


def _iota(shape, dim):
    return lax.broadcasted_iota(jnp.int32, shape, dim)


def _masked_softmax2(s, mask):
    s = jnp.where(mask, s, NEG_INF)
    m = jnp.max(s, axis=-1, keepdims=True)
    p = jnp.where(mask, jnp.exp2(s - m), 0.0)
    return p / jnp.maximum(jnp.sum(p, axis=-1, keepdims=True), 1e-30)


def _init_stats(m_ref, l_ref, acc_ref):
    m_ref[...] = jnp.full(m_ref.shape, NEG_INF, F32)
    l_ref[...] = jnp.zeros(l_ref.shape, F32)
    acc_ref[...] = jnp.zeros(acc_ref.shape, F32)


def _softmax_step(s, m_ref, l_ref, p_dtype=BF16):
    r, k = s.shape
    ps, alphas = [], []
    for r0 in range(0, r, ROW_BLOCK):
        r1 = min(r0 + ROW_BLOCK, r)
        cols = [s[r0:r1, c:c + LANES] for c in range(0, k, LANES)]
        cmax = cols[0]
        for col in cols[1:]:
            cmax = jnp.maximum(cmax, col)
        m_old = m_ref[r0:r1, :]
        m_new = jnp.maximum(m_old, jnp.max(cmax, axis=1, keepdims=True))
        alpha = jnp.exp2(m_old - m_new)
        pcs = [jnp.exp2(col - m_new) for col in cols]
        lsum = pcs[0]
        for pc in pcs[1:]:
            lsum = lsum + pc
        l_ref[r0:r1, :] = alpha * l_ref[r0:r1, :] + lsum
        m_ref[r0:r1, :] = m_new
        ps.append(jnp.concatenate([pc.astype(p_dtype) for pc in pcs], axis=1))
        alphas.append(alpha)
    return jnp.concatenate(ps, axis=0), jnp.concatenate(alphas, axis=0)


def _normalize(acc, l_ref):
    return acc / jnp.maximum(jnp.sum(l_ref[...], axis=1, keepdims=True), 1e-30)


def _top_mask(score, n_real, n_top, axis):
    blk = _iota(score.shape, axis)
    rank = jnp.zeros(score.shape, F32)
    for i in range(n_real):
        one = score[:, i:i + 1] if axis == 1 else score[i:i + 1, :]
        tie = jnp.where(blk > i, 1.0, 0.0)
        rank = rank + jnp.where(one > score, 1.0, jnp.where(one == score, tie, 0.0))
    return jnp.where((rank < n_top) & (blk < n_real), 1.0, 0.0)


def _cover_matrix(shape, cmp_axis, n_cmp, n_sel):
    ci = _iota(shape, cmp_axis)
    sj = _iota(shape, 1 - cmp_axis)
    hit = ((ci * CMP_STRIDE < (sj + 1) * SLC_BLOCK) & (ci * CMP_STRIDE + CMP_LEN > sj * SLC_BLOCK)
           & (ci < n_cmp) & (sj < n_sel))
    return jnp.where(hit, 1.0, 0.0).astype(BF16)


def _selection_scores(imp, pos, n_sel, axis):
    sj = _iota(imp.shape, axis)
    blk_t = pos // SLC_BLOCK
    valid = (sj <= blk_t) & (sj < n_sel)
    forced = (sj == 0) | (blk_t - sj < N_LOCAL_BLOCKS)
    return jnp.where(valid, jnp.where(forced, FORCE_SCORE, imp), -jnp.inf)


def _pad_rows(x, rows):
    rid = _iota((rows, x.shape[1]), 0)
    out = jnp.zeros((rows, x.shape[1]), x.dtype)
    for r in range(x.shape[0]):
        out = jnp.where(rid == r, jnp.broadcast_to(x[r:r + 1], out.shape), out)
    return out


def _to_half(x, src, dst):
    lane_half = _iota(x.shape, 1) // HEAD_DIM
    if isinstance(src, int) and isinstance(dst, int):
        moved = x if src == dst else pltpu.roll(x, HEAD_DIM, 1)
    else:
        moved = jnp.where(src == dst, x, pltpu.roll(x, HEAD_DIM, 1))
    return jnp.where(lane_half == dst, moved, 0.0)


def _head_block(x, h):
    return x[:, (h // 2) * LANES:(h // 2 + 1) * LANES]


def _assemble_heads(pieces):
    return jnp.concatenate([pieces[2 * e] + pieces[2 * e + 1] for e in range(len(pieces) // 2)], axis=1)


def _ffn_kernel(x_ref, gpre_ref, gpost_ref, wg_ref, wu_ref, wo_ref, o_ref, h_ref, acc_ref):
    f = pl.program_id(1)

    @pl.when(f == 0)
    def _():
        h_ref[...] = _rms(x_ref[...], gpre_ref[...]).astype(BF16)
        acc_ref[...] = jnp.zeros_like(acc_ref)

    h = h_ref[...]
    gate = _dot(h, wg_ref[...])
    up = _dot(h, wu_ref[...])
    acc_ref[...] += _dot((_silu(gate) * up).astype(BF16), wo_ref[...])

    @pl.when(f == pl.num_programs(1) - 1)
    def _():
        o_ref[...] = x_ref[...] + 0.5 * _rms(acc_ref[...], gpost_ref[...])


def _ffn(x, g_pre, g_post, w_in, w_out, *, tm, tf=256):
    rows, d = x.shape
    d_ff = w_out.shape[0]
    nf = d_ff // tf
    assert rows % tm == 0 and d_ff % tf == 0
    return pl.pallas_call(
        _ffn_kernel,
        grid=(rows // tm, nf),
        in_specs=[
            pl.BlockSpec((tm, d), lambda i, f: (i, 0)),
            pl.BlockSpec((1, d), lambda i, f: (0, 0)),
            pl.BlockSpec((1, d), lambda i, f: (0, 0)),
            pl.BlockSpec((d, tf), lambda i, f: (0, f)),
            pl.BlockSpec((d, tf), lambda i, f: (0, nf + f)),
            pl.BlockSpec((tf, d), lambda i, f: (f, 0)),
        ],
        out_specs=pl.BlockSpec((tm, d), lambda i, f: (i, 0)),
        out_shape=jax.ShapeDtypeStruct((rows, d), F32),
        scratch_shapes=[pltpu.VMEM((tm, d), BF16), pltpu.VMEM((tm, d), F32)],
        compiler_params=_params("parallel", "arbitrary"),
        name="ffn",
    )(x, g_pre, g_post, w_in, w_in, w_out)


def _oproj_kernel(o_ref, x_ref, w_ref, g_ref, out_ref):
    y = _dot(o_ref[...].astype(BF16), w_ref[...])
    out_ref[...] = x_ref[...] + _rms(y, g_ref[...])


def _oproj(o, x, w, g, *, tm):
    rows, d = x.shape
    return pl.pallas_call(
        _oproj_kernel,
        grid=(rows // tm,),
        in_specs=[
            pl.BlockSpec((tm, o.shape[1]), lambda i: (i, 0)),
            pl.BlockSpec((tm, d), lambda i: (i, 0)),
            pl.BlockSpec(w.shape, lambda i: (0, 0)),
            pl.BlockSpec((1, d), lambda i: (0, 0)),
        ],
        out_specs=pl.BlockSpec((tm, d), lambda i: (i, 0)),
        out_shape=jax.ShapeDtypeStruct((rows, d), F32),
        compiler_params=_params("parallel"),
        name="oproj",
    )(o, x, w, g)


def _log_sigmoid(z):
    return jnp.minimum(z, 0.0) - jnp.log1p(jnp.exp(-jnp.abs(z)))


def _seq_specs(rows, width, tm, seq_len):
    tiles = seq_len // tm
    return (pl.BlockSpec((1, width, tm), lambda i: (i // tiles, 0, i % tiles)),
            jax.ShapeDtypeStruct((rows // seq_len, width, seq_len), F32))


def _fox_proj_kernel(x_ref, g_ref, wqkv_ref, wf_ref, bf_ref, q_ref, k_ref, v_ref, lf_ref, *copies, channel_major):
    h = _rms(x_ref[...], g_ref[...]).astype(BF16)
    d = q_ref.shape[1]
    qkv = _dot(h, wqkv_ref[...])
    q_ref[...] = qkv[:, :d]
    k, v = qkv[:, d:2 * d], qkv[:, 2 * d:]
    if channel_major:
        k_ref[0] = k.T
        v_ref[0] = v.T
        copies[0][0] = k.T.astype(BF16)
        copies[1][...] = v.astype(BF16)
    else:
        k_ref[...] = k
        v_ref[...] = v
    fl = _dot(h, wf_ref[...])
    lf_ref[...] = _log_sigmoid(fl[:, :FOX_HEADS] + bf_ref[...])


def _fox_proj(x, g, wqkv, wf, bf, *, tm, seq_len=None):
    rows, d = x.shape
    row_spec = pl.BlockSpec((tm, d), lambda i: (i, 0))
    full = lambda a: pl.BlockSpec(a.shape, lambda i: (0, 0))
    row_sds = jax.ShapeDtypeStruct((rows, d), F32)
    if seq_len is None:
        kv_specs, kv_shapes = [row_spec] * 2, [row_sds] * 2
        extra_specs, extra_shapes = [], []
    else:
        spec, sds = _seq_specs(rows, d, tm, seq_len)
        kv_specs, kv_shapes = [spec] * 2, [sds] * 2
        extra_specs = [spec, row_spec]
        extra_shapes = [jax.ShapeDtypeStruct(sds.shape, BF16), jax.ShapeDtypeStruct((rows, d), BF16)]
    return pl.pallas_call(
        functools.partial(_fox_proj_kernel, channel_major=seq_len is not None),
        grid=(rows // tm,),
        in_specs=[row_spec, full(g), full(wqkv), full(wf), full(bf)],
        out_specs=[row_spec] + kv_specs + [pl.BlockSpec((tm, FOX_HEADS), lambda i: (i, 0))] + extra_specs,
        out_shape=[row_sds] + kv_shapes + [jax.ShapeDtypeStruct((rows, FOX_HEADS), F32)] + extra_shapes,
        compiler_params=_params("parallel"),
        name="fox_proj",
    )(x, g, wqkv, wf, bf)


def _rope_block(x, cos, sin_signed):
    lane = _iota(x.shape, 1)
    first_half = (lane % HEAD_DIM) < (HEAD_DIM // 2)
    rot = jnp.where(first_half, pltpu.roll(x, LANES - HEAD_DIM // 2, 1), pltpu.roll(x, HEAD_DIM // 2, 1))
    return x * cos + rot * sin_signed


def _nsa_proj_kernel(x_ref, g_ref, wr_ref, wp_ref, wg_ref, cos_ref, sin_ref, q_ref, gate_ref, *kv_refs, channel_major):
    h = _rms(x_ref[...], g_ref[...]).astype(BF16)
    cos = cos_ref[...]
    sin = sin_ref[...]
    r = _dot(h, wr_ref[...])
    nq = q_ref.shape[1]
    nkv = NSA_KV_HEADS * HEAD_DIM
    roped = [_rope_block(r[:, b * LANES:(b + 1) * LANES], cos, sin) for b in range(r.shape[1] // LANES)]
    nqb, nkb = nq // LANES, nkv // LANES
    q_ref[...] = jnp.concatenate(roped[:nqb], axis=1)
    kc, ks, kw = (jnp.concatenate(roped[nqb + e * nkb:nqb + (e + 1) * nkb], axis=1) for e in range(3))
    p = _dot(h, wp_ref[...])
    vc, vs, vw = (p[:, e * nkv:(e + 1) * nkv] for e in range(3))
    kv = (kc, vc, ks, vs, kw, vw)
    if channel_major:
        for ref, a in zip(kv_refs[:6], kv):
            ref[0] = a.T
        kv_refs[6][...] = kc
        kv_refs[7][...] = vc
        for ref, a in zip(kv_refs[8:], kv[2:]):
            ref[...] = a.astype(BF16)
    else:
        for ref, a in zip(kv_refs, kv):
            ref[...] = a
    gl = _dot(h, wg_ref[...])
    gate_ref[...] = jax.nn.sigmoid(gl[:, :gate_ref.shape[1]])


def _nsa_proj(x, g, wr, wp, wg, cos, sin, *, tm, seq_len=None):
    rows, d = x.shape
    nq = NSA_HEADS * HEAD_DIM
    nkv = NSA_KV_HEADS * HEAD_DIM
    ngate = 3 * NSA_HEADS
    n_tab = cos.shape[0] // tm
    row = lambda c: pl.BlockSpec((tm, c), lambda i: (i, 0))
    full = lambda a: pl.BlockSpec(a.shape, lambda i: (0, 0))
    tab = pl.BlockSpec((tm, LANES), lambda i: (i % n_tab, 0))
    sds = lambda c, dt=F32: jax.ShapeDtypeStruct((rows, c), dt)
    if seq_len is None:
        kv_specs, kv_shapes = [row(nkv)] * 6, [sds(nkv)] * 6
    else:
        spec, shape = _seq_specs(rows, nkv, tm, seq_len)
        kv_specs = [spec] * 6 + [row(nkv)] * 6
        kv_shapes = [shape] * 6 + [sds(nkv)] * 2 + [sds(nkv, BF16)] * 4
    return pl.pallas_call(
        functools.partial(_nsa_proj_kernel, channel_major=seq_len is not None),
        grid=(rows // tm,),
        in_specs=[row(d), full(g), full(wr), full(wp), full(wg), tab, tab],
        out_specs=[row(nq), row(ngate)] + kv_specs,
        out_shape=[sds(nq), sds(ngate)] + kv_shapes,
        compiler_params=_params("parallel"),
        name="nsa_proj",
    )(x, g, wr, wp, wg, cos, sin)


def _cumsum_kernel(lf_ref, ccol_ref, kaug_ref, *, blk):
    t = lf_ref.shape[1]
    lower = jnp.where(_iota((blk, blk), 1) <= _iota((blk, blk), 0), 1.0, 0.0).astype(BF16)
    carry = jnp.zeros((1, lf_ref.shape[2]), F32)
    for b in range(t // blk):
        cs = _dot3_r(lower, lf_ref[0, b * blk:(b + 1) * blk, :]) + carry
        ccol_ref[0, b * blk:(b + 1) * blk, :] = cs
        carry = cs[blk - 1:blk, :]
    nh = lf_ref.shape[2]
    width = kaug_ref.shape[1]
    col = _iota((nh, width), 1)
    head = _iota((nh, width), 0)
    target = LANES * (head // 2) + _bias_lane(head % 2, 1)
    pieces = _split3(-LOG2E * ccol_ref[0])
    kaug = jnp.where(_iota((t, width), 1) % LANES % BIAS_LANES_PER_HEAD < 3, 1.0, 0.0)
    kaug = jnp.where(_iota((t, width), 1) % LANES < 2 * BIAS_LANES_PER_HEAD, kaug, 0.0)
    for e, piece in enumerate(pieces):
        kaug = kaug + _dot(piece, jnp.where(col == target + e, 1.0, 0.0).astype(BF16))
    kaug_ref[0] = kaug.T.astype(BF16)


BIAS_LANES_PER_HEAD = 8


def _bias_lane(head_in_pair, side):
    return BIAS_LANES_PER_HEAD * head_in_pair + 3 * side


def _cumsum(lf, nlanes):
    n, t, h = lf.shape
    return pl.pallas_call(
        functools.partial(_cumsum_kernel, blk=256),
        grid=(n,),
        in_specs=[pl.BlockSpec((1, t, h), lambda i: (i, 0, 0))],
        out_specs=[pl.BlockSpec((1, t, h), lambda i: (i, 0, 0)), pl.BlockSpec((1, nlanes, t), lambda i: (i, 0, 0))],
        out_shape=[jax.ShapeDtypeStruct((n, t, h), F32), jax.ShapeDtypeStruct((n, nlanes, t), BF16)],
        compiler_params=_params("parallel"),
        name="fox_cumsum",
    )(lf)


def _fox_attn_kernel(q_ref, k_ref, v_ref, ccol_ref, kaug_ref, o_ref, m_ref, l_ref, acc_ref, *, tq, tk):
    hp = pl.program_id(1)
    i = pl.program_id(2)
    rows = 2 * tq
    q = q_ref[0] * SCALE2
    lane = _iota((tq, LANES), 1)
    qs = jnp.concatenate([jnp.where(lane < HEAD_DIM, q, 0.0), jnp.where(lane >= HEAD_DIM, q, 0.0)], axis=0)
    nh = ccol_ref.shape[2]
    head = _iota((nh, LANES), 0)
    col = _iota((nh, LANES), 1)
    qaug = []
    for e in range(2):
        a = jnp.where((lane >= _bias_lane(e, 1)) & (lane < _bias_lane(e, 1) + 3), 1.0, 0.0)
        for pi, piece in enumerate(_split3(LOG2E * ccol_ref[0])):
            place = jnp.where((head == 2 * hp + e) & (col == _bias_lane(e, 0) + pi), 1.0, 0.0).astype(BF16)
            a = a + _dot(piece, place)
        qaug.append(a)
    qs = jnp.concatenate([qs, jnp.concatenate(qaug, axis=0)], axis=1).astype(BF16)
    lane_minus_row = _iota((rows, LANES), 1) - _iota((rows, LANES), 0) % tq
    _init_stats(m_ref, l_ref, acc_ref)

    def tile(j, causal):
        k0 = pl.multiple_of(j * tk, tk)
        kt = jnp.concatenate([k_ref[0, :, pl.ds(k0, tk)], kaug_ref[0, :, pl.ds(k0, tk)]], axis=0)
        vt = v_ref[0, pl.ds(k0, tk), :]
        s = _dot(qs, kt)
        if causal:
            s = jnp.concatenate([jnp.where(lane_minus_row <= i * tq - k0 - c, s[:, c:c + LANES], MASKED)
                                 for c in range(0, tk, LANES)], axis=1)
        p, alpha = _softmax_step(s, m_ref, l_ref)
        acc_ref[...] = alpha * acc_ref[...] + _dot(p, vt)

    def run(lo, hi, causal):
        def body(j, carry):
            tile(j, causal)
            return carry
        lax.fori_loop(lo, hi, body, 0)

    n_below = (i * tq) // tk
    run(0, n_below, False)
    run(n_below, (i * tq + tq + tk - 1) // tk, True)
    o = _normalize(acc_ref[...], l_ref)
    o_ref[0] = jnp.where(lane < HEAD_DIM, o[:tq], o[tq:])


def _fox_attn(q, k, v, ccol, kaug, *, tq=512, tk=512):
    n, t, d = q.shape
    npair = d // LANES
    rows = 2 * tq
    return pl.pallas_call(
        functools.partial(_fox_attn_kernel, tq=tq, tk=tk),
        grid=(n, npair, t // tq),
        in_specs=[
            pl.BlockSpec((1, tq, LANES), lambda b, h, i: (b, i, h)),
            pl.BlockSpec((1, LANES, t), lambda b, h, i: (b, h, 0)),
            pl.BlockSpec((1, t, LANES), lambda b, h, i: (b, 0, h)),
            pl.BlockSpec((1, tq, ccol.shape[2]), lambda b, h, i: (b, i, 0)),
            pl.BlockSpec((1, LANES, t), lambda b, h, i: (b, h, 0)),
        ],
        out_specs=pl.BlockSpec((1, tq, LANES), lambda b, h, i: (b, i, h)),
        out_shape=jax.ShapeDtypeStruct((n, t, d), F32),
        scratch_shapes=[pltpu.VMEM((rows, LANES), F32)] * 3,
        compiler_params=_params("parallel", "parallel", "arbitrary"),
        name="fox_attn",
    )(q, k, v, ccol, kaug)


def _fox_decode_kernel(pt_ref, q_ref, kn_ref, vn_ref, lfn_ref, *rest, nq, rows):
    npg = PAGES_PER_STEP
    k_refs, v_refs, lf_refs = rest[:npg], rest[npg:2 * npg], rest[2 * npg:3 * npg]
    o_ref = rest[3 * npg]
    qbd_ref, kcat_ref, vcat_ref, m_ref, l_ref, acc_ref, carry_ref, ncum_ref = rest[3 * npg + 1:]
    del pt_ref
    jj = pl.program_id(1)
    d = q_ref.shape[2]
    nh = FOX_HEADS
    pg = PAGE_SIZE
    r2 = _iota((rows, d), 0)
    block_diag = (r2 % nh == _iota((rows, d), 1) // HEAD_DIM) & (r2 < nq * nh)
    re = _iota((rows, nh), 0)
    head_of_row = jnp.where((re % nh == _iota((rows, nh), 1)) & (re < nq * nh), 1.0, 0.0).astype(BF16)

    @pl.when(jj == 0)
    def _():
        q = q_ref[0] * SCALE2
        pad = [jnp.zeros((rows - nq * nh, d), F32)] if rows > nq * nh else []
        qb = jnp.concatenate([jnp.broadcast_to(q[t:t + 1], (nh, d)) for t in range(nq)] + pad, axis=0)
        qbd = jnp.where(block_diag, qb, 0.0).astype(BF16)
        qbd_ref[...] = qbd
        kn = _pad_rows(kn_ref[0], pg).astype(BF16)
        vn = _pad_rows(vn_ref[0], pg).astype(BF16)
        sn = _dot_nt(qbd, kn)
        xn = _dot3_nt_r(head_of_row, _pad_rows(lfn_ref[0], pg)) * LOG2E
        t_r = _iota((rows, 1), 0) // nh
        cum = []
        c = jnp.zeros((rows, 1), F32)
        for s in range(nq):
            c = c + xn[:, s:s + 1]
            cum.append(c)
        lane = _iota((rows, pg), 1)
        ncum = cum[-1]
        cpre = jnp.broadcast_to(cum[-1], (rows, pg))
        for s in reversed(range(nq - 1)):
            ncum = jnp.where(t_r == s, cum[s], ncum)
            cpre = jnp.where(lane == s, cum[s], cpre)
        mask = (lane <= t_r) & (lane < nq)
        logit = jnp.where(mask, sn + ncum - cpre, MASKED)
        m = jnp.maximum(jnp.max(logit, axis=-1, keepdims=True), NEG_INF)
        p = jnp.exp2(logit - m)
        m_ref[...] = jnp.broadcast_to(m, m_ref.shape)
        l_ref[...] = p
        acc_ref[...] = _dot(p.astype(BF16), vn)
        carry_ref[...] = jnp.zeros(carry_ref.shape, F32)
        ncum_ref[...] = jnp.broadcast_to(ncum, ncum_ref.shape)

    for p in range(npg):
        kcat_ref[:, p * pg:(p + 1) * pg] = k_refs[p][...].astype(BF16)
        vcat_ref[:, p * pg:(p + 1) * pg] = v_refs[p][...].astype(BF16)
    s = _dot(qbd_ref[...], kcat_ref[...])
    lfcat = jnp.concatenate([lf_refs[p][...] for p in range(npg)], axis=1)
    x = _dot3_r(head_of_row, lfcat) * LOG2E
    y = jnp.concatenate([x[:, p * pg:(p + 1) * pg] for p in range(npg)], axis=0)
    ri, ci = _iota((pg, 2 * pg), 0), _iota((pg, 2 * pg), 1)
    later_and_total = jnp.where((ri >= ci) | (ci >= pg), 1.0, 0.0).astype(BF16)
    z = _dot3_l(y, later_and_total)
    c = carry_ref[...]
    ncum = ncum_ref[...]
    cols = [None] * npg
    for p in reversed(range(npg)):
        zp = z[p * rows:(p + 1) * rows]
        cols[p] = s[:, p * pg:(p + 1) * pg] + (zp[:, :pg] - y[p * rows:(p + 1) * rows] + c + ncum)
        c = c + zp[:, pg:]
    carry_ref[...] = c
    p, alpha = _softmax_step(jnp.concatenate(cols, axis=1), m_ref, l_ref)
    pv = _dot_nt(p, vcat_ref[...])
    for b in range(0, d, LANES):
        acc_ref[:, b:b + LANES] = alpha * acc_ref[:, b:b + LANES] + pv[:, b:b + LANES]

    @pl.when(jj == pl.num_programs(1) - 1)
    def _():
        o = jnp.where(block_diag, _normalize(acc_ref[...], l_ref), 0.0)
        for t in range(nq):
            o_ref[0, t:t + 1, :] = jnp.sum(o[t * nh:(t + 1) * nh], axis=0, keepdims=True)


def _fox_decode(page_table, q, k_new, v_new, lf_new, cache_kt, cache_vt, cache_lft, layer):
    b, nq, d = q.shape
    n_pages = page_table.shape[1]
    npg = PAGES_PER_STEP
    assert n_pages % npg == 0
    rows = -(-nq * FOX_HEADS // (2 * SUBLANES)) * (2 * SUBLANES)

    def page_spec(height, p):
        return pl.BlockSpec((None, None, height, PAGE_SIZE),
                            lambda i, jj, pt: (layer, pt[i, n_pages - npg * (jj + 1) + p], 0, 0))

    tok = lambda w: pl.BlockSpec((1, nq, w), lambda i, jj, pt: (i, 0, 0))
    grid_spec = pltpu.PrefetchScalarGridSpec(
        num_scalar_prefetch=1,
        grid=(b, n_pages // npg),
        in_specs=[tok(d), tok(d), tok(d), tok(FOX_HEADS)]
        + [page_spec(d, p) for p in range(npg)] * 2 + [page_spec(FOX_HEADS, p) for p in range(npg)],
        out_specs=pl.BlockSpec((1, nq, d), lambda i, jj, pt: (i, 0, 0)),
        scratch_shapes=[
            pltpu.VMEM((rows, d), BF16),
            pltpu.VMEM((d, npg * PAGE_SIZE), BF16),
            pltpu.VMEM((d, npg * PAGE_SIZE), BF16),
            pltpu.VMEM((rows, LANES), F32),
            pltpu.VMEM((rows, LANES), F32),
            pltpu.VMEM((rows, d), F32),
            pltpu.VMEM((rows, LANES), F32),
            pltpu.VMEM((rows, LANES), F32),
        ],
    )
    return pl.pallas_call(
        functools.partial(_fox_decode_kernel, nq=nq, rows=rows),
        grid_spec=grid_spec,
        out_shape=jax.ShapeDtypeStruct((b, nq, d), F32),
        compiler_params=_params("parallel", "arbitrary"),
        name="fox_decode",
    )(page_table, q, k_new, v_new, lf_new, *([cache_kt] * npg), *([cache_vt] * npg), *([cache_lft] * npg))


def _compress_finish(a, b, w2):
    n = a.shape[0]
    hid = _silu(a + pltpu.roll(b, n - 1, 0))
    out = _dot(hid.astype(BF16), w2)
    return jnp.where(_iota(out.shape, 0) < n - 1, out, 0.0)


def _compress_rows(read_rows, n_chunk, pe_ref, w1_ref, w2_ref):
    half = CMP_STRIDE
    a = jnp.zeros((n_chunk, w2_ref.shape[1]), F32)
    b = jnp.zeros((n_chunk, w2_ref.shape[1]), F32)
    for j in range(half):
        xj = read_rows(j)
        a = a + _dot((xj + pe_ref[j:j + 1, :]).astype(BF16), w1_ref[j])
        b = b + _dot((xj + pe_ref[half + j:half + j + 1, :]).astype(BF16), w1_ref[half + j])
    return _compress_finish(a, b, w2_ref[...])


def _compress_kernel(*refs, nblk):
    x_refs = refs[:nblk]
    pe_ref, w1_ref, w2_ref, o_ref = refs[nblk:]
    n_chunk = o_ref.shape[1]
    read = lambda j: jnp.concatenate([r[0, pl.ds(j, n_chunk, stride=CMP_STRIDE), :] for r in x_refs], axis=1)
    o_ref[0] = _compress_rows(read, n_chunk, pe_ref, w1_ref, w2_ref)


def _compress(x, pe, w1, w2):
    n, t, gkv = x.shape
    assert CMP_LEN == 2 * CMP_STRIDE and t % CMP_STRIDE == 0
    c = t // CMP_STRIDE
    nblk = gkv // LANES
    full = lambda a: pl.BlockSpec(a.shape, lambda i: (0,) * a.ndim)
    lane_block = lambda e: pl.BlockSpec((1, t, LANES), lambda i: (i, 0, e))
    return pl.pallas_call(
        functools.partial(_compress_kernel, nblk=nblk),
        grid=(n,),
        in_specs=[lane_block(e) for e in range(nblk)] + [full(pe), full(w1), full(w2)],
        out_specs=pl.BlockSpec((1, c, gkv), lambda i: (i, 0, 0)),
        out_shape=jax.ShapeDtypeStruct((n, c, gkv), F32),
        compiler_params=_params("parallel"),
        name="nsa_compress",
    )(*([x] * nblk), pe, w1, w2)


def _nsa_attn_kernel(q_ref, kc_ref, vc_ref, ks_ref, vs_ref, kw_ref, vw_ref, gate_ref, blockmask_ref, o_ref,
                     m_ref, l_ref, acc_ref, *, tq, tk, n_cmp, n_sel):
    g = pl.program_id(1)
    i = pl.program_id(2)
    par = g % 2
    hp = NSA_GROUP
    rows = hp * tq
    q = q_ref[0] * SCALE2
    qs = jnp.concatenate([_to_half(_head_block(q, j), j % 2, par) for j in range(hp)], axis=0).astype(BF16)
    qpos = i * tq + _iota((rows, 1), 0) % tq
    row_minus_lane = _iota((rows, LANES), 0) % tq - _iota((rows, LANES), 1)

    ncp = kc_ref.shape[1]
    ci = _iota((1, ncp), 1)
    mask = (ci * CMP_STRIDE + CMP_LEN - 1 <= qpos) & (ci < n_cmp)
    p = _masked_softmax2(_dot_nt(qs, kc_ref[0].astype(BF16)), mask)
    o_cmp = _dot(p.astype(BF16), vc_ref[0].astype(BF16))
    psum = p[0:tq]
    for j in range(1, hp):
        psum = psum + p[j * tq:(j + 1) * tq]
    nsr = -(-n_sel // SUBLANES) * SUBLANES
    imp = _dot3_nt_r(_cover_matrix((nsr, ncp), 1, n_cmp, n_sel), psum)
    tpos = i * tq + _iota((1, tq), 1)
    sel_t = _top_mask(_selection_scores(imp, tpos, n_sel, 0), n_sel, min(SLC_TOP, n_sel), 0)
    unsel = jnp.concatenate([1.0 - sel_t, jnp.zeros((LANES - nsr, tq), F32)], axis=0).T.astype(BF16)
    qs_slc = jnp.concatenate([qs, jnp.concatenate([unsel] * hp, axis=0)], axis=1)

    def attend(lhs, kt, vt, transform):
        p, alpha = _softmax_step(transform(_dot_nt(lhs, kt)), m_ref, l_ref)
        acc_ref[...] = alpha * acc_ref[...] + _dot(p, vt)

    def loop(lo, hi, fn):
        def body(j, carry):
            fn(j)
            return carry
        lax.fori_loop(lo, hi, body, 0)

    def slc_tile(j, causal):
        k0 = pl.multiple_of(j * tk, tk)

        def transform(s):
            if not causal:
                return s
            cols = [jnp.where(row_minus_lane >= k0 + c - i * tq, s[:, c:c + LANES], MASKED)
                    for c in range(0, tk, LANES)]
            return jnp.concatenate(cols, axis=1)

        kt = jnp.concatenate([ks_ref[0, pl.ds(k0, tk), :], blockmask_ref[pl.ds(k0, tk), :]], axis=1)
        attend(qs_slc, kt, vs_ref[0, pl.ds(k0, tk), :], transform)

    _init_stats(m_ref, l_ref, acc_ref)
    n_below = (i * tq) // tk
    loop(0, n_below, lambda j: slc_tile(j, False))
    loop(n_below, (i * tq + tq + tk - 1) // tk, lambda j: slc_tile(j, True))
    o_slc = _normalize(acc_ref[...], l_ref)

    win_w = WINDOW + tq
    k0w = pl.multiple_of(jnp.maximum(i * tq - WINDOW, 0), tq)

    def win_transform(s):
        base = i * tq - k0w
        cols = []
        for c in range(0, win_w, LANES):
            sc = jnp.where(row_minus_lane >= c - base, s[:, c:c + LANES], MASKED)
            cols.append(jnp.where(row_minus_lane < WINDOW + c - base, sc, MASKED))
        return jnp.concatenate(cols, axis=1)

    _init_stats(m_ref, l_ref, acc_ref)
    attend(qs, kw_ref[0, pl.ds(k0w, win_w), :], vw_ref[0, pl.ds(k0w, win_w), :], win_transform)
    o_win = _normalize(acc_ref[...], l_ref)

    gates = gate_ref[0]
    gl = _iota(gates.shape, 1)
    outs = []
    for j in range(hp):
        head = g * hp + j
        gc = [jnp.sum(jnp.where(gl == c * NSA_HEADS + head, gates, 0.0), axis=1, keepdims=True) for c in range(3)]
        sl = slice(j * tq, (j + 1) * tq)
        outs.append(_to_half(gc[0] * o_cmp[sl] + gc[1] * o_slc[sl] + gc[2] * o_win[sl], par, j % 2))
    o_ref[0] = jnp.concatenate([outs[2 * e] + outs[2 * e + 1] for e in range(hp // 2)], axis=1)


def _nsa_attn(q, kcmp, vcmp, ks, vs, kw, vw, gates, *, tq=512, tk=512):
    n, t, _ = q.shape
    gw = NSA_GROUP * HEAD_DIM
    ncp = kcmp.shape[1]
    n_cmp = t // CMP_STRIDE - CMP_LEN // CMP_STRIDE + 1
    n_sel = -(-t // SLC_BLOCK)
    assert n_sel <= LANES and t % tq == 0 and t % tk == 0 and WINDOW % tq == 0 and t >= WINDOW + tq
    rows = NSA_GROUP * tq
    blockmask = jnp.where(jnp.arange(t, dtype=jnp.int32)[:, None] // SLC_BLOCK
                          == jnp.arange(LANES, dtype=jnp.int32)[None, :], MASKED, 0.0).astype(BF16)
    pair = lambda r: pl.BlockSpec((1, r, LANES), lambda b, g, i: (b, 0, g // 2))
    return pl.pallas_call(
        functools.partial(_nsa_attn_kernel, tq=tq, tk=tk, n_cmp=n_cmp, n_sel=n_sel),
        grid=(n, NSA_KV_HEADS, t // tq),
        in_specs=[pl.BlockSpec((1, tq, gw), lambda b, g, i: (b, i, g)), pair(ncp), pair(ncp),
                  pair(t), pair(t), pair(t), pair(t),
                  pl.BlockSpec((1, tq, gates.shape[2]), lambda b, g, i: (b, i, 0)),
                  pl.BlockSpec(blockmask.shape, lambda b, g, i: (0, 0))],
        out_specs=pl.BlockSpec((1, tq, gw), lambda b, g, i: (b, i, g)),
        out_shape=jax.ShapeDtypeStruct(q.shape, F32),
        scratch_shapes=[pltpu.VMEM((rows, LANES), F32)] * 3,
        compiler_params=_params("parallel", "parallel", "arbitrary"),
        name="nsa_attn",
    )(q, kcmp, vcmp, ks, vs, kw, vw, gates, blockmask)


def _group_queries(q8, g, par):
    return jnp.concatenate([_to_half(_head_block(q8, h), h % 2, par)
                            for h in range(g * NSA_GROUP, (g + 1) * NSA_GROUP)], axis=0).astype(BF16)


def _nsa_decode_cmp_kernel(pt_ref, q_ref, *rest, past_len, n_cmp, n_sel):
    npg = NSA_PAGES_PER_STEP
    kpages, vpages = rest[:npg], rest[npg:2 * npg]
    pek_ref, w1k_ref, w2k_ref, pev_ref, w1v_ref, w2v_ref = rest[2 * npg:2 * npg + 6]
    o_ref, sel_ref = rest[2 * npg + 6:2 * npg + 8]
    xk_ref, xv_ref = rest[2 * npg + 8:]
    del pt_ref
    jj = pl.program_id(1)
    pg = PAGE_SIZE
    r0 = jj * (npg * pg)
    nblk = xk_ref.shape[0]
    for p in range(npg):
        dst = pl.ds(pl.multiple_of(r0 + p * pg, pg), pg)
        xk = kpages[p][...].T
        xv = vpages[p][...].T
        for e in range(nblk):
            xk_ref[e, dst, :] = xk[:, e * LANES:(e + 1) * LANES]
            xv_ref[e, dst, :] = xv[:, e * LANES:(e + 1) * LANES]

    @pl.when(jj == pl.num_programs(1) - 1)
    def _():
        n_chunk = xk_ref.shape[1] // CMP_STRIDE

        def compress(x_ref, pe_ref, w1_ref, w2_ref):
            read = lambda j: jnp.concatenate(
                [x_ref[e, pl.ds(j, n_chunk, stride=CMP_STRIDE), :] for e in range(nblk)], axis=1)
            return _compress_rows(read, n_chunk, pe_ref, w1_ref, w2_ref)

        kcmp = compress(xk_ref, pek_ref, w1k_ref, w2k_ref)
        vcmp = compress(xv_ref, pev_ref, w1v_ref, w2v_ref)
        ncp = kcmp.shape[0]
        q8 = _pad_rows(q_ref[0] * SCALE2, SUBLANES)
        ci = _iota((1, ncp), 1)
        cover = _cover_matrix((ncp, sel_ref.shape[3]), 0, n_cmp, n_sel)
        pos8 = past_len + _iota((SUBLANES, 1), 0)
        pieces = [None] * NSA_HEADS
        for g in range(NSA_KV_HEADS):
            par = g % 2
            lanes = slice((g // 2) * LANES, (g // 2 + 1) * LANES)
            qs = _group_queries(q8, g, par)
            pos = past_len + _iota((qs.shape[0], 1), 0) % SUBLANES
            mask = (ci * CMP_STRIDE + CMP_LEN - 1 <= pos) & (ci < n_cmp)
            p = _masked_softmax2(_dot_nt(qs, kcmp[:, lanes].astype(BF16)), mask)
            o = _dot(p.astype(BF16), vcmp[:, lanes].astype(BF16))
            psum = p[0:SUBLANES]
            for j in range(1, NSA_GROUP):
                psum = psum + p[j * SUBLANES:(j + 1) * SUBLANES]
            imp = _dot3_l(psum, cover)
            sel_ref[0, g] = _top_mask(_selection_scores(imp, pos8, n_sel, 1), n_sel, min(SLC_TOP, n_sel), 1)
            for j in range(NSA_GROUP):
                h = g * NSA_GROUP + j
                pieces[h] = _to_half(o[j * SUBLANES:(j + 1) * SUBLANES], par, h % 2)
        o_ref[0] = _assemble_heads(pieces)


def _nsa_decode_cmp(page_table, q, cache_kt, cache_vt, wk, wv, layer, *, nsp):
    b, nq, dq = q.shape
    n_pages = page_table.shape[1]
    npg = NSA_PAGES_PER_STEP
    past_len = n_pages * PAGE_SIZE
    assert n_pages % npg == 0 and nq <= SUBLANES and nq < CMP_STRIDE and CMP_LEN == 2 * CMP_STRIDE
    n_chunk = (past_len + nq) // CMP_STRIDE
    assert n_chunk == n_pages * CHUNKS_PER_PAGE
    n_cmp = n_chunk - CMP_LEN // CMP_STRIDE + 1
    n_sel = -(-(past_len + nq) // SLC_BLOCK)
    assert n_sel <= nsp
    gkv = NSA_KV_HEADS * HEAD_DIM

    def page_spec(p):
        return pl.BlockSpec((None, None, gkv, PAGE_SIZE), lambda i, jj, pt: (layer, pt[i, npg * jj + p], 0, 0))

    full = lambda a: pl.BlockSpec(a.shape, lambda i, jj, pt: (0,) * a.ndim)
    grid_spec = pltpu.PrefetchScalarGridSpec(
        num_scalar_prefetch=1,
        grid=(b, n_pages // npg),
        in_specs=[pl.BlockSpec((1, nq, dq), lambda i, jj, pt: (i, 0, 0))]
        + [page_spec(p) for p in range(npg)] * 2 + [full(a) for a in wk] + [full(a) for a in wv],
        out_specs=[pl.BlockSpec((1, SUBLANES, dq), lambda i, jj, pt: (i, 0, 0)),
                   pl.BlockSpec((1, NSA_KV_HEADS, SUBLANES, nsp), lambda i, jj, pt: (i, 0, 0, 0))],
        scratch_shapes=[pltpu.VMEM((gkv // LANES, past_len, LANES), F32)] * 2,
    )
    return pl.pallas_call(
        functools.partial(_nsa_decode_cmp_kernel, past_len=past_len, n_cmp=n_cmp, n_sel=n_sel),
        grid_spec=grid_spec,
        out_shape=[jax.ShapeDtypeStruct((b, SUBLANES, dq), F32),
                   jax.ShapeDtypeStruct((b, NSA_KV_HEADS, SUBLANES, nsp), F32)],
        compiler_params=_params("parallel", "arbitrary"),
        name="nsa_decode_cmp",
    )(page_table, q, *([cache_kt] * npg), *([cache_vt] * npg), *wk, *wv)


def _nsa_decode_slc_kernel(pt_ref, q_ref, kn_ref, vn_ref, sel_ref, *rest, nq, past_len):
    npg = SLC_PAGES_PER_STEP
    kpages, vpages = rest[:npg], rest[npg:2 * npg]
    o_ref = rest[2 * npg]
    qs_ref, kcat_ref, vcat_ref, m_ref, l_ref, acc_ref = rest[2 * npg + 1:]
    del pt_ref
    jj = pl.program_id(1)
    pg = PAGE_SIZE
    npair = NSA_KV_HEADS // 2
    nsp = sel_ref.shape[3]
    selrows = [jnp.concatenate([sel_ref[0, 2 * gp + gi] for gi in range(2) for _ in range(NSA_GROUP)], axis=0)
               for gp in range(npair)]
    rows = selrows[0].shape[0]

    @pl.when(jj == 0)
    def _():
        q8 = _pad_rows(q_ref[0] * SCALE2, SUBLANES)
        kn = _pad_rows(kn_ref[0], pg).astype(BF16)
        vn = _pad_rows(vn_ref[0], pg).astype(BF16)
        t_r = _iota((rows, 1), 0) % SUBLANES
        lane = _iota((rows, pg), 1)
        for gp in range(npair):
            lanes = slice(gp * LANES, (gp + 1) * LANES)
            qs = jnp.concatenate([_group_queries(q8, 2 * gp + gi, gi) for gi in range(2)], axis=0)
            qs_ref[gp] = qs
            _init_stats(m_ref.at[gp], l_ref.at[gp], acc_ref.at[gp])
            new_block = selrows[gp][:, past_len // SLC_BLOCK:past_len // SLC_BLOCK + 1]
            mask = (lane <= t_r) & (lane < nq) & (new_block > 0.5)
            s = jnp.where(mask, _dot_nt(qs, kn[:, lanes]), MASKED)
            p, alpha = _softmax_step(s, m_ref.at[gp], l_ref.at[gp])
            acc_ref[gp] = alpha * acc_ref[gp] + _dot(p, vn[:, lanes])

    for p in range(npg):
        kcat_ref[:, p * pg:(p + 1) * pg] = kpages[p][...].astype(BF16)
        vcat_ref[:, p * pg:(p + 1) * pg] = vpages[p][...].astype(BF16)
    width = npg * pg
    block_of_key = jj * (width // SLC_BLOCK) + _iota((nsp, width), 1) // SLC_BLOCK
    expand = jnp.where(block_of_key == _iota((nsp, width), 0), 1.0, 0.0).astype(BF16)
    for gp in range(npair):
        rws = slice(gp * LANES, (gp + 1) * LANES)
        selx = _dot(selrows[gp].astype(BF16), expand)
        s = jnp.where(selx > 0.5, _dot(qs_ref[gp], kcat_ref[rws, :]), MASKED)
        p, alpha = _softmax_step(s, m_ref.at[gp], l_ref.at[gp])
        acc_ref[gp] = alpha * acc_ref[gp] + _dot_nt(p, vcat_ref[rws, :])

    @pl.when(jj == pl.num_programs(1) - 1)
    def _():
        pieces = [None] * NSA_HEADS
        for gp in range(npair):
            o = _normalize(acc_ref[gp], l_ref.at[gp])
            for gi in range(2):
                for j in range(NSA_GROUP):
                    h = (2 * gp + gi) * NSA_GROUP + j
                    r0 = (gi * NSA_GROUP + j) * SUBLANES
                    pieces[h] = _to_half(o[r0:r0 + SUBLANES], gi, h % 2)
        o_ref[0] = _assemble_heads(pieces)


def _nsa_decode_slc(page_table, q, k_new, v_new, sel, cache_kt, cache_vt, layer):
    b, nq, dq = q.shape
    gkv = k_new.shape[2]
    n_pages = page_table.shape[1]
    npg = SLC_PAGES_PER_STEP
    past_len = n_pages * PAGE_SIZE
    assert n_pages % npg == 0 and past_len % SLC_BLOCK == 0 and nq <= SLC_BLOCK
    npair = NSA_KV_HEADS // 2
    rows = 2 * NSA_GROUP * SUBLANES

    def page_spec(p):
        return pl.BlockSpec((None, None, gkv, PAGE_SIZE), lambda i, jj, pt: (layer, pt[i, npg * jj + p], 0, 0))

    tok = lambda w: pl.BlockSpec((1, nq, w), lambda i, jj, pt: (i, 0, 0))
    grid_spec = pltpu.PrefetchScalarGridSpec(
        num_scalar_prefetch=1,
        grid=(b, n_pages // npg),
        in_specs=[tok(dq), tok(gkv), tok(gkv),
                  pl.BlockSpec((1,) + sel.shape[1:], lambda i, jj, pt: (i, 0, 0, 0))]
        + [page_spec(p) for p in range(npg)] * 2,
        out_specs=pl.BlockSpec((1, SUBLANES, dq), lambda i, jj, pt: (i, 0, 0)),
        scratch_shapes=[
            pltpu.VMEM((npair, rows, LANES), BF16),
            pltpu.VMEM((gkv, npg * PAGE_SIZE), BF16),
            pltpu.VMEM((gkv, npg * PAGE_SIZE), BF16),
            pltpu.VMEM((npair, rows, LANES), F32),
            pltpu.VMEM((npair, rows, LANES), F32),
            pltpu.VMEM((npair, rows, LANES), F32),
        ],
    )
    return pl.pallas_call(
        functools.partial(_nsa_decode_slc_kernel, nq=nq, past_len=past_len),
        grid_spec=grid_spec,
        out_shape=jax.ShapeDtypeStruct((b, SUBLANES, dq), F32),
        compiler_params=_params("parallel", "arbitrary"),
        name="nsa_decode_slc",
    )(page_table, q, k_new, v_new, sel, *([cache_kt] * npg), *([cache_vt] * npg))


def _nsa_decode_win_kernel(q_ref, kn_ref, vn_ref, sk_ref, sv_ref, ocmp_ref, oslc_ref, gate_ref,
                           o_ref, nk_ref, nv_ref, *, nq):
    nbuf = sk_ref.shape[1]
    pg = PAGE_SIZE
    q8 = _pad_rows(q_ref[0] * SCALE, SUBLANES)
    kn = _pad_rows(kn_ref[0], pg).astype(BF16)
    vn = _pad_rows(vn_ref[0], pg).astype(BF16)
    gates = _pad_rows(gate_ref[0], SUBLANES)
    gl = _iota(gates.shape, 1)
    rows = NSA_GROUP * SUBLANES
    t_r = _iota((rows, 1), 0) % SUBLANES
    dist_s = t_r + nbuf - _iota((1, nbuf), 1)
    mask_s = (dist_s >= 0) & (dist_s < WINDOW)
    lane_n = _iota((1, pg), 1)
    dist_n = t_r - lane_n
    mask_n = (dist_n >= 0) & (dist_n < WINDOW) & (lane_n < nq)
    pieces = [None] * NSA_HEADS
    for g in range(NSA_KV_HEADS):
        par = g % 2
        lanes = slice((g // 2) * LANES, (g // 2 + 1) * LANES)
        qs = _group_queries(q8, g, par)
        ss = jnp.where(mask_s, _dot_nt(qs, sk_ref[0, :, lanes].astype(BF16)), NEG_INF)
        sn = jnp.where(mask_n, _dot_nt(qs, kn[:, lanes]), NEG_INF)
        m = jnp.maximum(jnp.max(ss, axis=-1, keepdims=True), jnp.max(sn, axis=-1, keepdims=True))
        ps = jnp.where(mask_s, jnp.exp(ss - m), 0.0)
        pn = jnp.where(mask_n, jnp.exp(sn - m), 0.0)
        den = jnp.sum(ps, axis=-1, keepdims=True) + jnp.sum(pn, axis=-1, keepdims=True)
        ow = (_dot(ps.astype(BF16), sv_ref[0, :, lanes].astype(BF16)) + _dot(pn.astype(BF16), vn[:, lanes]))
        ow = ow / jnp.maximum(den, 1e-30)
        for j in range(NSA_GROUP):
            h = g * NSA_GROUP + j
            pieces[h] = _to_half(ow[j * SUBLANES:(j + 1) * SUBLANES], par, h % 2)
    o_win = _assemble_heads(pieces)
    head_of_lane = _iota((SUBLANES, o_win.shape[1]), 1) // HEAD_DIM
    branch_gate = []
    for c in range(3):
        gfull = jnp.zeros(o_win.shape, F32)
        for h in range(NSA_HEADS):
            col = jnp.sum(jnp.where(gl == c * NSA_HEADS + h, gates, 0.0), axis=1, keepdims=True)
            gfull = jnp.where(head_of_lane == h, col, gfull)
        branch_gate.append(gfull)
    o = branch_gate[0] * ocmp_ref[0] + branch_gate[1] * oslc_ref[0] + branch_gate[2] * o_win
    o_ref[0] = o[:nq]
    nk_ref[0, 0:nbuf - nq, :] = sk_ref[0, nq:nbuf, :]
    nk_ref[0, nbuf - nq:nbuf, :] = kn_ref[0]
    nv_ref[0, 0:nbuf - nq, :] = sv_ref[0, nq:nbuf, :]
    nv_ref[0, nbuf - nq:nbuf, :] = vn_ref[0]


def _nsa_decode_win(q, k_new, v_new, state_k, state_v, o_cmp, o_slc, gates):
    b, nq, dq = q.shape
    nbuf, gkv = state_k.shape[1:]
    assert nbuf == WINDOW and nq <= SUBLANES
    blk = lambda a: pl.BlockSpec((1,) + a.shape[1:], lambda i: (i, 0, 0))
    return pl.pallas_call(
        functools.partial(_nsa_decode_win_kernel, nq=nq),
        grid=(b,),
        in_specs=[blk(a) for a in (q, k_new, v_new, state_k, state_v, o_cmp, o_slc, gates)],
        out_specs=[blk(q), blk(state_k), blk(state_v)],
        out_shape=[jax.ShapeDtypeStruct(q.shape, F32), jax.ShapeDtypeStruct(state_k.shape, F32),
                   jax.ShapeDtypeStruct(state_v.shape, F32)],
        compiler_params=_params("parallel"),
        name="nsa_decode_win",
    )(q, k_new, v_new, state_k, state_v, o_cmp, o_slc, gates)


def _rope_tables(pos):
    half = HEAD_DIM // 2
    inv = ROPE_THETA ** (-jnp.arange(half, dtype=F32) / half)
    ang = pos.astype(F32)[:, None] * inv[None, :]
    cos, sin = jnp.cos(ang), jnp.sin(ang)
    reps = LANES // HEAD_DIM
    return (jnp.tile(jnp.concatenate([cos, cos], axis=1), (1, reps)),
            jnp.tile(jnp.concatenate([-sin, sin], axis=1), (1, reps)))


def _compress_weights(pe, w1, w2):
    ng = NSA_KV_HEADS
    gkv = ng * HEAD_DIM
    eye = jnp.eye(ng, dtype=F32)
    w1_bd = jnp.einsum('jde,gh->jgdhe', w1, eye).reshape(CMP_LEN, gkv, gkv).astype(BF16)
    return jnp.tile(pe, (1, ng)), w1_bd, jnp.kron(eye, w2).astype(BF16)


def _pad_cols(w, cols):
    return jnp.pad(w, ((0, 0), (0, cols - w.shape[1])))


def _transposed_pages(cache):
    l, pool, pg, nh, hd = cache.shape
    return jnp.transpose(cache, (0, 1, 3, 4, 2)).reshape(l, pool, nh * hd, pg)


TM_PROMPT_FFN = 1024
TM_PROMPT = 512


def kernel(x_prompt, x_sample, cache_fox_k, cache_fox_v, cache_fox_logf, cache_nsa_cmp_k, cache_nsa_cmp_v, cache_nsa_slc_k, cache_nsa_slc_v, state_nsa_win_k, state_nsa_win_v, page_table, norm_pre, norm_post, ffn1_w_in, ffn1_w_out, ffn2_w_in, ffn2_w_out, fox_w_in, fox_b_f, fox_w_out, nsa_w_in, nsa_cmp_pe_k, nsa_cmp_w1_k, nsa_cmp_w2_k, nsa_cmp_pe_v, nsa_cmp_w1_v, nsa_cmp_w2_v, nsa_w_out):
    n, t, d = x_prompt.shape
    b, nq, _ = x_sample.shape
    depth = norm_pre.shape[0]
    assert depth == 2 and fox_w_in.shape[0] == 1 and nsa_w_in.shape[0] == 1
    n_pages = page_table.shape[1]
    past_len = n_pages * PAGE_SIZE
    rs = b * nq
    row = lambda a: a.reshape(1, -1)
    xp = x_prompt.reshape(n * t, d)
    xs = x_sample.reshape(rs, d)

    def ffn_pair(xp, xs, layer, slot, w_in, w_out):
        wi, wo = w_in[layer].astype(BF16), w_out[layer].astype(BF16)
        gpre, gpost = row(norm_pre[layer, slot]), row(norm_post[layer, slot])
        return (_ffn(xp, gpre, gpost, wi, wo, tm=TM_PROMPT_FFN), _ffn(xs, gpre, gpost, wi, wo, tm=rs))

    xp, xs = ffn_pair(xp, xs, 0, 0, ffn1_w_in, ffn1_w_out)
    dh = FOX_HEADS * HEAD_DIM
    wqkv = fox_w_in[0][:, :3 * dh].astype(BF16)
    wf = _pad_cols(fox_w_in[0][:, 3 * dh:], LANES).astype(BF16)
    bf = row(fox_b_f[0])
    wout = fox_w_out[0].astype(BF16)
    gpre, gpost = row(norm_pre[0, 1]), row(norm_post[0, 1])

    qp, fox_kt_p, fox_vt_p, fox_lf_p, kb, vb = _fox_proj(xp, gpre, wqkv, wf, bf, tm=TM_PROMPT, seq_len=t)
    ccol, kaug = _cumsum(fox_lf_p.reshape(n, t, FOX_HEADS), dh)
    op = _fox_attn(qp.reshape(n, t, dh), kb, vb.reshape(n, t, dh), ccol, kaug)
    xp = _oproj(op.reshape(n * t, dh), xp, wout, gpost, tm=TM_PROMPT)

    qs, fox_k_s, fox_v_s, fox_lf_s = _fox_proj(xs, gpre, wqkv, wf, bf, tm=rs)
    os_ = _fox_decode(page_table, qs.reshape(b, nq, dh), fox_k_s.reshape(b, nq, dh), fox_v_s.reshape(b, nq, dh),
                      fox_lf_s.reshape(b, nq, FOX_HEADS),
                      _transposed_pages(cache_fox_k), _transposed_pages(cache_fox_v),
                      jnp.transpose(cache_fox_logf, (0, 1, 3, 2)), 0)
    xs = _oproj(os_.reshape(rs, dh), xs, wout, gpost, tm=rs)
    xp, xs = ffn_pair(xp, xs, 0, 2, ffn2_w_in, ffn2_w_out)

    xp, xs = ffn_pair(xp, xs, 1, 0, ffn1_w_in, ffn1_w_out)
    dq = NSA_HEADS * HEAD_DIM
    gkv = NSA_KV_HEADS * HEAD_DIM
    w = nsa_w_in[0]
    cut = lambda k: w[:, dq + k * gkv:dq + (k + 1) * gkv]
    wr = jnp.concatenate([w[:, :dq], cut(0), cut(2), cut(4)], axis=1).astype(BF16)
    wp = jnp.concatenate([cut(1), cut(3), cut(5)], axis=1).astype(BF16)
    wg = _pad_cols(w[:, dq + 6 * gkv:], LANES).astype(BF16)
    wout = nsa_w_out[0].astype(BF16)
    wk = _compress_weights(nsa_cmp_pe_k[0], nsa_cmp_w1_k[0], nsa_cmp_w2_k[0])
    wv = _compress_weights(nsa_cmp_pe_v[0], nsa_cmp_w1_v[0], nsa_cmp_w2_v[0])
    gpre, gpost = row(norm_pre[1, 1]), row(norm_post[1, 1])

    cos_p, sin_p = _rope_tables(jnp.arange(t, dtype=jnp.int32))
    q, gates, *kv = _nsa_proj(xp, gpre, wr, wp, wg, cos_p, sin_p, tm=TM_PROMPT, seq_len=t)
    nsa_t_p = kv[:6]
    kc, vc, ksb, vsb, kwb, vwb = kv[6:]
    seq = lambda a: a.reshape(n, t, a.shape[1])
    kcmp = _compress(seq(kc), *wk)
    vcmp = _compress(seq(vc), *wv)
    op = _nsa_attn(seq(q), kcmp, vcmp, seq(ksb), seq(vsb), seq(kwb), seq(vwb), seq(gates))
    xp = _oproj(op.reshape(n * t, dq), xp, wout, gpost, tm=TM_PROMPT)

    cos_s, sin_s = _rope_tables(past_len + jnp.arange(nq, dtype=jnp.int32))
    cos_s, sin_s = jnp.tile(cos_s, (b, 1)), jnp.tile(sin_s, (b, 1))
    q, gates, kc, vc, ks, vs, kw, vw = _nsa_proj(xs, gpre, wr, wp, wg, cos_s, sin_s, tm=rs)
    tok = lambda a: a.reshape(b, nq, a.shape[1])
    n_sel = -(-(past_len + nq) // SLC_BLOCK)
    nsp = -(-n_sel // LANES) * LANES
    o_cmp, sel = _nsa_decode_cmp(page_table, tok(q), _transposed_pages(cache_nsa_cmp_k),
                                 _transposed_pages(cache_nsa_cmp_v), wk, wv, 0, nsp=nsp)
    o_slc = _nsa_decode_slc(page_table, tok(q), tok(ks), tok(vs), sel,
                            _transposed_pages(cache_nsa_slc_k), _transposed_pages(cache_nsa_slc_v), 0)
    nbuf = state_nsa_win_k.shape[2]
    os_, win_k_s, win_v_s = _nsa_decode_win(tok(q), tok(kw), tok(vw), state_nsa_win_k.reshape(b, nbuf, gkv),
                                            state_nsa_win_v.reshape(b, nbuf, gkv), o_cmp, o_slc, tok(gates))
    xs = _oproj(os_.reshape(rs, dq), xs, wout, gpost, tm=rs)
    nsa_s = (kc, vc, ks, vs)
    xp, xs = ffn_pair(xp, xs, 1, 2, ffn2_w_in, ffn2_w_out)

    keep_p = min(WINDOW, t)
    fox_shape = lambda a, m, r: a.reshape(1, m, r, FOX_HEADS, HEAD_DIM)
    nsa_shape = lambda a, m, r: a.reshape(1, m, r, NSA_KV_HEADS, HEAD_DIM)

    def token_major(a, heads):
        return jnp.transpose(a.reshape(1, n, heads, HEAD_DIM, a.shape[2]), (0, 1, 4, 2, 3))

    win_p = lambda a: token_major(a[:, :, t - keep_p:], NSA_KV_HEADS)
    return (xp.reshape(n, t, d), xs.reshape(b, nq, d),
            token_major(fox_kt_p, FOX_HEADS), token_major(fox_vt_p, FOX_HEADS), fox_lf_p.reshape(1, n, t, FOX_HEADS),
            fox_shape(fox_k_s, b, nq), fox_shape(fox_v_s, b, nq), fox_lf_s.reshape(1, b, nq, FOX_HEADS),
            token_major(nsa_t_p[0], NSA_KV_HEADS), token_major(nsa_t_p[1], NSA_KV_HEADS),
            token_major(nsa_t_p[2], NSA_KV_HEADS), token_major(nsa_t_p[3], NSA_KV_HEADS),
            win_p(nsa_t_p[4]), win_p(nsa_t_p[5]),
            nsa_shape(nsa_s[0], b, nq), nsa_shape(nsa_s[1], b, nq), nsa_shape(nsa_s[2], b, nq), nsa_shape(nsa_s[3], b, nq),
            nsa_shape(win_k_s, b, nbuf), nsa_shape(win_v_s, b, nbuf))
```
